```python
import math
import jax, jax.numpy as jnp
from jax import lax
import numpy as np

D_MODEL = 1024
BATCH = 8
SEQ = 8192
DEPTH = 2
DEC_BATCH = 32
DEC_SEQ = 32
PAST_LEN = 2048

CHUNK = 64
N_HEADS = 4
HEAD_V = 64
GROUP_W = N_HEADS * HEAD_V
MIX_W = 4 * GROUP_W
DN_DK = 64
DN_CONV = 4
DN_CONV_CH = 2 * N_HEADS * DN_DK + GROUP_W
HG_DK = 64
GLA_DK = 32
GLA_RANK = 16
GLA_TAU = 16.0
ML_DK = 64
D_FF = -(-8 * D_MODEL // (3 * 256)) * 256
EPS = 1e-6

IN_SPLITS = (
    DN_CONV_CH, N_HEADS, N_HEADS, GROUP_W,
    N_HEADS * HG_DK, N_HEADS * HG_DK, GROUP_W, GROUP_W,
    N_HEADS * GLA_DK, N_HEADS * GLA_DK, GROUP_W, GLA_RANK, GROUP_W,
    N_HEADS * ML_DK, N_HEADS * ML_DK, GROUP_W, N_HEADS, N_HEADS, GROUP_W,
)

kernel_name = "hymba_style_streaming_recurrent_encoder_step"

F32 = jnp.float32


def split_cols(z, sizes):
    idx = np.cumsum(sizes)[:-1].tolist()
    return jnp.split(z, idx, axis=-1)


def rmsnorm(x, g):
    xf = x.astype(F32)
    y = xf * lax.rsqrt(jnp.mean(xf * xf, axis=-1, keepdims=True) + EPS)
    return (y * g.astype(F32)).astype(x.dtype)


def l2norm(x):
    xf = x.astype(F32)
    return xf * lax.rsqrt(jnp.sum(xf * xf, axis=-1, keepdims=True) + EPS)


def to_chunks(a, c):
    b, t, h, d = a.shape
    return a.astype(F32).reshape(b, t // c, c, h, d).transpose(1, 0, 3, 2, 4)


def from_chunks(a):
    nc, b, h, c, d = a.shape
    return a.transpose(1, 0, 3, 2, 4).reshape(b, nc * c, h, d)


def gla_chunked(q, k, v, g, s0, c):
    qc, kc, vc, gc = (to_chunks(a, c) for a in (q, k, v, g))
    bcum = jnp.cumsum(gc, axis=3)
    incl = jnp.tril(jnp.ones((c, c), bool))

    def step(s, blk):
        qb, kb, vb, bb = blk
        diff = bb[..., :, None, :] - bb[..., None, :, :]
        decay = jnp.exp(jnp.where(incl[:, :, None], diff, -jnp.inf))
        attn = jnp.einsum('bhtk,bhsk,bhtsk->bhts', qb, kb, decay)
        o = jnp.einsum('bhts,bhsv->bhtv', attn, vb) + jnp.einsum('bhtk,bhkv->bhtv', qb * jnp.exp(bb), s)
        last = bb[..., -1:, :]
        s_new = jnp.exp(last[..., 0, :])[..., None] * s + jnp.einsum(
            'bhsk,bhsv->bhkv', kb * jnp.exp(last - bb), vb)
        return s_new, o

    s_fin, o = lax.scan(step, s0.astype(F32), (qc, kc, vc, bcum))
    return from_chunks(o), s_fin


def gated_delta_chunked(q, k, v, beta, g, s0, c):
    qc, kc, vc, bc, gc = (to_chunks(a, c) for a in (q, k, v, beta, g))
    gam = jnp.cumsum(gc, axis=3)
    gv = gam[..., 0]
    gts = gv[..., :, None] - gv[..., None, :]
    incl = jnp.tril(jnp.ones((c, c), bool))
    strict = jnp.tril(jnp.ones((c, c), bool), k=-1)
    dec_strict = jnp.exp(jnp.where(strict, gts, -jnp.inf))
    dec_incl = jnp.exp(jnp.where(incl, gts, -jnp.inf))
    m = bc * jnp.einsum('nbhtk,nbhsk->nbhts', kc, kc) * dec_strict
    a = m + jnp.eye(c, dtype=F32)
    w = lax.linalg.triangular_solve(a, bc * jnp.exp(gam) * kc, left_side=True, lower=True,
                                    unit_diagonal=True)
    u0 = lax.linalg.triangular_solve(a, bc * vc, left_side=True, lower=True,
                                     unit_diagonal=True)
    qk = jnp.einsum('nbhtk,nbhsk->nbhts', qc, kc) * dec_incl

    def step(s, blk):
        qb, kb, wb, ub, qkb, gb = blk
        u = ub - jnp.einsum('bhtk,bhkv->bhtv', wb, s)
        o = jnp.einsum('bhtk,bhkv->bhtv', qb * jnp.exp(gb), s) + jnp.einsum('bhts,bhsv->bhtv', qkb, u)
        gl = gb[..., -1:, :]
        s_new = jnp.exp(gl) * s + jnp.einsum('bhsk,bhsv->bhkv', kb * jnp.exp(gl - gb), u)
        return s_new, o

    s_fin, o = lax.scan(step, s0.astype(F32), (qc, kc, w, u0, qk, gam))
    return from_chunks(o), s_fin


def mlstm_chunked(q, k, v, ig, lf, c0, n0, m0, c):
    qc, kc, vc, ic, fc = (to_chunks(a, c) for a in (q, k, v, ig, lf))
    fcum = jnp.cumsum(fc, axis=3)[..., 0]
    iv = ic[..., 0]
    incl = jnp.tril(jnp.ones((c, c), bool))
    dlog = jnp.where(incl, fcum[..., :, None] - fcum[..., None, :] + iv[..., None, :], -jnp.inf)
    imax = jnp.max(dlog, axis=-1)
    qk = jnp.einsum('nbhtk,nbhsk->nbhts', qc, kc)

    def step(carry, blk):
        cs, ns, ms = carry
        qb, kb, vb, fb, ib, dl, im, qkb = blk
        inter = fb + ms[..., None]
        mt = jnp.maximum(inter, im)
        w_intra = jnp.exp(dl - mt[..., None]) * qkb
        w_inter = jnp.exp(inter - mt)
        num = jnp.einsum('bhts,bhsv->bhtv', w_intra, vb) + w_inter[..., None] * jnp.einsum(
            'bhtk,bhkv->bhtv', qb, cs)
        den = jnp.sum(w_intra, axis=-1) + w_inter * jnp.einsum('bhtk,bhk->bht', qb, ns)
        h = num / jnp.maximum(jnp.abs(den), jnp.exp(-mt))[..., None]
        fl = fb[..., -1]
        lw = fl[..., None] - fb + ib
        m_new = jnp.maximum(fl + ms, jnp.max(lw, axis=-1))
        ws = jnp.exp(lw - m_new[..., None])
        wc = jnp.exp(fl + ms - m_new)
        c_new = wc[..., None, None] * cs + jnp.einsum('bhs,bhsk,bhsv->bhkv', ws, kb, vb)
        n_new = wc[..., None] * ns + jnp.einsum('bhs,bhsk->bhk', ws, kb)
        return (c_new, n_new, m_new), h

    (c_fin, n_fin, m_fin), h = lax.scan(
        step, (c0.astype(F32), n0.astype(F32), m0.astype(F32)), (qc, kc, vc, fcum, iv, dlog, imax, qk))
    return from_chunks(h), c_fin, n_fin, m_fin


def token_mixers(h, st, p, l, lb):
    conv_buf, s_dn, s_hg, s_gla, c_ml, n_ml, m_ml = st
    B, T, _ = h.shape
    cs = min(CHUNK, T)
    z = jnp.einsum('btd,de->bte', h, p['w_in'][l])
    (dn_qkv, dn_b, dn_a, dn_g, hg_q, hg_f, hg_i, hg_g, gla_q, gla_k, gla_v, gla_r, gla_g,
     ml_q, ml_k, ml_v, ml_i, ml_f, ml_o) = split_cols(z, IN_SPLITS)

    def heads(a):
        return a.reshape(B, T, N_HEADS, -1)

    xp = jnp.concatenate([conv_buf.astype(z.dtype), dn_qkv], axis=1)
    wcv = p['dn_conv_w'][l]
    conv = xp[:, 0:T] * wcv[0]
    for j in range(1, DN_CONV):
        conv = conv + xp[:, j:j + T] * wcv[j]
    conv = jax.nn.silu(conv)
    new_conv = xp[:, T:]
    dq, dk, dv = split_cols(conv, (N_HEADS * DN_DK, N_HEADS * DN_DK, GROUP_W))
    dq = l2norm(heads(dq)) * DN_DK ** -0.5
    dk = l2norm(heads(dk))
    beta = jax.nn.sigmoid(dn_b.astype(F32))[..., None]
    g_dn = -jnp.exp(p['dn_a_log'][l].astype(F32)) * jax.nn.softplus(
        dn_a.astype(F32) + p['dn_dt_bias'][l].astype(F32))
    o_dn, s_dn_new = gated_delta_chunked(dq, dk, heads(dv), beta, g_dn[..., None], s_dn, cs)
    o_dn = rmsnorm(o_dn, p['dn_norm_g'][l]) * jax.nn.silu(heads(dn_g).astype(F32))

    zf = hg_f.astype(F32)
    lbf = lb.astype(F32)
    log_f = jnp.logaddexp(jnp.log(lbf), jnp.log1p(-lbf) + jax.nn.log_sigmoid(zf))
    key_hg = (1.0 - lbf) * jax.nn.sigmoid(-zf)
    o_hg, s_hg_new = gla_chunked(heads(jax.nn.silu(hg_q.astype(F32))), heads(key_hg),
                                 heads(hg_i), heads(log_f), s_hg, cs)
    o_hg = rmsnorm(o_hg, p['hg_norm_g'][l]) * jax.nn.silu(heads(hg_g).astype(F32))

    g_gla = jax.nn.log_sigmoid(jnp.einsum('btr,rk->btk', gla_r.astype(F32), p['gla_w_up'][l].astype(F32))
                               + p['gla_b_up'][l].astype(F32)) / GLA_TAU
    o_gla, s_gla_new = gla_chunked(heads(gla_q) * GLA_DK ** -0.5, heads(gla_k), heads(gla_v),
                                   heads(g_gla), s_gla, cs)
    o_gla = rmsnorm(o_gla, p['gla_norm_g'][l]) * jax.nn.silu(heads(gla_g).astype(F32))

    ig = (ml_i.astype(F32) + p['ml_i_bias'][l].astype(F32))[..., None]
    lf = jax.nn.log_sigmoid(ml_f.astype(F32) + p['ml_f_bias'][l].astype(F32))[..., None]
    h_ml, c_ml_new, n_ml_new, m_ml_new = mlstm_chunked(heads(ml_q) * ML_DK ** -0.5, heads(ml_k),
                                                       heads(ml_v), ig, lf, c_ml, n_ml, m_ml, cs)
    o_ml = rmsnorm(h_ml, p['ml_norm_g'][l]) * jax.nn.sigmoid(heads(ml_o).astype(F32))

    mixed = jnp.concatenate([o.reshape(B, T, GROUP_W) for o in (o_dn, o_hg, o_gla, o_ml)], axis=-1)
    out = jnp.einsum('bte,ed->btd', mixed.astype(h.dtype), p['w_out'][l])
    new = (new_conv, s_dn_new, s_hg_new, s_gla_new, c_ml_new, n_ml_new, m_ml_new)
    return out, new


def swiglu(h, w_up, w_down):
    gate, up = jnp.split(jnp.einsum('btd,df->btf', h, w_up), 2, axis=-1)
    return jnp.einsum('btf,fd->btd', jax.nn.silu(gate) * up, w_down)


def trunk(x, c, st, p):
    lb_all = jnp.cumsum(jax.nn.softmax(p['hg_lb_logits'].astype(F32), axis=0), axis=0)
    lb_all = lb_all - lb_all[0]
    outs = [[] for _ in range(len(st))]
    for l in range(DEPTH):
        mod = jnp.einsum('bd,de->be', jax.nn.silu(c), p['w_ada'][l]) + p['b_ada'][l]
        sh1, sc1, gt1, sh2, sc2, gt2 = jnp.split(mod[:, None, :], 6, axis=-1)
        h = rmsnorm(x, p['g_mix'][l]) * (1 + sc1) + sh1
        mix, new = token_mixers(h, tuple(s[l] for s in st), p, l, lb_all[l])
        x = x + gt1 * mix
        h = rmsnorm(x, p['g_ffn'][l]) * (1 + sc2) + sh2
        x = x + gt2 * swiglu(h, p['w_up'][l], p['w_down'][l])
        for o, s in zip(outs, new):
            o.append(s)
    y = rmsnorm(x, p['g_final'])
    return y, tuple(jnp.stack(o) for o in outs)


def zero_states(b, dtype):
    return (jnp.zeros((DEPTH, b, DN_CONV - 1, DN_CONV_CH), dtype),
            jnp.zeros((DEPTH, b, N_HEADS, DN_DK, HEAD_V), F32),
            jnp.zeros((DEPTH, b, N_HEADS, HG_DK, HEAD_V), F32),
            jnp.zeros((DEPTH, b, N_HEADS, GLA_DK, HEAD_V), F32),
            jnp.zeros((DEPTH, b, N_HEADS, ML_DK, HEAD_V), F32),
            jnp.zeros((DEPTH, b, N_HEADS, ML_DK), F32),
            jnp.zeros((DEPTH, b, N_HEADS), F32))


def setup_inputs(seed: int = 0) -> dict:
    key = jax.random.key(seed)
    ks = jax.random.split(key, 32)
    H = N_HEADS
    n_in = sum(IN_SPLITS)

    def nrm(k, shape, s):
        return s * jax.random.normal(k, shape, F32)

    dt = jnp.exp(jax.random.uniform(ks[18], (DEPTH, H), F32, math.log(1e-3), math.log(1e-1)))
    return {
        'x_prompt': nrm(ks[0], (BATCH, SEQ, D_MODEL), 1.0),
        'x_sample': nrm(ks[1], (DEC_BATCH, DEC_SEQ, D_MODEL), 1.0),
        'c_prompt': nrm(ks[2], (BATCH, D_MODEL), 1.0),
        'c_sample': nrm(ks[3], (DEC_BATCH, D_MODEL), 1.0),
        'cache_dn_conv': nrm(ks[4], (DEPTH, DEC_BATCH, DN_CONV - 1, DN_CONV_CH), 1.0),
        'state_dn': nrm(ks[5], (DEPTH, DEC_BATCH, H, DN_DK, HEAD_V), 0.3),
        'state_hgrn': nrm(ks[6], (DEPTH, DEC_BATCH, H, HG_DK, HEAD_V), 0.5),
        'state_gla': nrm(ks[7], (DEPTH, DEC_BATCH, H, GLA_DK, HEAD_V), 0.5),
        'state_mlstm_c': nrm(ks[8], (DEPTH, DEC_BATCH, H, ML_DK, HEAD_V), 0.5),
        'state_mlstm_n': nrm(ks[9], (DEPTH, DEC_BATCH, H, ML_DK), 0.5),
        'state_mlstm_m': nrm(ks[10], (DEPTH, DEC_BATCH, H), 1.0),
        'w_ada': nrm(ks[11], (DEPTH, D_MODEL, 6 * D_MODEL), 0.5 * D_MODEL ** -0.5),
        'b_ada': nrm(ks[12], (DEPTH, 6 * D_MODEL), 0.01),
        'g_mix': 1.0 + nrm(ks[13], (DEPTH, D_MODEL), 0.01),
        'g_ffn': 1.0 + nrm(ks[14], (DEPTH, D_MODEL), 0.01),
        'w_in': nrm(ks[15], (DEPTH, D_MODEL, n_in), D_MODEL ** -0.5),
        'dn_conv_w': nrm(ks[16], (DEPTH, DN_CONV, DN_CONV_CH), DN_CONV ** -0.5),
        'dn_a_log': jnp.log(jax.random.uniform(ks[17], (DEPTH, H), F32, 1.0, 16.0)),
        'dn_dt_bias': dt + jnp.log(-jnp.expm1(-dt)),
        'dn_norm_g': 1.0 + nrm(ks[19], (DEPTH, HEAD_V), 0.01),
        'hg_lb_logits': nrm(ks[20], (DEPTH, H * HG_DK), 0.1),
        'hg_norm_g': 1.0 + nrm(ks[21], (DEPTH, HEAD_V), 0.01),
        'gla_w_up': nrm(ks[22], (DEPTH, GLA_RANK, H * GLA_DK), GLA_RANK ** -0.5),
        'gla_b_up': nrm(ks[23], (DEPTH, H * GLA_DK), 0.1),
        'gla_norm_g': 1.0 + nrm(ks[24], (DEPTH, HEAD_V), 0.01),
        'ml_i_bias': nrm(ks[25], (DEPTH, H), 0.1),
        'ml_f_bias': 3.0 + nrm(ks[26], (DEPTH, H), 0.5),
        'ml_norm_g': 1.0 + nrm(ks[27], (DEPTH, HEAD_V), 0.01),
        'w_out': nrm(ks[28], (DEPTH, MIX_W, D_MODEL), MIX_W ** -0.5),
        'w_up': nrm(ks[29], (DEPTH, D_MODEL, 2 * D_FF), D_MODEL ** -0.5),
        'w_down': nrm(ks[30], (DEPTH, D_FF, D_MODEL), D_FF ** -0.5),
        'g_final': 1.0 + nrm(ks[31], (D_MODEL,), 0.01),
    }


def reference(x_prompt, x_sample, c_prompt, c_sample, cache_dn_conv, state_dn, state_hgrn, state_gla,
              state_mlstm_c, state_mlstm_n, state_mlstm_m, w_ada, b_ada, g_mix, g_ffn, w_in, dn_conv_w,
              dn_a_log, dn_dt_bias, dn_norm_g, hg_lb_logits, hg_norm_g, gla_w_up, gla_b_up, gla_norm_g,
              ml_i_bias, ml_f_bias, ml_norm_g, w_out, w_up, w_down, g_final):
    p = dict(w_ada=w_ada, b_ada=b_ada, g_mix=g_mix, g_ffn=g_ffn, w_in=w_in, dn_conv_w=dn_conv_w,
             dn_a_log=dn_a_log, dn_dt_bias=dn_dt_bias, dn_norm_g=dn_norm_g, hg_lb_logits=hg_lb_logits,
             hg_norm_g=hg_norm_g, gla_w_up=gla_w_up, gla_b_up=gla_b_up, gla_norm_g=gla_norm_g,
             ml_i_bias=ml_i_bias, ml_f_bias=ml_f_bias, ml_norm_g=ml_norm_g, w_out=w_out, w_up=w_up,
             w_down=w_down, g_final=g_final)
    y_prompt, p_st = trunk(x_prompt, c_prompt, zero_states(x_prompt.shape[0], x_prompt.dtype), p)
    s_in = (cache_dn_conv, state_dn, state_hgrn, state_gla, state_mlstm_c, state_mlstm_n, state_mlstm_m)
    y_sample, s_st = trunk(x_sample, c_sample, s_in, p)
    p_dn_conv, p_dn, p_hgrn, p_gla, p_mlstm_c, p_mlstm_n, p_mlstm_m = p_st
    s_dn_conv, s_dn, s_hgrn, s_gla, s_mlstm_c, s_mlstm_n, s_mlstm_m = s_st
    return (y_prompt, y_sample,
            p_dn_conv, p_dn, p_hgrn, p_gla, p_mlstm_c, p_mlstm_n, p_mlstm_m,
            s_dn_conv, s_dn, s_hgrn, s_gla, s_mlstm_c, s_mlstm_n, s_mlstm_m)
```

```python
import functools

import numpy as np
import jax
import jax.numpy as jnp
from jax import lax
from jax.experimental import pallas as pl
from jax.experimental.pallas import tpu as pltpu

F32 = jnp.float32
BF16 = jnp.bfloat16

D_MODEL = 1024
DEPTH = 2
CHUNK = 64
N_HEADS = 4
HEAD_V = 64
GROUP_W = N_HEADS * HEAD_V
MIX_W = 4 * GROUP_W
DN_DK = 64
DN_CONV = 4
DN_CONV_CH = 3 * GROUP_W
HG_DK = 64
GLA_DK = 32
GLA_W = N_HEADS * GLA_DK
GLA_RANK = 16
GLA_TAU = 16.0
ML_DK = 64
D_FF = 2816
EPS = 1e-6

Z_DNQKV = 0
Z_DNG = 768
Z_HGQ = 1024
Z_HGF = 1280
Z_HGI = 1536
Z_HGG = 1792
Z_GLAQ = 2048
Z_GLAK = 2176
Z_GLAV = 2304
Z_GLAG = 2560
Z_MLQ = 2816
Z_MLK = 3072
Z_MLV = 3328
Z_MLO = 3584
Z_SMALL = 3840
N_IN = 3968
S_DNB = 0
S_GLAR = 16
S_DNA = 32
S_MLI = 64
S_MLF = 96

SUB = 8
VMEM_LIMIT = 56 * 1024 * 1024

M_INCL, M_STRICT, M_DIAG, M_SAME16 = 0, 1, 2, 3
M_LEVEL0 = 4
SUBBLK = 8


def _sigmoid(x):
    return 1.0 / (1.0 + jnp.exp(-x))


def _silu(x):
    return x * _sigmoid(x)


def _log_sigmoid(x):
    return jnp.minimum(x, 0.0) - jnp.log1p(jnp.exp(-jnp.abs(x)))


def _softplus(x):
    return jnp.maximum(x, 0.0) + jnp.log1p(jnp.exp(-jnp.abs(x)))


def _dot(a, b):
    return jnp.dot(a.astype(BF16), b.astype(BF16), preferred_element_type=F32)


def _dot_nt(a, b):
    return lax.dot_general(a.astype(BF16), b.astype(BF16), (((1,), (1,)), ((), ())),
                           preferred_element_type=F32)


def _dot_tn(a, b):
    return lax.dot_general(a.astype(BF16), b.astype(BF16), (((0,), (0,)), ((), ())),
                           preferred_element_type=F32)


def _split3(x):
    x1 = x.astype(BF16)
    r = x - x1.astype(F32)
    x2 = r.astype(BF16)
    r = r - x2.astype(F32)
    return x1, x2, r.astype(BF16)


def _move_r(x, sel):
    x1, x2, x3 = _split3(x)
    d = lambda a: jnp.dot(a, sel, preferred_element_type=F32)
    return d(x1) + (d(x2) + d(x3))


def _move_l(sel, x):
    x1, x2, x3 = _split3(x)
    d = lambda a: jnp.dot(sel, a, preferred_element_type=F32)
    return d(x1) + (d(x2) + d(x3))


def _bd(x, mask):
    xb = x.astype(BF16)
    return jnp.concatenate([xb] * N_HEADS, axis=0) * mask


def _scan0(x, op, fill):
    n = x.shape[0]
    row = lax.broadcasted_iota(jnp.int32, x.shape, 0)
    sh = 1
    while sh < n:
        r = pltpu.roll(x, sh, axis=0)
        x = op(x, jnp.where(row >= sh, r, fill))
        sh *= 2
    return x


def _rowform(xe, diag):
    return jnp.sum(xe * diag, axis=0, keepdims=True)


def _rmsnorm_rows(x, g):
    return x * lax.rsqrt(jnp.mean(x * x, axis=-1, keepdims=True) + EPS) * g


def _head_norm_gate(o, ones_bd, g_row, gate):
    ms = _dot(o * o, ones_bd) * (1.0 / HEAD_V)
    return o * lax.rsqrt(ms + EPS) * g_row * gate


def _tri_inverse(mm, masks, bdp, c):
    eye = masks[M_DIAG]
    mul = lambda a, b: _dot(a, _bd(b, bdp))
    md = mm * masks[M_SAME16]
    mo = mm - md
    p2 = mul(md, md)
    p4 = mul(p2, p2)
    p8 = mul(p4, p4)
    d = eye - md
    d = d + mul(d, p2)
    d = d + mul(d, p4)
    d = d + mul(d, p8)
    n = mul(d, mo)
    r = eye - n
    nlev = c // 16
    if nlev > 2:
        n2 = mul(n, n)
        r = r + mul(r, n2)
        if nlev > 4:
            r = r + mul(r, mul(n2, n2))
    return mul(r, d)


def _deltanet_chunk(q, k, v, beta_s, gam_s, st, kc, c):
    masks = kc['masks']
    gam_e = _move_r(gam_s, kc['e_c'][1])
    beta_e = _move_r(beta_s, kc['e_c'][0])
    if c == HEAD_V:
        gam_d, beta_d = gam_e, beta_e
    else:
        gam_d = _move_r(gam_s, kc['e_d'][1])
        beta_d = _move_r(beta_s, kc['e_d'][0])
    gam_r = _rowform(gam_e, masks[M_DIAG])
    dec = jnp.exp(jnp.minimum(gam_e - gam_r, 0.0))
    kb = _bd(k, kc['bd256'][...])
    kk = _dot_nt(k, kb)
    qk = _dot_nt(q, kb)
    mm = beta_e * kk * dec * masks[M_STRICT]
    t_inv = _tri_inverse(mm, masks, kc['bdp'][...], c)
    eg = jnp.exp(gam_d)
    w = _dot(t_inv, _bd(beta_d * eg * k, kc['bd256'][...]))
    u0 = _dot(t_inv, _bd(beta_d * v, kc['bd256'][...]))
    u = u0 - _dot_nt(w, st)
    o = _dot_nt(q * eg, st) + _dot(qk * dec * masks[M_INCL], _bd(u, kc['bd256'][...]))
    gl = gam_d[c - 1:c, :]
    st_new = jnp.exp(gl) * st + kc['st256'][...] * _dot_tn(u, k * jnp.exp(gl - gam_d))
    return o, st_new


def _gla_chunk(q, k, v, g, st, kc, c, wide):
    masks = kc['masks']
    bdk = kc['bd256'][...] if wide else kc['bd128'][...]
    ie = kc['ie256'][...] if wide else kc['ie128'][...]
    stm = kc['st256'][...] if wide else kc['st128'][...]
    b = _scan0(g, jnp.add, 0.0)
    attn = jnp.zeros((c, N_HEADS * c), F32)
    sz = c // 2
    lvl = 0
    row = lax.broadcasted_iota(jnp.int32, b.shape, 0)
    while sz >= SUBBLK:
        ref = _move_l(kc['sel'][lvl], b)
        right = jnp.bitwise_and(jnp.right_shift(row, sz.bit_length() - 1), 1) == 1
        qd = jnp.where(right, q * jnp.exp(jnp.minimum(b - ref, 0.0)), 0.0)
        kd = jnp.where(right, 0.0, k * jnp.exp(jnp.minimum(ref - b, 0.0)))
        attn = attn + _dot_nt(qd, _bd(kd, bdk)) * masks[M_LEVEL0 + lvl]
        sz //= 2
        lvl += 1
    for j in range(SUBBLK):
        if j == 0:
            x = q * k
        else:
            ks = pltpu.roll(k, j, axis=0)
            bs = pltpu.roll(b, j, axis=0)
            x = q * ks * jnp.exp(jnp.minimum(b - bs, 0.0))
        attn = attn + _dot(x, ie) * masks[M_LEVEL0 + lvl + j]
    o = _dot(attn, _bd(v, kc['bd256'][...])) + _dot_nt(q * jnp.exp(b), st)
    last = b[c - 1:c, :]
    st_new = jnp.exp(last) * st + stm * _dot_tn(v, k * jnp.exp(last - b))
    return o, st_new


def _mlstm_chunk(q, k, v, ig_s, lf_s, ct, n_row, m_row, kc, c):
    masks = kc['masks']
    fcum = _scan0(lf_s, jnp.add, 0.0)
    a = ig_s - fcum
    imax = fcum + _scan0(a, jnp.maximum, -jnp.inf)
    inter = fcum + m_row
    mt = jnp.maximum(inter, imax)
    x1_e = _move_r(fcum - mt, kc['e_c'][2])
    a_r = _rowform(_move_r(a, kc['e_c'][2]), masks[M_DIAG])
    kb = _bd(k, kc['bd256'][...])
    qk = _dot_nt(q, kb)
    w_intra = jnp.exp(jnp.minimum(x1_e + a_r, 0.0)) * masks[M_INCL] * qk
    w_inter = jnp.exp(_move_r(inter - mt, kc['e_d'][2]))
    floor = jnp.exp(_move_r(-mt, kc['e_d'][2]))
    num = _dot(w_intra, _bd(v, kc['bd256'][...])) + w_inter * _dot_nt(q, ct)
    den = _dot(w_intra, kc['iep'][...]) + w_inter * _dot(q * n_row, kc['ones256'][...])
    hh = num / jnp.maximum(jnp.abs(den), floor)
    fl = fcum[c - 1:c, :]
    lw = fl - fcum + ig_s
    m_new = jnp.maximum(fl + m_row, jnp.max(lw, axis=0, keepdims=True))
    ws = jnp.exp(_move_r(lw - m_new, kc['e_d'][2]))
    wc8 = jnp.broadcast_to(fl + m_row - m_new, (SUB, 128))
    wc = jnp.exp(_move_r(wc8, kc['e_d'][2]))[0:1, :]
    ct_new = wc * ct + kc['st256'][...] * _dot_tn(ws * v, k)
    n_new = wc * n_row + jnp.sum(ws * k, axis=0, keepdims=True)
    return hh, ct_new, n_new, m_new


def _mixer_kernel(x_ref, mod_ref, gmix_ref, win_ref, wout_ref, convw_ref, sp_ref, gn_ref, lbl_ref, lbs_ref,
                  wup_ref, bup_ref,
                  masks_ref, ec_ref, ed_ref, bd256_ref, bd128_ref, bdp_ref, st256_ref, st128_ref,
                  ie256_ref, ie128_ref, iep_ref, ones256_ref, sel_ref,
                  conv0_ref, sdn0_ref, shg0_ref, sgla0_ref, c0_ref, n0_ref, m0_ref,
                  y_ref, convo_ref, sdn_ref, shg_ref, sgla_ref, cml_ref, nml_ref, mml_ref,
                  z_ref, xp_ref, qkv_ref, mix_ref, *, nb, tb, c):
    ti = pl.program_id(1)
    rows = nb * tb
    nchunk = tb // c

    @pl.when(ti == 0)
    def _():
        convo_ref[...] = conv0_ref[...]
        sdn_ref[...] = sdn0_ref[...]
        shg_ref[...] = shg0_ref[...]
        sgla_ref[...] = sgla0_ref[...]
        cml_ref[...] = c0_ref[...]
        nml_ref[...] = n0_ref[...]
        mml_ref[...] = m0_ref[...]

    x = x_ref[...]
    mod = mod_ref[...]
    h = _rmsnorm_rows(x, gmix_ref[...]) * (1.0 + mod[:, 1:2, :]) + mod[:, 0:1, :]
    z_ref[...] = jnp.dot(h.reshape(rows, D_MODEL).astype(BF16), win_ref[...], preferred_element_type=F32)

    convw = convw_ref[...]
    for b in range(nb):
        xp_ref[b, SUB - (DN_CONV - 1):SUB, :] = convo_ref[b]
        xp_ref[b, SUB:SUB + tb, :] = z_ref[b * tb:(b + 1) * tb, Z_DNQKV:Z_DNQKV + DN_CONV_CH]
        acc = xp_ref[b, SUB - 3:SUB - 3 + tb, :] * convw[0:1, :]
        for j in range(1, DN_CONV):
            acc = acc + xp_ref[b, SUB - 3 + j:SUB - 3 + j + tb, :] * convw[j:j + 1, :]
        qkv_ref[b * tb:(b + 1) * tb, :] = _silu(acc)
        convo_ref[b] = xp_ref[b, SUB + tb - (DN_CONV - 1):SUB + tb, :]

    kc = dict(masks=masks_ref, e_c=ec_ref, e_d=ed_ref, bd256=bd256_ref, bd128=bd128_ref, bdp=bdp_ref,
              st256=st256_ref, st128=st128_ref, ie256=ie256_ref, ie128=ie128_ref, iep=iep_ref,
              ones256=ones256_ref, sel=sel_ref)
    sp = sp_ref[...]
    gn = gn_ref[...]
    lbl = lbl_ref[...]
    lbs = lbs_ref[...]
    lbe = jnp.exp(lbl - jnp.max(lbl, axis=0, keepdims=True))
    lb = jnp.sum(lbs * (lbe / jnp.sum(lbe, axis=0, keepdims=True)), axis=0, keepdims=True)
    log_lb = jnp.log(lb)
    log_1mlb = jnp.log1p(-lb)
    neg_a = -jnp.exp(sp[1:2, :])

    def chunk_body(i, carry):
        if nchunk == 1:
            b = i
        elif nb == 1:
            b = 0
        else:
            b = i // nchunk
        r0 = pl.multiple_of(i * c, c)
        rs = pl.ds(r0, c)
        zc = lambda off, w: z_ref[rs, off:off + w]

        small = zc(Z_SMALL, 128)
        sb = small + sp[0:1, :]
        qkv = qkv_ref[rs, :]
        cq, ck, cv = qkv[:, 0:GROUP_W], qkv[:, GROUP_W:2 * GROUP_W], qkv[:, 2 * GROUP_W:3 * GROUP_W]
        dq = cq * lax.rsqrt(_dot(cq * cq, kc['ones256'][...]) + EPS) * (DN_DK ** -0.5)
        dk = ck * lax.rsqrt(_dot(ck * ck, kc['ones256'][...]) + EPS)
        beta_s = _sigmoid(small)
        gam_s = _scan0(neg_a * _softplus(sb), jnp.add, 0.0)
        o_dn, sdn_new = _deltanet_chunk(dq, dk, cv, beta_s, gam_s, sdn_ref[b], kc, c)
        sdn_ref[b] = sdn_new
        mix_ref[rs, 0:GROUP_W] = _head_norm_gate(o_dn, kc['ones256'][...], gn[0:1, :],
                                                 _silu(zc(Z_DNG, GROUP_W))).astype(BF16)
        zf = zc(Z_HGF, GROUP_W)
        lsz = _log_sigmoid(zf)
        t2 = log_1mlb + lsz
        mx = jnp.maximum(log_lb, t2)
        log_f = mx + jnp.log(jnp.exp(log_lb - mx) + jnp.exp(t2 - mx))
        key_hg = (1.0 - lb) * _sigmoid(-zf)
        o_hg, shg_new = _gla_chunk(_silu(zc(Z_HGQ, GROUP_W)), key_hg, zc(Z_HGI, GROUP_W), log_f,
                                   shg_ref[b], kc, c, True)
        shg_ref[b] = shg_new
        mix_ref[rs, GROUP_W:2 * GROUP_W] = _head_norm_gate(o_hg, kc['ones256'][...], gn[1:2, :],
                                                           _silu(zc(Z_HGG, GROUP_W))).astype(BF16)
        g_gla = _log_sigmoid(_dot(small, wup_ref[...]) + bup_ref[...]) * (1.0 / GLA_TAU)
        o_gla, sgla_new = _gla_chunk(zc(Z_GLAQ, GLA_W) * (GLA_DK ** -0.5), zc(Z_GLAK, GLA_W),
                                     zc(Z_GLAV, GROUP_W), g_gla, sgla_ref[b], kc, c, False)
        sgla_ref[b] = sgla_new
        mix_ref[rs, 2 * GROUP_W:3 * GROUP_W] = _head_norm_gate(o_gla, kc['ones256'][...], gn[2:3, :],
                                                               _silu(zc(Z_GLAG, GROUP_W))).astype(BF16)
        ig_s = pltpu.roll(sb, S_MLF - S_MLI, axis=1)
        lf_s = _log_sigmoid(sb)
        h_ml, c_new, n_new, m_new = _mlstm_chunk(zc(Z_MLQ, GROUP_W) * (ML_DK ** -0.5), zc(Z_MLK, GROUP_W),
                                                 zc(Z_MLV, GROUP_W), ig_s, lf_s,
                                                 cml_ref[b], nml_ref[b], mml_ref[b], kc, c)
        cml_ref[b] = c_new
        nml_ref[b] = n_new
        mml_ref[b] = m_new
        mix_ref[rs, 3 * GROUP_W:4 * GROUP_W] = _head_norm_gate(h_ml, kc['ones256'][...], gn[3:4, :],
                                                               _sigmoid(zc(Z_MLO, GROUP_W))).astype(BF16)
        return carry

    lax.fori_loop(0, nb * nchunk, chunk_body, 0)

    out = jnp.dot(mix_ref[...], wout_ref[...], preferred_element_type=F32).reshape(nb, tb, D_MODEL)
    y_ref[...] = x + mod[:, 2:3, :] * out


def _const_tables(c):
    pc = N_HEADS * c
    t = np.arange(c)[:, None]
    lane = np.arange(pc)[None, :]
    hs, s = lane // c, lane % c
    masks = [s <= t, s < t, s == t, (s // 16) == (t // 16)]
    sels = []
    sz = c // 2
    while sz >= SUBBLK:
        masks.append((s // (2 * sz)) == (t // (2 * sz)))
        r = np.arange(c)[None, :]
        sels.append(r == (t // (2 * sz)) * (2 * sz) + sz - 1)
        sz //= 2
    for j in range(SUBBLK):
        masks.append((s == t - j) & ((s // SUBBLK) == (t // SUBBLK)))
    masks = np.stack([np.broadcast_to(m, (c, pc)) for m in masks]).astype(np.float32)
    sel = np.stack(sels).astype(np.float32)

    def expand(col0, w):
        j = np.arange(128)[:, None]
        l = np.arange(N_HEADS * w)[None, :]
        return (j == col0 + l // w).astype(np.float32)

    e_c = np.stack([expand(S_DNB, c), expand(S_DNA, c), expand(S_MLF, c)])
    e_d = np.stack([expand(S_DNB, HEAD_V), expand(S_DNA, HEAD_V), expand(S_MLF, HEAD_V)])

    def blk(nr, rg, nl, lg):
        return ((np.arange(nr)[:, None] // rg) == (np.arange(nl)[None, :] // lg)).astype(np.float32)

    tabs = dict(
        masks=jnp.asarray(masks), e_c=jnp.asarray(e_c, BF16), e_d=jnp.asarray(e_d, BF16),
        bd256=jnp.asarray(blk(pc, c, GROUP_W, HEAD_V), BF16), bd128=jnp.asarray(blk(pc, c, GLA_W, GLA_DK), BF16),
        bdp=jnp.asarray(blk(pc, c, pc, c), BF16),
        st256=jnp.asarray(blk(GROUP_W, HEAD_V, GROUP_W, HEAD_V)), st128=jnp.asarray(blk(GROUP_W, HEAD_V, GLA_W, GLA_DK)),
        ie256=jnp.asarray(blk(GROUP_W, HEAD_V, pc, c), BF16), ie128=jnp.asarray(blk(GLA_W, GLA_DK, pc, c), BF16),
        iep=jnp.asarray(blk(pc, c, GROUP_W, HEAD_V), BF16),
        ones256=jnp.asarray(blk(GROUP_W, HEAD_V, GROUP_W, HEAD_V), BF16), sel=jnp.asarray(sel, BF16))
    order = ['masks', 'e_c', 'e_d', 'bd256', 'bd128', 'bdp', 'st256', 'st128', 'ie256', 'ie128', 'iep', 'ones256',
             'sel']
    return [tabs[k] for k in order]


def _full_spec(a):
    nd = a.ndim
    return pl.BlockSpec(a.shape, lambda bi, ti, _n=nd: (0,) * _n)


def _mixer_call(x, mod, lw, states, nb, tb):
    bsz, t, _ = x.shape
    c = min(CHUNK, t)
    assert t % tb == 0 and tb % c == 0 and bsz % nb == 0 and c % 16 == 0
    rows = nb * tb
    consts = _const_tables(c)
    params = [lw['g_mix'], lw['w_in'], lw['w_out'], lw['conv_w'], lw['sp'], lw['gn'], lw['lb_logits'], lw['lb_sel'],
              lw['wup'], lw['bup']]
    xspec = pl.BlockSpec((nb, tb, D_MODEL), lambda bi, ti: (bi, ti, 0))

    def bspec(a):
        nd = a.ndim
        return pl.BlockSpec((nb,) + a.shape[1:], lambda bi, ti, _n=nd: (bi,) + (0,) * (_n - 1))

    in_specs = ([xspec, bspec(mod)] + [_full_spec(a) for a in params] + [_full_spec(a) for a in consts]
                + [bspec(s) for s in states])
    out_shape = [jax.ShapeDtypeStruct(x.shape, F32)] + [jax.ShapeDtypeStruct(s.shape, F32) for s in states]
    out_specs = [xspec] + [bspec(s) for s in states]
    scratch = [pltpu.VMEM((rows, N_IN), F32),
               pltpu.VMEM((nb, SUB + tb, DN_CONV_CH), F32),
               pltpu.VMEM((rows, DN_CONV_CH), F32),
               pltpu.VMEM((rows, MIX_W), BF16)]
    kern = functools.partial(_mixer_kernel, nb=nb, tb=tb, c=c)
    return pl.pallas_call(
        kern, grid=(bsz // nb, t // tb), in_specs=in_specs, out_specs=out_specs, out_shape=out_shape,
        scratch_shapes=scratch, name='mixer',
        compiler_params=pltpu.CompilerParams(dimension_semantics=('arbitrary', 'arbitrary'),
                                             vmem_limit_bytes=VMEM_LIMIT),
    )(x, mod, *params, *consts, *states)


FF_TILE = 256


def _ffn_kernel(x_ref, mod_ref, gffn_ref, wup_ref, wdown_ref, gfin_ref, y_ref, *, nb, tb, final):
    rows = nb * tb
    x = x_ref[...]
    mod = mod_ref[...]
    h = _rmsnorm_rows(x, gffn_ref[...]) * (1.0 + mod[:, 4:5, :]) + mod[:, 3:4, :]
    hb = h.reshape(rows, D_MODEL).astype(BF16)
    acc = jnp.zeros((rows, D_MODEL), F32)
    for j in range(D_FF // FF_TILE):
        gate = jnp.dot(hb, wup_ref[:, j * FF_TILE:(j + 1) * FF_TILE], preferred_element_type=F32)
        up = jnp.dot(hb, wup_ref[:, D_FF + j * FF_TILE:D_FF + (j + 1) * FF_TILE], preferred_element_type=F32)
        act = (_silu(gate) * up).astype(BF16)
        acc = acc + jnp.dot(act, wdown_ref[j * FF_TILE:(j + 1) * FF_TILE, :], preferred_element_type=F32)
    y = x + mod[:, 5:6, :] * acc.reshape(nb, tb, D_MODEL)
    if final:
        y = _rmsnorm_rows(y, gfin_ref[...])
    y_ref[...] = y


def _ffn_call(x, mod, lw, g_final, nb, tb, final):
    bsz, t, _ = x.shape
    xspec = pl.BlockSpec((nb, tb, D_MODEL), lambda bi, ti: (bi, ti, 0))
    mspec = pl.BlockSpec((nb, 6, D_MODEL), lambda bi, ti: (bi, 0, 0))
    params = [lw['g_ffn'], lw['w_up'], lw['w_down'], g_final]
    kern = functools.partial(_ffn_kernel, nb=nb, tb=tb, final=final)
    return pl.pallas_call(
        kern, grid=(bsz // nb, t // tb), in_specs=[xspec, mspec] + [_full_spec(a) for a in params],
        out_specs=xspec, out_shape=jax.ShapeDtypeStruct(x.shape, F32), name='ffn',
        compiler_params=pltpu.CompilerParams(dimension_semantics=('arbitrary', 'arbitrary'),
                                             vmem_limit_bytes=VMEM_LIMIT),
    )(x, mod, *params)


ADA_TILE = 1536


def _ada_kernel(c_ref, w_ref, b_ref, o_ref):
    o_ref[0] = jnp.dot(_silu(c_ref[...]).astype(BF16), w_ref[0], preferred_element_type=F32) + b_ref[0]


def _ada_call(c_all, w_ada, b_ada):
    n = c_all.shape[0]
    nt = 6 * D_MODEL // ADA_TILE
    return pl.pallas_call(
        _ada_kernel, grid=(DEPTH, nt),
        in_specs=[pl.BlockSpec((n, D_MODEL), lambda l, j: (0, 0)),
                  pl.BlockSpec((1, D_MODEL, ADA_TILE), lambda l, j: (l, 0, j)),
                  pl.BlockSpec((1, 1, ADA_TILE), lambda l, j: (l, 0, j))],
        out_specs=pl.BlockSpec((1, n, ADA_TILE), lambda l, j: (l, 0, j)),
        out_shape=jax.ShapeDtypeStruct((DEPTH, n, 6 * D_MODEL), F32), name='ada',
        compiler_params=pltpu.CompilerParams(dimension_semantics=('arbitrary', 'arbitrary')),
    )(c_all, w_ada, b_ada.reshape(DEPTH, 1, 6 * D_MODEL))


def _permute_w_in(w):
    d = w.shape[0]
    zeros = lambda n: jnp.zeros((d, n), w.dtype)
    small = jnp.concatenate([w[:, 768:772], zeros(12), w[:, 2568:2584], w[:, 772:776], zeros(28),
                             w[:, 3608:3612], zeros(28), w[:, 3612:3616], zeros(28)], axis=1)
    return jnp.concatenate([w[:, 0:768], w[:, 776:2568], w[:, 2584:3608], w[:, 3616:3872], small], axis=1)


def _lane_row(pairs, width=128):
    row = jnp.zeros((width,), F32)
    for off, val in pairs:
        row = row.at[off:off + val.shape[0]].set(val.astype(F32))
    return row


def _layer_weights(p, l):
    sp = jnp.zeros((SUB, 128), F32)
    sp = sp.at[0].set(_lane_row([(S_DNA, p['dn_dt_bias'][l]), (S_MLI, p['ml_i_bias'][l]), (S_MLF, p['ml_f_bias'][l])]))
    sp = sp.at[1].set(_lane_row([(S_DNA, p['dn_a_log'][l])]))
    gn = jnp.stack([jnp.tile(p[k][l].astype(F32), N_HEADS) for k in ('dn_norm_g', 'hg_norm_g', 'gla_norm_g', 'ml_norm_g')])
    wup = jnp.zeros((128, GLA_W), F32).at[S_GLAR:S_GLAR + GLA_RANK].set(p['gla_w_up'][l]).astype(BF16)
    lb_sel = (jnp.arange(DEPTH) >= 1) & (jnp.arange(DEPTH) <= l)
    return dict(
        g_mix=p['g_mix'][l].reshape(1, D_MODEL), g_ffn=p['g_ffn'][l].reshape(1, D_MODEL),
        w_in=_permute_w_in(p['w_in'][l]).astype(BF16), w_out=p['w_out'][l].astype(BF16),
        conv_w=p['dn_conv_w'][l], sp=sp, gn=gn, lb_logits=p['hg_lb_logits'].astype(F32),
        lb_sel=lb_sel.astype(F32).reshape(DEPTH, 1), wup=wup, bup=p['gla_b_up'][l].reshape(1, GLA_W).astype(F32),
        w_up=p['w_up'][l].astype(BF16), w_down=p['w_down'][l].astype(BF16))


def _state_to_bd(s):
    b, h, dk, dv = s.shape
    eye = jnp.eye(h, dtype=s.dtype)
    return jnp.einsum('bhkv,hg->bhvgk', s, eye).reshape(b, h * dv, h * dk)


def _state_from_bd(s, dk):
    b = s.shape[0]
    s5 = s.reshape(b, N_HEADS, HEAD_V, N_HEADS, dk)
    return jnp.stack([s5[:, h, :, h, :] for h in range(N_HEADS)], axis=1).transpose(0, 1, 3, 2)


def _trunk(x, mods, states, lws, g_final, nb, tb):
    new_states = []
    for l in range(DEPTH):
        outs = _mixer_call(x, mods[l], lws[l], states[l], nb, tb)
        x = outs[0]
        new_states.append(outs[1:])
        x = _ffn_call(x, mods[l], lws[l], g_final, nb, tb, l == DEPTH - 1)
    return x, new_states


def _pack_states(conv, s_dn, s_hg, s_gla, c_ml, n_ml, m_ml, l):
    b = conv.shape[1]
    m_row = jnp.zeros((b, 1, 128), F32).at[:, 0, S_MLF:S_MLF + N_HEADS].set(m_ml[l].astype(F32))
    return (conv[l].astype(F32), _state_to_bd(s_dn[l].astype(F32)), _state_to_bd(s_hg[l].astype(F32)),
            _state_to_bd(s_gla[l].astype(F32)), _state_to_bd(c_ml[l].astype(F32)),
            n_ml[l].astype(F32).reshape(b, 1, N_HEADS * ML_DK), m_row)


def _unpack_states(sts):
    conv = jnp.stack([s[0] for s in sts])
    s_dn = jnp.stack([_state_from_bd(s[1], DN_DK) for s in sts])
    s_hg = jnp.stack([_state_from_bd(s[2], HG_DK) for s in sts])
    s_gla = jnp.stack([_state_from_bd(s[3], GLA_DK) for s in sts])
    c_ml = jnp.stack([_state_from_bd(s[4], ML_DK) for s in sts])
    n_ml = jnp.stack([s[5].reshape(s[5].shape[0], N_HEADS, ML_DK) for s in sts])
    m_ml = jnp.stack([s[6][:, 0, S_MLF:S_MLF + N_HEADS] for s in sts])
    return conv, s_dn, s_hg, s_gla, c_ml, n_ml, m_ml


def _zero_states(b):
    z = lambda *s: jnp.zeros(s, F32)
    return (z(DEPTH, b, DN_CONV - 1, DN_CONV_CH), z(DEPTH, b, N_HEADS, DN_DK, HEAD_V),
            z(DEPTH, b, N_HEADS, HG_DK, HEAD_V), z(DEPTH, b, N_HEADS, GLA_DK, HEAD_V),
            z(DEPTH, b, N_HEADS, ML_DK, HEAD_V), z(DEPTH, b, N_HEADS, ML_DK), z(DEPTH, b, N_HEADS))


def _tiling(bsz, t):
    tb = min(t, 256)
    nb = max(1, min(bsz, 256 // tb))
    while bsz % nb:
        nb -= 1
    return nb, tb


def kernel(x_prompt, x_sample, c_prompt, c_sample, cache_dn_conv, state_dn, state_hgrn, state_gla, state_mlstm_c, state_mlstm_n, state_mlstm_m, w_ada, b_ada, g_mix, g_ffn, w_in, dn_conv_w, dn_a_log, dn_dt_bias, dn_norm_g, hg_lb_logits, hg_norm_g, gla_w_up, gla_b_up, gla_norm_g, ml_i_bias, ml_f_bias, ml_norm_g, w_out, w_up, w_down, g_final):
    p = dict(g_mix=g_mix, g_ffn=g_ffn, w_in=w_in, dn_conv_w=dn_conv_w, dn_a_log=dn_a_log, dn_dt_bias=dn_dt_bias,
             dn_norm_g=dn_norm_g, hg_lb_logits=hg_lb_logits, hg_norm_g=hg_norm_g, gla_w_up=gla_w_up,
             gla_b_up=gla_b_up, gla_norm_g=gla_norm_g, ml_i_bias=ml_i_bias, ml_f_bias=ml_f_bias,
             ml_norm_g=ml_norm_g, w_out=w_out, w_up=w_up, w_down=w_down)
    lws = [_layer_weights(p, l) for l in range(DEPTH)]
    gfin = g_final.reshape(1, D_MODEL).astype(F32)
    bp, bs = x_prompt.shape[0], x_sample.shape[0]
    mod = _ada_call(jnp.concatenate([c_prompt, c_sample], axis=0).astype(F32), w_ada.astype(BF16),
                    b_ada.astype(F32)).reshape(DEPTH, bp + bs, 6, D_MODEL)

    outs = []
    for x, lo, hi, raw in ((x_prompt, 0, bp, _zero_states(bp)),
                           (x_sample, bp, bp + bs, (cache_dn_conv, state_dn, state_hgrn, state_gla,
                                                    state_mlstm_c, state_mlstm_n, state_mlstm_m))):
        nb, tb = _tiling(x.shape[0], x.shape[1])
        states = [_pack_states(*raw, l) for l in range(DEPTH)]
        y, new = _trunk(x.astype(F32), [mod[l, lo:hi] for l in range(DEPTH)], states, lws, gfin, nb, tb)
        outs.append((y, _unpack_states(new)))
    (y_p, st_p), (y_s, st_s) = outs
    return (y_p, y_s) + tuple(st_p) + tuple(st_s)
```

```python
import functools

import numpy as np
import jax
import jax.numpy as jnp
from jax import lax
from jax.experimental import pallas as pl
from jax.experimental.pallas import tpu as pltpu

F32 = jnp.float32
BF16 = jnp.bfloat16

D_MODEL = 1024
DEPTH = 2
CHUNK = 64
N_HEADS = 4
HEAD_V = 64
GROUP_W = N_HEADS * HEAD_V
MIX_W = 4 * GROUP_W
DN_DK = 64
DN_CONV = 4
DN_CONV_CH = 3 * GROUP_W
HG_DK = 64
GLA_DK = 32
GLA_W = N_HEADS * GLA_DK
GLA_RANK = 16
GLA_TAU = 16.0
ML_DK = 64
D_FF = 2816
EPS = 1e-6

Z_DNQKV = 0
Z_DNG = 768
Z_HGQ = 1024
Z_HGF = 1280
Z_HGI = 1536
Z_HGG = 1792
Z_GLAQ = 2048
Z_GLAK = 2176
Z_GLAV = 2304
Z_GLAG = 2560
Z_MLQ = 2816
Z_MLK = 3072
Z_MLV = 3328
Z_MLO = 3584
Z_SMALL = 3840
N_IN = 3968
S_DNB = 0
S_GLAR = 16
S_DNA = 32
S_MLI = 64
S_MLF = 96

SUB = 8
VMEM_LIMIT = 56 * 1024 * 1024

M_INCL, M_STRICT, M_DIAG, M_SAME16 = 0, 1, 2, 3
M_LEVEL0 = 4
SUBBLK = 8
MAX_INTERLEAVE = 2


def _sigmoid(x):
    return 1.0 / (1.0 + jnp.exp(-x))


def _silu(x):
    return x * _sigmoid(x)


def _log_sigmoid(x):
    return jnp.minimum(x, 0.0) - jnp.log1p(jnp.exp(-jnp.abs(x)))


def _softplus(x):
    return jnp.maximum(x, 0.0) + jnp.log1p(jnp.exp(-jnp.abs(x)))


def _dot(a, b):
    return jnp.dot(a.astype(BF16), b.astype(BF16), preferred_element_type=F32)


def _dot_nt(a, b):
    return lax.dot_general(a.astype(BF16), b.astype(BF16), (((1,), (1,)), ((), ())),
                           preferred_element_type=F32)


def _dot_tn(a, b):
    return lax.dot_general(a.astype(BF16), b.astype(BF16), (((0,), (0,)), ((), ())),
                           preferred_element_type=F32)


def _split3(x):
    x1 = x.astype(BF16)
    r = x - x1.astype(F32)
    x2 = r.astype(BF16)
    r = r - x2.astype(F32)
    return x1, x2, r.astype(BF16)


def _move_r(x, sel):
    x1, x2, x3 = _split3(x)
    d = lambda a: jnp.dot(a, sel, preferred_element_type=F32)
    return d(x1) + (d(x2) + d(x3))


def _move_l(sel, x):
    x1, x2, x3 = _split3(x)
    d = lambda a: jnp.dot(sel, a, preferred_element_type=F32)
    return d(x1) + (d(x2) + d(x3))


def _bd(x, mask):
    xb = x.astype(BF16)
    return jnp.concatenate([xb] * N_HEADS, axis=0) * mask


def _scan0(x, op, fill):
    n = x.shape[0]
    row = lax.broadcasted_iota(jnp.int32, x.shape, 0)
    sh = 1
    while sh < n:
        r = pltpu.roll(x, sh, axis=0)
        x = op(x, jnp.where(row >= sh, r, fill))
        sh *= 2
    return x


def _rowform(xe, diag):
    return jnp.sum(xe * diag, axis=0, keepdims=True)


def _rmsnorm_rows(x, g):
    return x * lax.rsqrt(jnp.mean(x * x, axis=-1, keepdims=True) + EPS) * g


def _head_norm_gate(o, ones_bd, g_row, gate):
    ms = _dot(o * o, ones_bd) * (1.0 / HEAD_V)
    return o * lax.rsqrt(ms + EPS) * g_row * gate


def _run_interleaved(gens):
    live = list(gens)
    while live:
        alive = []
        for g in live:
            try:
                next(g)
                alive.append(g)
            except StopIteration:
                pass
        live = alive


def _await(boxes, key):
    while key not in boxes:
        yield
    return boxes[key]


def _tri_inverse(mm, masks, bdp, c):
    eye = masks[M_DIAG]
    mul = lambda a, b: _dot(a, _bd(b, bdp))
    md = mm * masks[M_SAME16]
    mo = mm - md
    p2 = mul(md, md)
    yield
    d = eye - md
    d = d + mul(d, p2)
    p4 = mul(p2, p2)
    yield
    d = d + mul(d, p4)
    p8 = mul(p4, p4)
    yield
    d = d + mul(d, p8)
    yield
    n = mul(d, mo)
    yield
    r = eye - n
    nlev = c // 16
    if nlev > 2:
        n2 = mul(n, n)
        yield
        r = r + mul(r, n2)
        yield
        if nlev > 4:
            n4 = mul(n2, n2)
            yield
            r = r + mul(r, n4)
            yield
    t_inv = mul(r, d)
    yield
    return t_inv


def _deltanet_chunk(q, k, v, beta_s, gam_s, get_state, kc, c):
    masks = kc['masks']
    gam_e = _move_r(gam_s, kc['e_c'][1])
    beta_e = _move_r(beta_s, kc['e_c'][0])
    if c == HEAD_V:
        gam_d, beta_d = gam_e, beta_e
    else:
        gam_d = _move_r(gam_s, kc['e_d'][1])
        beta_d = _move_r(beta_s, kc['e_d'][0])
    kb = _bd(k, kc['bd256'][...])
    kk = _dot_nt(k, kb)
    qk = _dot_nt(q, kb)
    yield
    gam_r = _rowform(gam_e, masks[M_DIAG])
    dec = jnp.exp(jnp.minimum(gam_e - gam_r, 0.0))
    mm = beta_e * kk * dec * masks[M_STRICT]
    t_inv = yield from _tri_inverse(mm, masks, kc['bdp'][...], c)
    eg = jnp.exp(gam_d)
    w = _dot(t_inv, _bd(beta_d * eg * k, kc['bd256'][...]))
    u0 = _dot(t_inv, _bd(beta_d * v, kc['bd256'][...]))
    gl = gam_d[c - 1:c, :]
    kdec = k * jnp.exp(gl - gam_d)
    yield
    st = yield from get_state()
    u = u0 - _dot_nt(w, st)
    qs = _dot_nt(q * eg, st)
    yield
    st_new = jnp.exp(gl) * st + kc['st256'][...] * _dot_tn(u, kdec)
    o = qs + _dot(qk * dec * masks[M_INCL], _bd(u, kc['bd256'][...]))
    return o, st_new


def _gla_chunk(q, k, v, g, get_state, kc, c, wide):
    masks = kc['masks']
    bdk = kc['bd256'][...] if wide else kc['bd128'][...]
    ie = kc['ie256'][...] if wide else kc['ie128'][...]
    stm = kc['st256'][...] if wide else kc['st128'][...]
    b = _scan0(g, jnp.add, 0.0)
    sizes = []
    sz = c // 2
    while sz >= SUBBLK:
        sizes.append(sz)
        sz //= 2
    refs = [_move_l(kc['sel'][lvl], b) for lvl in range(len(sizes))]
    yield
    row = lax.broadcasted_iota(jnp.int32, b.shape, 0)
    attn = jnp.zeros((c, N_HEADS * c), F32)
    for lvl, sz in enumerate(sizes):
        ref = refs[lvl]
        right = jnp.bitwise_and(jnp.right_shift(row, sz.bit_length() - 1), 1) == 1
        qd = jnp.where(right, q * jnp.exp(jnp.minimum(b - ref, 0.0)), 0.0)
        kd = jnp.where(right, 0.0, k * jnp.exp(jnp.minimum(ref - b, 0.0)))
        attn = attn + _dot_nt(qd, _bd(kd, bdk)) * masks[M_LEVEL0 + lvl]
        yield
    nl = len(sizes)
    for j in range(SUBBLK):
        if j == 0:
            x = q * k
        else:
            ks = pltpu.roll(k, j, axis=0)
            bs = pltpu.roll(b, j, axis=0)
            x = q * ks * jnp.exp(jnp.minimum(b - bs, 0.0))
        attn = attn + _dot(x, ie) * masks[M_LEVEL0 + nl + j]
        if j % 2 == 1:
            yield
    last = b[c - 1:c, :]
    upd = stm * _dot_tn(v, k * jnp.exp(last - b))
    o = _dot(attn, _bd(v, kc['bd256'][...]))
    yield
    st = yield from get_state()
    st_new = jnp.exp(last) * st + upd
    o = o + _dot_nt(q * jnp.exp(b), st)
    return o, st_new


def _mlstm_chunk(q, k, v, ig_s, lf_s, get_m, put_m, get_cn, kc, c):
    masks = kc['masks']
    fcum = _scan0(lf_s, jnp.add, 0.0)
    a = ig_s - fcum
    imax = fcum + _scan0(a, jnp.maximum, -jnp.inf)
    fl = fcum[c - 1:c, :]
    lw = fl - fcum + ig_s
    lw_max = jnp.max(lw, axis=0, keepdims=True)
    kb = _bd(k, kc['bd256'][...])
    qk = _dot_nt(q, kb)
    a_r = _rowform(_move_r(a, kc['e_c'][2]), masks[M_DIAG])
    yield
    m_row = yield from get_m()
    m_new = jnp.maximum(fl + m_row, lw_max)
    put_m(m_new)
    inter = fcum + m_row
    mt = jnp.maximum(inter, imax)
    x1_e = _move_r(fcum - mt, kc['e_c'][2])
    w_inter = jnp.exp(_move_r(inter - mt, kc['e_d'][2]))
    floor = jnp.exp(_move_r(-mt, kc['e_d'][2]))
    ws = jnp.exp(_move_r(lw - m_new, kc['e_d'][2]))
    wc8 = jnp.broadcast_to(fl + m_row - m_new, (SUB, 128))
    wc = jnp.exp(_move_r(wc8, kc['e_d'][2]))[0:1, :]
    yield
    w_intra = jnp.exp(jnp.minimum(x1_e + a_r, 0.0)) * masks[M_INCL] * qk
    num = _dot(w_intra, _bd(v, kc['bd256'][...]))
    den = _dot(w_intra, kc['iep'][...])
    upd = kc['st256'][...] * _dot_tn(ws * v, k)
    n_upd = jnp.sum(ws * k, axis=0, keepdims=True)
    yield
    ct, n_row = yield from get_cn()
    ct_new = wc * ct + upd
    n_new = wc * n_row + n_upd
    num = num + w_inter * _dot_nt(q, ct)
    den = den + w_inter * _dot(q * n_row, kc['ones256'][...])
    hh = num / jnp.maximum(jnp.abs(den), floor)
    return hh, ct_new, n_new


def _mixer_kernel(x_ref, mod_ref, gmix_ref, win_ref, wout_ref, convw_ref, sp_ref, gn_ref, lbl_ref, lbs_ref,
                  wup_ref, bup_ref,
                  masks_ref, ec_ref, ed_ref, bd256_ref, bd128_ref, bdp_ref, st256_ref, st128_ref,
                  ie256_ref, ie128_ref, iep_ref, ones256_ref, sel_ref,
                  conv0_ref, sdn0_ref, shg0_ref, sgla0_ref, c0_ref, n0_ref, m0_ref,
                  y_ref, convo_ref, sdn_ref, shg_ref, sgla_ref, cml_ref, nml_ref, mml_ref,
                  z_ref, xp_ref, qkv_ref, mix_ref, *, nb, tb, c, ilv):
    ti = pl.program_id(1)
    rows = nb * tb
    nchunk = tb // c

    @pl.when(ti == 0)
    def _():
        convo_ref[...] = conv0_ref[...]
        sdn_ref[...] = sdn0_ref[...]
        shg_ref[...] = shg0_ref[...]
        sgla_ref[...] = sgla0_ref[...]
        cml_ref[...] = c0_ref[...]
        nml_ref[...] = n0_ref[...]
        mml_ref[...] = m0_ref[...]

    x = x_ref[...]
    mod = mod_ref[...]
    h = _rmsnorm_rows(x, gmix_ref[...]) * (1.0 + mod[:, 1:2, :]) + mod[:, 0:1, :]
    z_ref[...] = jnp.dot(h.reshape(rows, D_MODEL).astype(BF16), win_ref[...], preferred_element_type=F32)

    convw = convw_ref[...]
    for b in range(nb):
        xp_ref[b, SUB - (DN_CONV - 1):SUB, :] = convo_ref[b]
        xp_ref[b, SUB:SUB + tb, :] = z_ref[b * tb:(b + 1) * tb, Z_DNQKV:Z_DNQKV + DN_CONV_CH]
        acc = xp_ref[b, SUB - 3:SUB - 3 + tb, :] * convw[0:1, :]
        for j in range(1, DN_CONV):
            acc = acc + xp_ref[b, SUB - 3 + j:SUB - 3 + j + tb, :] * convw[j:j + 1, :]
        qkv_ref[b * tb:(b + 1) * tb, :] = _silu(acc)
        convo_ref[b] = xp_ref[b, SUB + tb - (DN_CONV - 1):SUB + tb, :]

    kc = dict(masks=masks_ref, e_c=ec_ref, e_d=ed_ref, bd256=bd256_ref, bd128=bd128_ref, bdp=bdp_ref,
              st256=st256_ref, st128=st128_ref, ie256=ie256_ref, ie128=ie128_ref, iep=iep_ref,
              ones256=ones256_ref, sel=sel_ref)
    sp = sp_ref[...]
    gn = gn_ref[...]
    lbl = lbl_ref[...]
    lbs = lbs_ref[...]
    lbe = jnp.exp(lbl - jnp.max(lbl, axis=0, keepdims=True))
    lb = jnp.sum(lbs * (lbe / jnp.sum(lbe, axis=0, keepdims=True)), axis=0, keepdims=True)
    log_lb = jnp.log(lb)
    log_1mlb = jnp.log1p(-lb)
    neg_a = -jnp.exp(sp[1:2, :])
    chain = nchunk > 1

    def chunk_gens(i, u, boxes):
        if nchunk == 1:
            b = i
        elif nb == 1:
            b = 0
        else:
            b = i // nchunk
        rs = pl.ds(pl.multiple_of(i * c, c), c)
        zc = lambda off, w: z_ref[rs, off:off + w]

        def getter(key, read):
            def get():
                if chain and u > 0:
                    return (yield from _await(boxes, (key, u - 1)))
                return read()
                yield
            return get

        def norm_gate(o, row, gate):
            return _head_norm_gate(o, kc['ones256'][...], gn[row:row + 1, :], gate).astype(BF16)

        def dn():
            small = zc(Z_SMALL, 128)
            sb = small + sp[0:1, :]
            qkv = qkv_ref[rs, :]
            cq, ck, cv = qkv[:, 0:GROUP_W], qkv[:, GROUP_W:2 * GROUP_W], qkv[:, 2 * GROUP_W:3 * GROUP_W]
            ssq = _dot(cq * cq, kc['ones256'][...])
            ssk = _dot(ck * ck, kc['ones256'][...])
            beta_s = _sigmoid(small)
            gam_s = _scan0(neg_a * _softplus(sb), jnp.add, 0.0)
            yield
            dq = cq * lax.rsqrt(ssq + EPS) * (DN_DK ** -0.5)
            dk = ck * lax.rsqrt(ssk + EPS)
            o, st_new = yield from _deltanet_chunk(dq, dk, cv, beta_s, gam_s,
                                                   getter('dn', lambda: sdn_ref[b]), kc, c)
            boxes[('dn', u)] = st_new
            sdn_ref[b] = st_new
            yield
            mix_ref[rs, 0:GROUP_W] = norm_gate(o, 0, _silu(zc(Z_DNG, GROUP_W)))

        def hg():
            zf = zc(Z_HGF, GROUP_W)
            lsz = _log_sigmoid(zf)
            t2 = log_1mlb + lsz
            mx = jnp.maximum(log_lb, t2)
            log_f = mx + jnp.log(jnp.exp(log_lb - mx) + jnp.exp(t2 - mx))
            key_hg = (1.0 - lb) * _sigmoid(-zf)
            o, st_new = yield from _gla_chunk(_silu(zc(Z_HGQ, GROUP_W)), key_hg, zc(Z_HGI, GROUP_W), log_f,
                                              getter('hg', lambda: shg_ref[b]), kc, c, True)
            boxes[('hg', u)] = st_new
            shg_ref[b] = st_new
            yield
            mix_ref[rs, GROUP_W:2 * GROUP_W] = norm_gate(o, 1, _silu(zc(Z_HGG, GROUP_W)))

        def gla():
            small = zc(Z_SMALL, 128)
            g_gla = _log_sigmoid(_dot(small, wup_ref[...]) + bup_ref[...]) * (1.0 / GLA_TAU)
            yield
            o, st_new = yield from _gla_chunk(zc(Z_GLAQ, GLA_W) * (GLA_DK ** -0.5), zc(Z_GLAK, GLA_W),
                                              zc(Z_GLAV, GROUP_W), g_gla,
                                              getter('gla', lambda: sgla_ref[b]), kc, c, False)
            boxes[('gla', u)] = st_new
            sgla_ref[b] = st_new
            yield
            mix_ref[rs, 2 * GROUP_W:3 * GROUP_W] = norm_gate(o, 2, _silu(zc(Z_GLAG, GROUP_W)))

        def ml():
            sb = zc(Z_SMALL, 128) + sp[0:1, :]
            ig_s = pltpu.roll(sb, S_MLF - S_MLI, axis=1)
            lf_s = _log_sigmoid(sb)

            def put_m(m_new):
                boxes[('ml_m', u)] = m_new
                mml_ref[b] = m_new

            hh, c_new, n_new = yield from _mlstm_chunk(
                zc(Z_MLQ, GROUP_W) * (ML_DK ** -0.5), zc(Z_MLK, GROUP_W), zc(Z_MLV, GROUP_W), ig_s, lf_s,
                getter('ml_m', lambda: mml_ref[b]), put_m,
                getter('ml_cn', lambda: (cml_ref[b], nml_ref[b])), kc, c)
            boxes[('ml_cn', u)] = (c_new, n_new)
            cml_ref[b] = c_new
            nml_ref[b] = n_new
            yield
            mix_ref[rs, 3 * GROUP_W:4 * GROUP_W] = norm_gate(hh, 3, _sigmoid(zc(Z_MLO, GROUP_W)))

        return [dn(), hg(), gla(), ml()]

    def chunk_body(it, carry):
        boxes = {}
        gens = []
        for u in range(ilv):
            gens += chunk_gens(it * ilv + u, u, boxes)
        _run_interleaved(gens)
        return carry

    lax.fori_loop(0, nb * nchunk // ilv, chunk_body, 0)

    out = jnp.dot(mix_ref[...], wout_ref[...], preferred_element_type=F32).reshape(nb, tb, D_MODEL)
    y_ref[...] = x + mod[:, 2:3, :] * out


def _const_tables(c):
    pc = N_HEADS * c
    t = np.arange(c)[:, None]
    lane = np.arange(pc)[None, :]
    hs, s = lane // c, lane % c
    masks = [s <= t, s < t, s == t, (s // 16) == (t // 16)]
    sels = []
    sz = c // 2
    while sz >= SUBBLK:
        masks.append((s // (2 * sz)) == (t // (2 * sz)))
        r = np.arange(c)[None, :]
        sels.append(r == (t // (2 * sz)) * (2 * sz) + sz - 1)
        sz //= 2
    for j in range(SUBBLK):
        masks.append((s == t - j) & ((s // SUBBLK) == (t // SUBBLK)))
    masks = np.stack([np.broadcast_to(m, (c, pc)) for m in masks]).astype(np.float32)
    sel = np.stack(sels).astype(np.float32)

    def expand(col0, w):
        j = np.arange(128)[:, None]
        l = np.arange(N_HEADS * w)[None, :]
        return (j == col0 + l // w).astype(np.float32)

    e_c = np.stack([expand(S_DNB, c), expand(S_DNA, c), expand(S_MLF, c)])
    e_d = np.stack([expand(S_DNB, HEAD_V), expand(S_DNA, HEAD_V), expand(S_MLF, HEAD_V)])

    def blk(nr, rg, nl, lg):
        return ((np.arange(nr)[:, None] // rg) == (np.arange(nl)[None, :] // lg)).astype(np.float32)

    tabs = dict(
        masks=jnp.asarray(masks), e_c=jnp.asarray(e_c, BF16), e_d=jnp.asarray(e_d, BF16),
        bd256=jnp.asarray(blk(pc, c, GROUP_W, HEAD_V), BF16), bd128=jnp.asarray(blk(pc, c, GLA_W, GLA_DK), BF16),
        bdp=jnp.asarray(blk(pc, c, pc, c), BF16),
        st256=jnp.asarray(blk(GROUP_W, HEAD_V, GROUP_W, HEAD_V)), st128=jnp.asarray(blk(GROUP_W, HEAD_V, GLA_W, GLA_DK)),
        ie256=jnp.asarray(blk(GROUP_W, HEAD_V, pc, c), BF16), ie128=jnp.asarray(blk(GLA_W, GLA_DK, pc, c), BF16),
        iep=jnp.asarray(blk(pc, c, GROUP_W, HEAD_V), BF16),
        ones256=jnp.asarray(blk(GROUP_W, HEAD_V, GROUP_W, HEAD_V), BF16), sel=jnp.asarray(sel, BF16))
    order = ['masks', 'e_c', 'e_d', 'bd256', 'bd128', 'bdp', 'st256', 'st128', 'ie256', 'ie128', 'iep', 'ones256',
             'sel']
    return [tabs[k] for k in order]


def _full_spec(a):
    nd = a.ndim
    return pl.BlockSpec(a.shape, lambda bi, ti, _n=nd: (0,) * _n)


def _mixer_call(x, mod, lw, states, nb, tb):
    bsz, t, _ = x.shape
    c = min(CHUNK, t)
    assert t % tb == 0 and tb % c == 0 and bsz % nb == 0 and c % 16 == 0
    rows = nb * tb
    consts = _const_tables(c)
    params = [lw['g_mix'], lw['w_in'], lw['w_out'], lw['conv_w'], lw['sp'], lw['gn'], lw['lb_logits'], lw['lb_sel'],
              lw['wup'], lw['bup']]
    xspec = pl.BlockSpec((nb, tb, D_MODEL), lambda bi, ti: (bi, ti, 0))

    def bspec(a):
        nd = a.ndim
        return pl.BlockSpec((nb,) + a.shape[1:], lambda bi, ti, _n=nd: (bi,) + (0,) * (_n - 1))

    in_specs = ([xspec, bspec(mod)] + [_full_spec(a) for a in params] + [_full_spec(a) for a in consts]
                + [bspec(s) for s in states])
    out_shape = [jax.ShapeDtypeStruct(x.shape, F32)] + [jax.ShapeDtypeStruct(s.shape, F32) for s in states]
    out_specs = [xspec] + [bspec(s) for s in states]
    scratch = [pltpu.VMEM((rows, N_IN), F32),
               pltpu.VMEM((nb, SUB + tb, DN_CONV_CH), F32),
               pltpu.VMEM((rows, DN_CONV_CH), F32),
               pltpu.VMEM((rows, MIX_W), BF16)]
    ilv = MAX_INTERLEAVE
    while (nb * (tb // c)) % ilv or (tb // c > 1 and (tb // c) % ilv):
        ilv //= 2
    kern = functools.partial(_mixer_kernel, nb=nb, tb=tb, c=c, ilv=ilv)
    return pl.pallas_call(
        kern, grid=(bsz // nb, t // tb), in_specs=in_specs, out_specs=out_specs, out_shape=out_shape,
        scratch_shapes=scratch, name='mixer',
        compiler_params=pltpu.CompilerParams(dimension_semantics=('arbitrary', 'arbitrary'),
                                             vmem_limit_bytes=VMEM_LIMIT),
    )(x, mod, *params, *consts, *states)


FF_TILE = 256


def _ffn_kernel(x_ref, mod_ref, gffn_ref, wup_ref, wdown_ref, gfin_ref, y_ref, *, nb, tb, final):
    rows = nb * tb
    x = x_ref[...]
    mod = mod_ref[...]
    h = _rmsnorm_rows(x, gffn_ref[...]) * (1.0 + mod[:, 4:5, :]) + mod[:, 3:4, :]
    hb = h.reshape(rows, D_MODEL).astype(BF16)
    acc = jnp.zeros((rows, D_MODEL), F32)
    for j in range(D_FF // FF_TILE):
        gate = jnp.dot(hb, wup_ref[:, j * FF_TILE:(j + 1) * FF_TILE], preferred_element_type=F32)
        up = jnp.dot(hb, wup_ref[:, D_FF + j * FF_TILE:D_FF + (j + 1) * FF_TILE], preferred_element_type=F32)
        act = (_silu(gate) * up).astype(BF16)
        acc = acc + jnp.dot(act, wdown_ref[j * FF_TILE:(j + 1) * FF_TILE, :], preferred_element_type=F32)
    y = x + mod[:, 5:6, :] * acc.reshape(nb, tb, D_MODEL)
    if final:
        y = _rmsnorm_rows(y, gfin_ref[...])
    y_ref[...] = y


def _ffn_call(x, mod, lw, g_final, nb, tb, final):
    bsz, t, _ = x.shape
    xspec = pl.BlockSpec((nb, tb, D_MODEL), lambda bi, ti: (bi, ti, 0))
    mspec = pl.BlockSpec((nb, 6, D_MODEL), lambda bi, ti: (bi, 0, 0))
    params = [lw['g_ffn'], lw['w_up'], lw['w_down'], g_final]
    kern = functools.partial(_ffn_kernel, nb=nb, tb=tb, final=final)
    return pl.pallas_call(
        kern, grid=(bsz // nb, t // tb), in_specs=[xspec, mspec] + [_full_spec(a) for a in params],
        out_specs=xspec, out_shape=jax.ShapeDtypeStruct(x.shape, F32), name='ffn',
        compiler_params=pltpu.CompilerParams(dimension_semantics=('arbitrary', 'arbitrary'),
                                             vmem_limit_bytes=VMEM_LIMIT),
    )(x, mod, *params)


ADA_TILE = 1536


def _ada_kernel(c_ref, w_ref, b_ref, o_ref):
    o_ref[0] = jnp.dot(_silu(c_ref[...]).astype(BF16), w_ref[0], preferred_element_type=F32) + b_ref[0]


def _ada_call(c_all, w_ada, b_ada):
    n = c_all.shape[0]
    nt = 6 * D_MODEL // ADA_TILE
    return pl.pallas_call(
        _ada_kernel, grid=(DEPTH, nt),
        in_specs=[pl.BlockSpec((n, D_MODEL), lambda l, j: (0, 0)),
                  pl.BlockSpec((1, D_MODEL, ADA_TILE), lambda l, j: (l, 0, j)),
                  pl.BlockSpec((1, 1, ADA_TILE), lambda l, j: (l, 0, j))],
        out_specs=pl.BlockSpec((1, n, ADA_TILE), lambda l, j: (l, 0, j)),
        out_shape=jax.ShapeDtypeStruct((DEPTH, n, 6 * D_MODEL), F32), name='ada',
        compiler_params=pltpu.CompilerParams(dimension_semantics=('arbitrary', 'arbitrary')),
    )(c_all, w_ada, b_ada.reshape(DEPTH, 1, 6 * D_MODEL))


def _permute_w_in(w):
    d = w.shape[0]
    zeros = lambda n: jnp.zeros((d, n), w.dtype)
    small = jnp.concatenate([w[:, 768:772], zeros(12), w[:, 2568:2584], w[:, 772:776], zeros(28),
                             w[:, 3608:3612], zeros(28), w[:, 3612:3616], zeros(28)], axis=1)
    return jnp.concatenate([w[:, 0:768], w[:, 776:2568], w[:, 2584:3608], w[:, 3616:3872], small], axis=1)


def _lane_row(pairs, width=128):
    row = jnp.zeros((width,), F32)
    for off, val in pairs:
        row = row.at[off:off + val.shape[0]].set(val.astype(F32))
    return row


def _layer_weights(p, l):
    sp = jnp.zeros((SUB, 128), F32)
    sp = sp.at[0].set(_lane_row([(S_DNA, p['dn_dt_bias'][l]), (S_MLI, p['ml_i_bias'][l]), (S_MLF, p['ml_f_bias'][l])]))
    sp = sp.at[1].set(_lane_row([(S_DNA, p['dn_a_log'][l])]))
    gn = jnp.stack([jnp.tile(p[k][l].astype(F32), N_HEADS) for k in ('dn_norm_g', 'hg_norm_g', 'gla_norm_g', 'ml_norm_g')])
    wup = jnp.zeros((128, GLA_W), F32).at[S_GLAR:S_GLAR + GLA_RANK].set(p['gla_w_up'][l]).astype(BF16)
    lb_sel = (jnp.arange(DEPTH) >= 1) & (jnp.arange(DEPTH) <= l)
    return dict(
        g_mix=p['g_mix'][l].reshape(1, D_MODEL), g_ffn=p['g_ffn'][l].reshape(1, D_MODEL),
        w_in=_permute_w_in(p['w_in'][l]).astype(BF16), w_out=p['w_out'][l].astype(BF16),
        conv_w=p['dn_conv_w'][l], sp=sp, gn=gn, lb_logits=p['hg_lb_logits'].astype(F32),
        lb_sel=lb_sel.astype(F32).reshape(DEPTH, 1), wup=wup, bup=p['gla_b_up'][l].reshape(1, GLA_W).astype(F32),
        w_up=p['w_up'][l].astype(BF16), w_down=p['w_down'][l].astype(BF16))


def _state_to_bd(s):
    b, h, dk, dv = s.shape
    eye = jnp.eye(h, dtype=s.dtype)
    return jnp.einsum('bhkv,hg->bhvgk', s, eye).reshape(b, h * dv, h * dk)


def _state_from_bd(s, dk):
    b = s.shape[0]
    s5 = s.reshape(b, N_HEADS, HEAD_V, N_HEADS, dk)
    return jnp.einsum('bhvgk,hg->bhkv', s5, jnp.eye(N_HEADS, dtype=s.dtype))


def _trunk(x, mods, states, lws, g_final, nb, tb):
    new_states = []
    for l in range(DEPTH):
        outs = _mixer_call(x, mods[l], lws[l], states[l], nb, tb)
        x = outs[0]
        new_states.append(outs[1:])
        x = _ffn_call(x, mods[l], lws[l], g_final, nb, tb, l == DEPTH - 1)
    return x, new_states


def _pack_states(conv, s_dn, s_hg, s_gla, c_ml, n_ml, m_ml, l):
    b = conv.shape[1]
    m_row = jnp.zeros((b, 1, 128), F32).at[:, 0, S_MLF:S_MLF + N_HEADS].set(m_ml[l].astype(F32))
    return (conv[l].astype(F32), _state_to_bd(s_dn[l].astype(F32)), _state_to_bd(s_hg[l].astype(F32)),
            _state_to_bd(s_gla[l].astype(F32)), _state_to_bd(c_ml[l].astype(F32)),
            n_ml[l].astype(F32).reshape(b, 1, N_HEADS * ML_DK), m_row)


def _unpack_states(sts):
    conv = jnp.stack([s[0] for s in sts])
    s_dn = jnp.stack([_state_from_bd(s[1], DN_DK) for s in sts])
    s_hg = jnp.stack([_state_from_bd(s[2], HG_DK) for s in sts])
    s_gla = jnp.stack([_state_from_bd(s[3], GLA_DK) for s in sts])
    c_ml = jnp.stack([_state_from_bd(s[4], ML_DK) for s in sts])
    n_ml = jnp.stack([s[5].reshape(s[5].shape[0], N_HEADS, ML_DK) for s in sts])
    m_ml = jnp.stack([s[6][:, 0, S_MLF:S_MLF + N_HEADS] for s in sts])
    return conv, s_dn, s_hg, s_gla, c_ml, n_ml, m_ml


def _zero_packed(b):
    z = lambda *s: jnp.zeros(s, F32)
    return (z(b, DN_CONV - 1, DN_CONV_CH), z(b, GROUP_W, N_HEADS * DN_DK), z(b, GROUP_W, N_HEADS * HG_DK),
            z(b, GROUP_W, GLA_W), z(b, GROUP_W, N_HEADS * ML_DK), z(b, 1, N_HEADS * ML_DK), z(b, 1, 128))


def _tiling(bsz, t):
    tb = min(t, 256)
    nb = max(1, min(bsz, 256 // tb))
    while bsz % nb:
        nb -= 1
    return nb, tb


def kernel(x_prompt, x_sample, c_prompt, c_sample, cache_dn_conv, state_dn, state_hgrn, state_gla, state_mlstm_c, state_mlstm_n, state_mlstm_m, w_ada, b_ada, g_mix, g_ffn, w_in, dn_conv_w, dn_a_log, dn_dt_bias, dn_norm_g, hg_lb_logits, hg_norm_g, gla_w_up, gla_b_up, gla_norm_g, ml_i_bias, ml_f_bias, ml_norm_g, w_out, w_up, w_down, g_final):
    p = dict(g_mix=g_mix, g_ffn=g_ffn, w_in=w_in, dn_conv_w=dn_conv_w, dn_a_log=dn_a_log, dn_dt_bias=dn_dt_bias,
             dn_norm_g=dn_norm_g, hg_lb_logits=hg_lb_logits, hg_norm_g=hg_norm_g, gla_w_up=gla_w_up,
             gla_b_up=gla_b_up, gla_norm_g=gla_norm_g, ml_i_bias=ml_i_bias, ml_f_bias=ml_f_bias,
             ml_norm_g=ml_norm_g, w_out=w_out, w_up=w_up, w_down=w_down)
    lws = [_layer_weights(p, l) for l in range(DEPTH)]
    gfin = g_final.reshape(1, D_MODEL).astype(F32)
    bp, bs = x_prompt.shape[0], x_sample.shape[0]
    mod = _ada_call(jnp.concatenate([c_prompt, c_sample], axis=0).astype(F32), w_ada.astype(BF16),
                    b_ada.astype(F32)).reshape(DEPTH, bp + bs, 6, D_MODEL)

    outs = []
    raw = (cache_dn_conv, state_dn, state_hgrn, state_gla, state_mlstm_c, state_mlstm_n, state_mlstm_m)
    for x, lo, hi, states in ((x_prompt, 0, bp, [_zero_packed(bp)] * DEPTH),
                              (x_sample, bp, bp + bs, [_pack_states(*raw, l) for l in range(DEPTH)])):
        nb, tb = _tiling(x.shape[0], x.shape[1])
        y, new = _trunk(x.astype(F32), [mod[l, lo:hi] for l in range(DEPTH)], states, lws, gfin, nb, tb)
        outs.append((y, _unpack_states(new)))
    (y_p, st_p), (y_s, st_s) = outs
    return (y_p, y_s) + tuple(st_p) + tuple(st_s)
```

```python
import functools

import numpy as np
import jax
import jax.numpy as jnp
from jax import lax
from jax.experimental import pallas as pl
from jax.experimental.pallas import tpu as pltpu

F32 = jnp.float32
BF16 = jnp.bfloat16

D_MODEL = 1024
DEPTH = 2
CHUNK = 64
N_HEADS = 4
HEAD_V = 64
GROUP_W = N_HEADS * HEAD_V
MIX_W = 4 * GROUP_W
DN_DK = 64
DN_CONV = 4
DN_CONV_CH = 3 * GROUP_W
HG_DK = 64
GLA_DK = 32
GLA_W = N_HEADS * GLA_DK
GLA_RANK = 16
GLA_TAU = 16.0
ML_DK = 64
D_FF = 2816
EPS = 1e-6

Z_DNQKV = 0
Z_DNG = 768
Z_HGQ = 1024
Z_HGF = 1280
Z_HGI = 1536
Z_HGG = 1792
Z_GLAQ = 2048
Z_GLAK = 2176
Z_GLAV = 2304
Z_GLAG = 2560
Z_MLQ = 2816
Z_MLK = 3072
Z_MLV = 3328
Z_MLO = 3584
Z_SMALL = 3840
N_IN = 3968
S_DNB = 0
S_GLAR = 16
S_DNA = 32
S_MLI = 64
S_MLF = 96

SUB = 8
VMEM_LIMIT = 56 * 1024 * 1024

M_INCL, M_STRICT, M_DIAG, M_SAME16 = 0, 1, 2, 3
M_LEVEL0 = 4
LOG2E = 1.4426950408889634
MIXER_ROWS = 256
FFN_ROWS = 512
MAX_INTERLEAVE = 4


def _sigmoid(x):
    return 1.0 / (1.0 + jnp.exp(-x))


def _silu(x):
    return x * _sigmoid(x)


def _log_sigmoid(x):
    return jnp.minimum(x, 0.0) - jnp.log1p(jnp.exp(-jnp.abs(x)))


def _softplus(x):
    return jnp.maximum(x, 0.0) + jnp.log1p(jnp.exp(-jnp.abs(x)))


def _dot(a, b):
    return jnp.dot(a.astype(BF16), b.astype(BF16), preferred_element_type=F32)


def _dot_nt(a, b):
    return lax.dot_general(a.astype(BF16), b.astype(BF16), (((1,), (1,)), ((), ())),
                           preferred_element_type=F32)


def _dot_tn(a, b):
    return lax.dot_general(a.astype(BF16), b.astype(BF16), (((0,), (0,)), ((), ())),
                           preferred_element_type=F32)


def _split3(x):
    x1 = x.astype(BF16)
    r = x - x1.astype(F32)
    x2 = r.astype(BF16)
    r = r - x2.astype(F32)
    return x1, x2, r.astype(BF16)


def _move_r(x, sel):
    x1, x2, x3 = _split3(x)
    d = lambda a: jnp.dot(a, sel, preferred_element_type=F32)
    return d(x1) + (d(x2) + d(x3))


def _move_l(sel, x):
    x1, x2, x3 = _split3(x)
    d = lambda a: jnp.dot(sel, a, preferred_element_type=F32)
    return d(x1) + (d(x2) + d(x3))


def _bd(x, mask):
    xb = x.astype(BF16)
    return jnp.concatenate([xb] * N_HEADS, axis=0) * mask


def _scan0(x, op, fill):
    n = x.shape[0]
    row = lax.broadcasted_iota(jnp.int32, x.shape, 0)
    sh = 1
    while sh < n:
        r = pltpu.roll(x, sh, axis=0)
        x = op(x, jnp.where(row >= sh, r, fill))
        sh *= 2
    return x


def _rowform(xe, diag):
    return jnp.sum(xe * diag, axis=0, keepdims=True)


def _rmsnorm_rows(x, g):
    return x * lax.rsqrt(jnp.mean(x * x, axis=-1, keepdims=True) + EPS) * g


def _head_norm_gate(o, ones_bd, g_row, gate):
    ms = _dot(o * o, ones_bd) * (1.0 / HEAD_V)
    return o * lax.rsqrt(ms + EPS) * g_row * gate


def _run_interleaved(gens):
    live = list(gens)
    while live:
        alive = []
        for g in live:
            try:
                next(g)
                alive.append(g)
            except StopIteration:
                pass
        live = alive


def _await(boxes, key):
    while key not in boxes:
        yield
    return boxes[key]


def _tri_solve(mm, rhs, masks, bdp, bdr, c):
    assert c // 16 <= 4
    eye = masks[M_DIAG]
    mul = lambda a, b: _dot(a, _bd(b, bdp))
    app = lambda a, r: _dot(a, _bd(r, bdr))
    md = mm * masks[M_SAME16]
    mo = mm - md
    p2 = mul(md, md)
    yield
    d = eye - md
    d = d + mul(d, p2)
    p4 = mul(p2, p2)
    yield
    d = d + mul(d, p4)
    p8 = mul(p4, p4)
    yield
    d = d + mul(d, p8)
    yield
    n = mul(d, mo)
    ys = [app(d, r) for r in rhs]
    yield
    zs = [y - app(n, y) for y in ys]
    if c // 16 <= 2:
        yield
        return zs
    n2 = mul(n, n)
    yield
    ws = [z + app(n2, z) for z in zs]
    yield
    return ws


def _deltanet_chunk(q, k, v, beta_s, gam_s, get_state, kc, c):
    masks = kc['masks']
    gam_e = _move_r(gam_s, kc['e_c'][1])
    beta_e = _move_r(beta_s, kc['e_c'][0])
    if c == HEAD_V:
        gam_d, beta_d = gam_e, beta_e
    else:
        gam_d = _move_r(gam_s, kc['e_d'][1])
        beta_d = _move_r(beta_s, kc['e_d'][0])
    kb = _bd(k, kc['bd256'][...])
    kk = _dot_nt(k, kb)
    qk = _dot_nt(q, kb)
    yield
    gam_r = _rowform(gam_e, masks[M_DIAG])
    dec = jnp.exp(jnp.minimum(gam_e - gam_r, 0.0))
    mm = beta_e * kk * dec * masks[M_STRICT]
    eg = jnp.exp(gam_d)
    w, u0 = yield from _tri_solve(mm, [beta_d * eg * k, beta_d * v], masks, kc['bdp'][...], kc['bd256'][...], c)
    gl = gam_d[c - 1:c, :]
    kdec = k * jnp.exp(gl - gam_d)
    st = yield from get_state()
    u = u0 - _dot_nt(w, st)
    qs = _dot_nt(q * eg, st)
    yield
    st_new = jnp.exp(gl) * st + kc['st256'][...] * _dot_tn(u, kdec)
    o = qs + _dot(qk * dec * masks[M_INCL], _bd(u, kc['bd256'][...]))
    return o, st_new


def _block_ref(b, sz):
    c, w = b.shape
    g3 = b.reshape(c // (2 * sz), 2 * sz, w)
    return jnp.broadcast_to(g3[:, sz - 1:sz, :], g3.shape).reshape(c, w)


def _gla_chunk(q, k, v, g, get_state, kc, c, wide):
    masks = kc['masks']
    bdk = kc['bd256'][...] if wide else kc['bd128'][...]
    ie = kc['ie256'][...] if wide else kc['ie128'][...]
    stm = kc['st256'][...] if wide else kc['st128'][...]
    g2 = g * LOG2E
    bd2 = _move_l(kc['lmat'][...], g2)
    b = bd2[0:c, :]
    yield
    row = lax.broadcasted_iota(jnp.int32, b.shape, 0)
    attn = _dot(q * k, ie) * masks[M_DIAG]
    lvl = 0
    sz = c // 2
    while sz >= 1:
        if sz >= 4:
            d = b - _block_ref(b, sz)
        elif sz == 2:
            d = bd2[c:2 * c, :]
        else:
            d = jnp.where(jnp.bitwise_and(row, 1) == 1, g2, 0.0)
        e = jnp.exp2(jnp.minimum(d, -d))
        attn = attn + _dot_nt(q * e, _bd(k * e, bdk)) * masks[M_LEVEL0 + lvl]
        yield
        sz //= 2
        lvl += 1
    last = b[c - 1:c, :]
    upd = stm * _dot_tn(v, k * jnp.exp2(last - b))
    o = _dot(attn, _bd(v, kc['bd256'][...]))
    yield
    st = yield from get_state()
    st_new = jnp.exp2(last) * st + upd
    o = o + _dot_nt(q * jnp.exp2(b), st)
    return o, st_new


def _mlstm_chunk(q, k, v, ig_s, lf_s, get_m, put_m, get_cn, kc, c):
    masks = kc['masks']
    fcum = _move_l(kc['lmat'][0:c, :], lf_s)
    a = ig_s - fcum
    imax = fcum + _scan0(a, jnp.maximum, -jnp.inf)
    fl = fcum[c - 1:c, :]
    lw = fl - fcum + ig_s
    lw_max = jnp.max(lw, axis=0, keepdims=True)
    kb = _bd(k, kc['bd256'][...])
    qk = _dot_nt(q, kb)
    a_r = _rowform(_move_r(a, kc['e_c'][2]), masks[M_DIAG])
    yield
    m_row = yield from get_m()
    m_new = jnp.maximum(fl + m_row, lw_max)
    put_m(m_new)
    inter = fcum + m_row
    mt = jnp.maximum(inter, imax)
    x1_e = _move_r(fcum - mt, kc['e_c'][2])
    w_inter = jnp.exp(_move_r(inter - mt, kc['e_d'][2]))
    floor = jnp.exp(_move_r(-mt, kc['e_d'][2]))
    ws = jnp.exp(_move_r(lw - m_new, kc['e_d'][2]))
    wc8 = jnp.broadcast_to(fl + m_row - m_new, (SUB, 128))
    wc = jnp.exp(_move_r(wc8, kc['e_d'][2]))[0:1, :]
    yield
    w_intra = jnp.exp(jnp.minimum(x1_e + a_r, 0.0)) * masks[M_INCL] * qk
    num = _dot(w_intra, _bd(v, kc['bd256'][...]))
    den = _dot(w_intra, kc['iep'][...])
    upd = kc['st256'][...] * _dot_tn(ws * v, k)
    n_upd = jnp.sum(ws * k, axis=0, keepdims=True)
    yield
    ct, n_row = yield from get_cn()
    ct_new = wc * ct + upd
    n_new = wc * n_row + n_upd
    num = num + w_inter * _dot_nt(q, ct)
    den = den + w_inter * _dot(q * n_row, kc['ones256'][...])
    hh = num / jnp.maximum(jnp.abs(den), floor)
    return hh, ct_new, n_new


def _mixer_kernel(x_ref, mod_ref, gmix_ref, win_ref, wout_ref, convw_ref, sp_ref, gn_ref, lbl_ref, lbs_ref,
                  wup_ref, bup_ref,
                  masks_ref, ec_ref, ed_ref, bd256_ref, bd128_ref, bdp_ref, st256_ref, st128_ref,
                  ie256_ref, ie128_ref, iep_ref, ones256_ref, lmat_ref,
                  conv0_ref, sdn0_ref, shg0_ref, sgla0_ref, c0_ref, n0_ref, m0_ref,
                  y_ref, convo_ref, sdn_ref, shg_ref, sgla_ref, cml_ref, nml_ref, mml_ref,
                  z_ref, xp_ref, qkv_ref, mix_ref, *, nb, tb, c, ilv):
    ti = pl.program_id(1)
    rows = nb * tb
    nchunk = tb // c

    @pl.when(ti == 0)
    def _():
        convo_ref[...] = conv0_ref[...]
        sdn_ref[...] = sdn0_ref[...]
        shg_ref[...] = shg0_ref[...]
        sgla_ref[...] = sgla0_ref[...]
        cml_ref[...] = c0_ref[...]
        nml_ref[...] = n0_ref[...]
        mml_ref[...] = m0_ref[...]

    x = x_ref[...]
    mod = mod_ref[...]
    h = _rmsnorm_rows(x, gmix_ref[...]) * (1.0 + mod[:, 1:2, :]) + mod[:, 0:1, :]
    z_ref[...] = jnp.dot(h.reshape(rows, D_MODEL).astype(BF16), win_ref[...], preferred_element_type=F32)

    convw = convw_ref[...]
    for b in range(nb):
        xp_ref[b, SUB - (DN_CONV - 1):SUB, :] = convo_ref[b]
        xp_ref[b, SUB:SUB + tb, :] = z_ref[b * tb:(b + 1) * tb, Z_DNQKV:Z_DNQKV + DN_CONV_CH]
        acc = xp_ref[b, SUB - 3:SUB - 3 + tb, :] * convw[0:1, :]
        for j in range(1, DN_CONV):
            acc = acc + xp_ref[b, SUB - 3 + j:SUB - 3 + j + tb, :] * convw[j:j + 1, :]
        qkv_ref[b * tb:(b + 1) * tb, :] = _silu(acc)
        convo_ref[b] = xp_ref[b, SUB + tb - (DN_CONV - 1):SUB + tb, :]

    kc = dict(masks=masks_ref, e_c=ec_ref, e_d=ed_ref, bd256=bd256_ref, bd128=bd128_ref, bdp=bdp_ref,
              st256=st256_ref, st128=st128_ref, ie256=ie256_ref, ie128=ie128_ref, iep=iep_ref,
              ones256=ones256_ref, lmat=lmat_ref)
    sp = sp_ref[...]
    gn = gn_ref[...]
    lbl = lbl_ref[...]
    lbs = lbs_ref[...]
    lbe = jnp.exp(lbl - jnp.max(lbl, axis=0, keepdims=True))
    lb = jnp.sum(lbs * (lbe / jnp.sum(lbe, axis=0, keepdims=True)), axis=0, keepdims=True)
    log_lb = jnp.log(lb)
    log_1mlb = jnp.log1p(-lb)
    neg_a = -jnp.exp(sp[1:2, :])
    chain = nchunk > 1

    def chunk_gens(i, u, boxes):
        if nchunk == 1:
            b = i
        elif nb == 1:
            b = 0
        else:
            b = i // nchunk
        rs = pl.ds(pl.multiple_of(i * c, c), c)
        zc = lambda off, w: z_ref[rs, off:off + w]

        def getter(key, read):
            def get():
                if chain and u > 0:
                    return (yield from _await(boxes, (key, u - 1)))
                return read()
                yield
            return get

        def norm_gate(o, row, gate):
            return _head_norm_gate(o, kc['ones256'][...], gn[row:row + 1, :], gate).astype(BF16)

        def dn():
            small = zc(Z_SMALL, 128)
            sb = small + sp[0:1, :]
            qkv = qkv_ref[rs, :]
            cq, ck, cv = qkv[:, 0:GROUP_W], qkv[:, GROUP_W:2 * GROUP_W], qkv[:, 2 * GROUP_W:3 * GROUP_W]
            ssq = _dot(cq * cq, kc['ones256'][...])
            ssk = _dot(ck * ck, kc['ones256'][...])
            beta_s = _sigmoid(small)
            gam_s = _move_l(kc['lmat'][0:c, :], neg_a * _softplus(sb))
            yield
            dq = cq * lax.rsqrt(ssq + EPS) * (DN_DK ** -0.5)
            dk = ck * lax.rsqrt(ssk + EPS)
            o, st_new = yield from _deltanet_chunk(dq, dk, cv, beta_s, gam_s,
                                                   getter('dn', lambda: sdn_ref[b]), kc, c)
            boxes[('dn', u)] = st_new
            sdn_ref[b] = st_new
            yield
            mix_ref[rs, 0:GROUP_W] = norm_gate(o, 0, _silu(zc(Z_DNG, GROUP_W)))

        def hg():
            zf = zc(Z_HGF, GROUP_W)
            lsz = _log_sigmoid(zf)
            t2 = log_1mlb + lsz
            mx = jnp.maximum(log_lb, t2)
            log_f = mx + jnp.log(jnp.exp(log_lb - mx) + jnp.exp(t2 - mx))
            key_hg = (1.0 - lb) * _sigmoid(-zf)
            o, st_new = yield from _gla_chunk(_silu(zc(Z_HGQ, GROUP_W)), key_hg, zc(Z_HGI, GROUP_W), log_f,
                                              getter('hg', lambda: shg_ref[b]), kc, c, True)
            boxes[('hg', u)] = st_new
            shg_ref[b] = st_new
            yield
            mix_ref[rs, GROUP_W:2 * GROUP_W] = norm_gate(o, 1, _silu(zc(Z_HGG, GROUP_W)))

        def gla():
            small = zc(Z_SMALL, 128)
            g_gla = _log_sigmoid(_dot(small, wup_ref[...]) + bup_ref[...]) * (1.0 / GLA_TAU)
            yield
            o, st_new = yield from _gla_chunk(zc(Z_GLAQ, GLA_W) * (GLA_DK ** -0.5), zc(Z_GLAK, GLA_W),
                                              zc(Z_GLAV, GROUP_W), g_gla,
                                              getter('gla', lambda: sgla_ref[b]), kc, c, False)
            boxes[('gla', u)] = st_new
            sgla_ref[b] = st_new
            yield
            mix_ref[rs, 2 * GROUP_W:3 * GROUP_W] = norm_gate(o, 2, _silu(zc(Z_GLAG, GROUP_W)))

        def ml():
            sb = zc(Z_SMALL, 128) + sp[0:1, :]
            ig_s = pltpu.roll(sb, S_MLF - S_MLI, axis=1)
            lf_s = _log_sigmoid(sb)

            def put_m(m_new):
                boxes[('ml_m', u)] = m_new
                mml_ref[b] = m_new

            hh, c_new, n_new = yield from _mlstm_chunk(
                zc(Z_MLQ, GROUP_W) * (ML_DK ** -0.5), zc(Z_MLK, GROUP_W), zc(Z_MLV, GROUP_W), ig_s, lf_s,
                getter('ml_m', lambda: mml_ref[b]), put_m,
                getter('ml_cn', lambda: (cml_ref[b], nml_ref[b])), kc, c)
            boxes[('ml_cn', u)] = (c_new, n_new)
            cml_ref[b] = c_new
            nml_ref[b] = n_new
            yield
            mix_ref[rs, 3 * GROUP_W:4 * GROUP_W] = norm_gate(hh, 3, _sigmoid(zc(Z_MLO, GROUP_W)))

        return [dn(), hg(), gla(), ml()]

    def chunk_body(it, carry):
        boxes = {}
        gens = []
        for u in range(ilv):
            gens += chunk_gens(it * ilv + u, u, boxes)
        _run_interleaved(gens)
        return carry

    lax.fori_loop(0, nb * nchunk // ilv, chunk_body, 0)

    out = jnp.dot(mix_ref[...], wout_ref[...], preferred_element_type=F32).reshape(nb, tb, D_MODEL)
    y_ref[...] = x + mod[:, 2:3, :] * out


def _const_tables(c):
    pc = N_HEADS * c
    t = np.arange(c)[:, None]
    lane = np.arange(pc)[None, :]
    hs, s = lane // c, lane % c
    masks = [s <= t, s < t, s == t, (s // 16) == (t // 16)]
    sz = c // 2
    while sz >= 1:
        masks.append(((s // (2 * sz)) == (t // (2 * sz))) & ((t // sz) % 2 == 1) & ((s // sz) % 2 == 0))
        sz //= 2
    masks = np.stack([np.broadcast_to(m, (c, pc)) for m in masks]).astype(np.float32)
    r = np.arange(c)[None, :]
    tri = (r <= t).astype(np.float32)
    lmat = np.concatenate([tri, tri - tri[(np.arange(c) // 4) * 4 + 1]], axis=0)

    def expand(col0, w):
        j = np.arange(128)[:, None]
        l = np.arange(N_HEADS * w)[None, :]
        return (j == col0 + l // w).astype(np.float32)

    e_c = np.stack([expand(S_DNB, c), expand(S_DNA, c), expand(S_MLF, c)])
    e_d = np.stack([expand(S_DNB, HEAD_V), expand(S_DNA, HEAD_V), expand(S_MLF, HEAD_V)])

    def blk(nr, rg, nl, lg):
        return ((np.arange(nr)[:, None] // rg) == (np.arange(nl)[None, :] // lg)).astype(np.float32)

    tabs = dict(
        masks=jnp.asarray(masks), e_c=jnp.asarray(e_c, BF16), e_d=jnp.asarray(e_d, BF16),
        bd256=jnp.asarray(blk(pc, c, GROUP_W, HEAD_V), BF16), bd128=jnp.asarray(blk(pc, c, GLA_W, GLA_DK), BF16),
        bdp=jnp.asarray(blk(pc, c, pc, c), BF16),
        st256=jnp.asarray(blk(GROUP_W, HEAD_V, GROUP_W, HEAD_V)), st128=jnp.asarray(blk(GROUP_W, HEAD_V, GLA_W, GLA_DK)),
        ie256=jnp.asarray(blk(GROUP_W, HEAD_V, pc, c), BF16), ie128=jnp.asarray(blk(GLA_W, GLA_DK, pc, c), BF16),
        iep=jnp.asarray(blk(pc, c, GROUP_W, HEAD_V), BF16),
        ones256=jnp.asarray(blk(GROUP_W, HEAD_V, GROUP_W, HEAD_V), BF16), lmat=jnp.asarray(lmat, BF16))
    order = ['masks', 'e_c', 'e_d', 'bd256', 'bd128', 'bdp', 'st256', 'st128', 'ie256', 'ie128', 'iep', 'ones256',
             'lmat']
    return [tabs[k] for k in order]


def _full_spec(a):
    nd = a.ndim
    return pl.BlockSpec(a.shape, lambda bi, ti, _n=nd: (0,) * _n, pipeline_mode=pl.Buffered(1))


def _mixer_call(x, mod, lw, states, nb, tb):
    bsz, t, _ = x.shape
    c = min(CHUNK, t)
    assert t % tb == 0 and tb % c == 0 and bsz % nb == 0 and c % 16 == 0
    rows = nb * tb
    consts = _const_tables(c)
    params = [lw['g_mix'], lw['w_in'], lw['w_out'], lw['conv_w'], lw['sp'], lw['gn'], lw['lb_logits'], lw['lb_sel'],
              lw['wup'], lw['bup']]
    xspec = pl.BlockSpec((nb, tb, D_MODEL), lambda bi, ti: (bi, ti, 0))

    def bspec(a):
        nd = a.ndim
        return pl.BlockSpec((nb,) + a.shape[1:], lambda bi, ti, _n=nd: (bi,) + (0,) * (_n - 1))

    in_specs = ([xspec, bspec(mod)] + [_full_spec(a) for a in params] + [_full_spec(a) for a in consts]
                + [bspec(s) for s in states])
    out_shape = [jax.ShapeDtypeStruct(x.shape, F32)] + [jax.ShapeDtypeStruct(s.shape, F32) for s in states]
    out_specs = [xspec] + [bspec(s) for s in states]
    scratch = [pltpu.VMEM((rows, N_IN), F32),
               pltpu.VMEM((nb, SUB + tb, DN_CONV_CH), F32),
               pltpu.VMEM((rows, DN_CONV_CH), F32),
               pltpu.VMEM((rows, MIX_W), BF16)]
    ilv = MAX_INTERLEAVE
    while (nb * (tb // c)) % ilv or (tb // c > 1 and (tb // c) % ilv):
        ilv //= 2
    kern = functools.partial(_mixer_kernel, nb=nb, tb=tb, c=c, ilv=ilv)
    return pl.pallas_call(
        kern, grid=(bsz // nb, t // tb), in_specs=in_specs, out_specs=out_specs, out_shape=out_shape,
        scratch_shapes=scratch, name='mixer',
        compiler_params=pltpu.CompilerParams(dimension_semantics=('arbitrary', 'arbitrary'),
                                             vmem_limit_bytes=VMEM_LIMIT),
    )(x, mod, *params, *consts, *states)


FF_TILE = 256


def _ffn_kernel(x_ref, mod_ref, gffn_ref, wup_ref, wdown_ref, gfin_ref, y_ref, *, nb, tb, final):
    rows = nb * tb
    x = x_ref[...]
    mod = mod_ref[...]
    h = _rmsnorm_rows(x, gffn_ref[...]) * (1.0 + mod[:, 4:5, :]) + mod[:, 3:4, :]
    hb = h.reshape(rows, D_MODEL).astype(BF16)
    acc = jnp.zeros((rows, D_MODEL), F32)
    for j in range(D_FF // FF_TILE):
        gate = jnp.dot(hb, wup_ref[:, j * FF_TILE:(j + 1) * FF_TILE], preferred_element_type=F32)
        up = jnp.dot(hb, wup_ref[:, D_FF + j * FF_TILE:D_FF + (j + 1) * FF_TILE], preferred_element_type=F32)
        act = (_silu(gate) * up).astype(BF16)
        acc = acc + jnp.dot(act, wdown_ref[j * FF_TILE:(j + 1) * FF_TILE, :], preferred_element_type=F32)
    y = x + mod[:, 5:6, :] * acc.reshape(nb, tb, D_MODEL)
    if final:
        y = _rmsnorm_rows(y, gfin_ref[...])
    y_ref[...] = y


def _ffn_call(x, mod, lw, g_final, nb, tb, final):
    bsz, t, _ = x.shape
    xspec = pl.BlockSpec((nb, tb, D_MODEL), lambda bi, ti: (bi, ti, 0))
    mspec = pl.BlockSpec((nb, 6, D_MODEL), lambda bi, ti: (bi, 0, 0))
    params = [lw['g_ffn'], lw['w_up'], lw['w_down'], g_final]
    kern = functools.partial(_ffn_kernel, nb=nb, tb=tb, final=final)
    return pl.pallas_call(
        kern, grid=(bsz // nb, t // tb), in_specs=[xspec, mspec] + [_full_spec(a) for a in params],
        out_specs=xspec, out_shape=jax.ShapeDtypeStruct(x.shape, F32), name='ffn',
        compiler_params=pltpu.CompilerParams(dimension_semantics=('arbitrary', 'arbitrary'),
                                             vmem_limit_bytes=VMEM_LIMIT),
    )(x, mod, *params)


ADA_TILE = 1536


def _ada_kernel(c_ref, w_ref, b_ref, o_ref):
    o_ref[0] = jnp.dot(_silu(c_ref[...]).astype(BF16), w_ref[0], preferred_element_type=F32) + b_ref[0]


def _ada_call(c_all, w_ada, b_ada):
    n = c_all.shape[0]
    nt = 6 * D_MODEL // ADA_TILE
    return pl.pallas_call(
        _ada_kernel, grid=(DEPTH, nt),
        in_specs=[pl.BlockSpec((n, D_MODEL), lambda l, j: (0, 0)),
                  pl.BlockSpec((1, D_MODEL, ADA_TILE), lambda l, j: (l, 0, j)),
                  pl.BlockSpec((1, 1, ADA_TILE), lambda l, j: (l, 0, j))],
        out_specs=pl.BlockSpec((1, n, ADA_TILE), lambda l, j: (l, 0, j)),
        out_shape=jax.ShapeDtypeStruct((DEPTH, n, 6 * D_MODEL), F32), name='ada',
        compiler_params=pltpu.CompilerParams(dimension_semantics=('arbitrary', 'arbitrary')),
    )(c_all, w_ada, b_ada.reshape(DEPTH, 1, 6 * D_MODEL))


def _permute_w_in(w):
    d = w.shape[0]
    zeros = lambda n: jnp.zeros((d, n), w.dtype)
    small = jnp.concatenate([w[:, 768:772], zeros(12), w[:, 2568:2584], w[:, 772:776], zeros(28),
                             w[:, 3608:3612], zeros(28), w[:, 3612:3616], zeros(28)], axis=1)
    return jnp.concatenate([w[:, 0:768], w[:, 776:2568], w[:, 2584:3608], w[:, 3616:3872], small], axis=1)


def _lane_row(pairs, width=128):
    row = jnp.zeros((width,), F32)
    for off, val in pairs:
        row = row.at[off:off + val.shape[0]].set(val.astype(F32))
    return row


def _layer_weights(p, l):
    sp = jnp.zeros((SUB, 128), F32)
    sp = sp.at[0].set(_lane_row([(S_DNA, p['dn_dt_bias'][l]), (S_MLI, p['ml_i_bias'][l]), (S_MLF, p['ml_f_bias'][l])]))
    sp = sp.at[1].set(_lane_row([(S_DNA, p['dn_a_log'][l])]))
    gn = jnp.stack([jnp.tile(p[k][l].astype(F32), N_HEADS) for k in ('dn_norm_g', 'hg_norm_g', 'gla_norm_g', 'ml_norm_g')])
    wup = jnp.zeros((128, GLA_W), F32).at[S_GLAR:S_GLAR + GLA_RANK].set(p['gla_w_up'][l]).astype(BF16)
    lb_sel = (jnp.arange(DEPTH) >= 1) & (jnp.arange(DEPTH) <= l)
    return dict(
        g_mix=p['g_mix'][l].reshape(1, D_MODEL), g_ffn=p['g_ffn'][l].reshape(1, D_MODEL),
        w_in=_permute_w_in(p['w_in'][l]).astype(BF16), w_out=p['w_out'][l].astype(BF16),
        conv_w=p['dn_conv_w'][l], sp=sp, gn=gn, lb_logits=p['hg_lb_logits'].astype(F32),
        lb_sel=lb_sel.astype(F32).reshape(DEPTH, 1), wup=wup, bup=p['gla_b_up'][l].reshape(1, GLA_W).astype(F32),
        w_up=p['w_up'][l].astype(BF16), w_down=p['w_down'][l].astype(BF16))


def _state_to_bd(s):
    b, h, dk, dv = s.shape
    eye = jnp.eye(h, dtype=s.dtype)
    return jnp.einsum('bhkv,hg->bhvgk', s, eye).reshape(b, h * dv, h * dk)


def _state_from_bd(s, dk):
    b = s.shape[0]
    s5 = s.reshape(b, N_HEADS, HEAD_V, N_HEADS, dk)
    return jnp.einsum('bhvgk,hg->bhkv', s5, jnp.eye(N_HEADS, dtype=s.dtype))


def _trunk(x, mods, states, lws, g_final, nb, tb):
    new_states = []
    for l in range(DEPTH):
        outs = _mixer_call(x, mods[l], lws[l], states[l], nb, tb)
        x = outs[0]
        new_states.append(outs[1:])
        x = _ffn_call(x, mods[l], lws[l], g_final, *_tiling(x.shape[0], x.shape[1], FFN_ROWS), l == DEPTH - 1)
    return x, new_states


def _pack_states(conv, s_dn, s_hg, s_gla, c_ml, n_ml, m_ml, l):
    b = conv.shape[1]
    m_row = jnp.zeros((b, 1, 128), F32).at[:, 0, S_MLF:S_MLF + N_HEADS].set(m_ml[l].astype(F32))
    return (conv[l].astype(F32), _state_to_bd(s_dn[l].astype(F32)), _state_to_bd(s_hg[l].astype(F32)),
            _state_to_bd(s_gla[l].astype(F32)), _state_to_bd(c_ml[l].astype(F32)),
            n_ml[l].astype(F32).reshape(b, 1, N_HEADS * ML_DK), m_row)


def _unpack_states(sts):
    conv = jnp.stack([s[0] for s in sts])
    s_dn = jnp.stack([_state_from_bd(s[1], DN_DK) for s in sts])
    s_hg = jnp.stack([_state_from_bd(s[2], HG_DK) for s in sts])
    s_gla = jnp.stack([_state_from_bd(s[3], GLA_DK) for s in sts])
    c_ml = jnp.stack([_state_from_bd(s[4], ML_DK) for s in sts])
    n_ml = jnp.stack([s[5].reshape(s[5].shape[0], N_HEADS, ML_DK) for s in sts])
    m_ml = jnp.stack([s[6][:, 0, S_MLF:S_MLF + N_HEADS] for s in sts])
    return conv, s_dn, s_hg, s_gla, c_ml, n_ml, m_ml


def _zero_packed(b):
    z = lambda *s: jnp.zeros(s, F32)
    return (z(b, DN_CONV - 1, DN_CONV_CH), z(b, GROUP_W, N_HEADS * DN_DK), z(b, GROUP_W, N_HEADS * HG_DK),
            z(b, GROUP_W, GLA_W), z(b, GROUP_W, N_HEADS * ML_DK), z(b, 1, N_HEADS * ML_DK), z(b, 1, 128))


def _tiling(bsz, t, rows):
    tb = min(t, rows)
    nb = max(1, min(bsz, rows // tb))
    while bsz % nb:
        nb -= 1
    return nb, tb


def kernel(x_prompt, x_sample, c_prompt, c_sample, cache_dn_conv, state_dn, state_hgrn, state_gla, state_mlstm_c, state_mlstm_n, state_mlstm_m, w_ada, b_ada, g_mix, g_ffn, w_in, dn_conv_w, dn_a_log, dn_dt_bias, dn_norm_g, hg_lb_logits, hg_norm_g, gla_w_up, gla_b_up, gla_norm_g, ml_i_bias, ml_f_bias, ml_norm_g, w_out, w_up, w_down, g_final):
    p = dict(g_mix=g_mix, g_ffn=g_ffn, w_in=w_in, dn_conv_w=dn_conv_w, dn_a_log=dn_a_log, dn_dt_bias=dn_dt_bias,
             dn_norm_g=dn_norm_g, hg_lb_logits=hg_lb_logits, hg_norm_g=hg_norm_g, gla_w_up=gla_w_up,
             gla_b_up=gla_b_up, gla_norm_g=gla_norm_g, ml_i_bias=ml_i_bias, ml_f_bias=ml_f_bias,
             ml_norm_g=ml_norm_g, w_out=w_out, w_up=w_up, w_down=w_down)
    lws = [_layer_weights(p, l) for l in range(DEPTH)]
    gfin = g_final.reshape(1, D_MODEL).astype(F32)
    bp, bs = x_prompt.shape[0], x_sample.shape[0]
    mod = _ada_call(jnp.concatenate([c_prompt, c_sample], axis=0).astype(F32), w_ada.astype(BF16),
                    b_ada.astype(F32)).reshape(DEPTH, bp + bs, 6, D_MODEL)

    outs = []
    raw = (cache_dn_conv, state_dn, state_hgrn, state_gla, state_mlstm_c, state_mlstm_n, state_mlstm_m)
    for x, lo, hi, states in ((x_prompt, 0, bp, [_zero_packed(bp)] * DEPTH),
                              (x_sample, bp, bp + bs, [_pack_states(*raw, l) for l in range(DEPTH)])):
        nb, tb = _tiling(x.shape[0], x.shape[1], MIXER_ROWS)
        y, new = _trunk(x.astype(F32), [mod[l, lo:hi] for l in range(DEPTH)], states, lws, gfin, nb, tb)
        outs.append((y, _unpack_states(new)))
    (y_p, st_p), (y_s, st_s) = outs
    return (y_p, y_s) + tuple(st_p) + tuple(st_s)
```

```python
import functools

import numpy as np
import jax
import jax.numpy as jnp
from jax import lax
from jax.experimental import pallas as pl
from jax.experimental.pallas import tpu as pltpu

F32 = jnp.float32
BF16 = jnp.bfloat16

D_MODEL = 1024
DEPTH = 2
CHUNK = 64
N_HEADS = 4
HEAD_V = 64
GROUP_W = N_HEADS * HEAD_V
MIX_W = 4 * GROUP_W
DN_DK = 64
DN_CONV = 4
DN_CONV_CH = 3 * GROUP_W
HG_DK = 64
GLA_DK = 32
GLA_W = N_HEADS * GLA_DK
GLA_RANK = 16
GLA_TAU = 16.0
ML_DK = 64
D_FF = 2816
EPS = 1e-6

Z_DNQKV = 0
Z_DNG = 768
Z_HGQ = 1024
Z_HGF = 1280
Z_HGI = 1536
Z_HGG = 1792
Z_GLAQ = 2048
Z_GLAK = 2176
Z_GLAV = 2304
Z_GLAG = 2560
Z_MLQ = 2816
Z_MLK = 3072
Z_MLV = 3328
Z_MLO = 3584
Z_SMALL = 3840
N_IN = 3968
S_DNB = 0
S_GLAR = 16
S_DNA = 32
S_MLI = 64
S_MLF = 96

SUB = 8
VMEM_LIMIT = 56 * 1024 * 1024

M_INCL, M_STRICT, M_DIAG, M_SAME16 = 0, 1, 2, 3
M_LEVEL0 = 4
LOG2E = 1.4426950408889634
MIXER_ROWS = 256
FFN_ROWS = 512
MAX_INTERLEAVE = 4


def _sigmoid(x):
    return 1.0 / (1.0 + jnp.exp(-x))


def _silu(x):
    return x * _sigmoid(x)


def _log_sigmoid(x):
    return jnp.minimum(x, 0.0) - jnp.log1p(jnp.exp(-jnp.abs(x)))


def _softplus(x):
    return jnp.maximum(x, 0.0) + jnp.log1p(jnp.exp(-jnp.abs(x)))


def _dot(a, b):
    return jnp.dot(a.astype(BF16), b.astype(BF16), preferred_element_type=F32)


def _dot_nt(a, b):
    return lax.dot_general(a.astype(BF16), b.astype(BF16), (((1,), (1,)), ((), ())),
                           preferred_element_type=F32)


def _dot_tn(a, b):
    return lax.dot_general(a.astype(BF16), b.astype(BF16), (((0,), (0,)), ((), ())),
                           preferred_element_type=F32)


def _split3(x):
    x1 = x.astype(BF16)
    r = x - x1.astype(F32)
    x2 = r.astype(BF16)
    r = r - x2.astype(F32)
    return x1, x2, r.astype(BF16)


def _move_r(x, sel):
    x1, x2, x3 = _split3(x)
    d = lambda a: jnp.dot(a, sel, preferred_element_type=F32)
    return d(x1) + (d(x2) + d(x3))


def _move_l(sel, x):
    x1, x2, x3 = _split3(x)
    d = lambda a: jnp.dot(sel, a, preferred_element_type=F32)
    return d(x1) + (d(x2) + d(x3))


def _move_tn(x, sel):
    x1, x2, x3 = _split3(x)
    d = lambda a: lax.dot_general(a, sel, (((0,), (0,)), ((), ())), preferred_element_type=F32)
    return d(x1) + (d(x2) + d(x3))


def _move_nt(sel, x):
    x1, x2, x3 = _split3(x)
    d = lambda a: lax.dot_general(sel, a, (((1,), (1,)), ((), ())), preferred_element_type=F32)
    return d(x1) + (d(x2) + d(x3))


def _bd(x, mask):
    xb = x.astype(BF16)
    return jnp.concatenate([xb] * N_HEADS, axis=0) * mask


def _scan0(x, op, fill):
    n = x.shape[0]
    row = lax.broadcasted_iota(jnp.int32, x.shape, 0)
    sh = 1
    while sh < n:
        r = pltpu.roll(x, sh, axis=0)
        x = op(x, jnp.where(row >= sh, r, fill))
        sh *= 2
    return x


def _rowform(xe, diag):
    return jnp.sum(xe * diag, axis=0, keepdims=True)


def _rmsnorm_rows(x, g):
    return x * lax.rsqrt(jnp.mean(x * x, axis=-1, keepdims=True) + EPS) * g


def _head_norm_gate(o, ones_bd, g_row, gate):
    ms = _dot(o * o, ones_bd) * (1.0 / HEAD_V)
    return o * lax.rsqrt(ms + EPS) * g_row * gate


def _run_interleaved(gens):
    live = list(gens)
    while live:
        alive = []
        for g in live:
            try:
                next(g)
                alive.append(g)
            except StopIteration:
                pass
        live = alive


def _await(boxes, key):
    while key not in boxes:
        yield
    return boxes[key]


def _tri_solve(mm, rhs, masks, bdp, bdr, c):
    assert c // 16 <= 4
    eye = masks[M_DIAG]
    mul = lambda a, b: _dot(a, _bd(b, bdp))
    app = lambda a, r: _dot(a, _bd(r, bdr))
    md = mm * masks[M_SAME16]
    mo = mm - md
    p2 = mul(md, md)
    yield
    d = eye - md
    d = d + mul(d, p2)
    p4 = mul(p2, p2)
    yield
    d = d + mul(d, p4)
    p8 = mul(p4, p4)
    yield
    d = d + mul(d, p8)
    yield
    n = mul(d, mo)
    ys = [app(d, r) for r in rhs]
    yield
    zs = [y - app(n, y) for y in ys]
    if c // 16 <= 2:
        yield
        return zs
    n2 = mul(n, n)
    yield
    ws = [z + app(n2, z) for z in zs]
    yield
    return ws


def _deltanet_chunk(q, k, v, beta_s, gam_s, get_state, kc, c):
    masks = kc['masks']
    gam_e = _move_r(gam_s, kc['e_c'][1])
    beta_e = _move_r(beta_s, kc['e_c'][0])
    if c == HEAD_V:
        gam_d, beta_d = gam_e, beta_e
    else:
        gam_d = _move_r(gam_s, kc['e_d'][1])
        beta_d = _move_r(beta_s, kc['e_d'][0])
    kb = _bd(k, kc['bd256'][...])
    kk = _dot_nt(k, kb)
    qk = _dot_nt(q, kb)
    yield
    gam_r = _rowform(gam_e, masks[M_DIAG])
    dec = jnp.exp(jnp.minimum(gam_e - gam_r, 0.0))
    mm = beta_e * kk * dec * masks[M_STRICT]
    eg = jnp.exp(gam_d)
    w, u0 = yield from _tri_solve(mm, [beta_d * eg * k, beta_d * v], masks, kc['bdp'][...], kc['bd256'][...], c)
    gl = gam_d[c - 1:c, :]
    kdec = k * jnp.exp(gl - gam_d)
    st = yield from get_state()
    u = u0 - _dot_nt(w, st)
    qs = _dot_nt(q * eg, st)
    yield
    st_new = jnp.exp(gl) * st + kc['st256'][...] * _dot_tn(u, kdec)
    o = qs + _dot(qk * dec * masks[M_INCL], _bd(u, kc['bd256'][...]))
    return o, st_new


def _block_ref(b, sz):
    c, w = b.shape
    g3 = b.reshape(c // (2 * sz), 2 * sz, w)
    return jnp.broadcast_to(g3[:, sz - 1:sz, :], g3.shape).reshape(c, w)


def _gla_chunk(q, k, v, g, get_state, kc, c, wide):
    masks = kc['masks']
    bdk = kc['bd256'][...] if wide else kc['bd128'][...]
    ie = kc['ie256'][...] if wide else kc['ie128'][...]
    stm = kc['st256'][...] if wide else kc['st128'][...]
    g2 = g * LOG2E
    bd2 = _move_l(kc['lmat'][...], g2)
    b = bd2[0:c, :]
    yield
    row = lax.broadcasted_iota(jnp.int32, b.shape, 0)
    attn = _dot(q * k, ie) * masks[M_DIAG]
    lvl = 0
    sz = c // 2
    while sz >= 1:
        if sz >= 4:
            d = b - _block_ref(b, sz)
        elif sz == 2:
            d = bd2[c:2 * c, :]
        else:
            d = jnp.where(jnp.bitwise_and(row, 1) == 1, g2, 0.0)
        e = jnp.exp2(jnp.minimum(d, -d))
        attn = attn + _dot_nt(q * e, _bd(k * e, bdk)) * masks[M_LEVEL0 + lvl]
        yield
        sz //= 2
        lvl += 1
    last = b[c - 1:c, :]
    upd = stm * _dot_tn(v, k * jnp.exp2(last - b))
    o = _dot(attn, _bd(v, kc['bd256'][...]))
    yield
    st = yield from get_state()
    st_new = jnp.exp2(last) * st + upd
    o = o + _dot_nt(q * jnp.exp2(b), st)
    return o, st_new


def _mlstm_chunk(q, k, v, ig_s, lf_s, get_m, put_m, get_cn, kc, c):
    masks = kc['masks']
    fcum = _move_l(kc['lmat'][0:c, :], lf_s)
    a = ig_s - fcum
    imax = fcum + _scan0(a, jnp.maximum, -jnp.inf)
    fl = fcum[c - 1:c, :]
    lw = fl - fcum + ig_s
    lw_max = jnp.max(lw, axis=0, keepdims=True)
    kb = _bd(k, kc['bd256'][...])
    qk = _dot_nt(q, kb)
    a_r = _rowform(_move_r(a, kc['e_c'][2]), masks[M_DIAG])
    yield
    m_row = yield from get_m()
    m_new = jnp.maximum(fl + m_row, lw_max)
    put_m(m_new)
    inter = fcum + m_row
    mt = jnp.maximum(inter, imax)
    x1_e = _move_r(fcum - mt, kc['e_c'][2])
    w_inter = jnp.exp(_move_r(inter - mt, kc['e_d'][2]))
    floor = jnp.exp(_move_r(-mt, kc['e_d'][2]))
    ws = jnp.exp(_move_r(lw - m_new, kc['e_d'][2]))
    wc8 = jnp.broadcast_to(fl + m_row - m_new, (SUB, 128))
    wc = jnp.exp(_move_r(wc8, kc['e_d'][2]))[0:1, :]
    yield
    w_intra = jnp.exp(jnp.minimum(x1_e + a_r, 0.0)) * masks[M_INCL] * qk
    num = _dot(w_intra, _bd(v, kc['bd256'][...]))
    den = _dot(w_intra, kc['iep'][...])
    upd = kc['st256'][...] * _dot_tn(ws * v, k)
    n_upd = jnp.sum(ws * k, axis=0, keepdims=True)
    yield
    ct, n_row = yield from get_cn()
    ct_new = wc * ct + upd
    n_new = wc * n_row + n_upd
    num = num + w_inter * _dot_nt(q, ct)
    den = den + w_inter * _dot(q * n_row, kc['ones256'][...])
    hh = num / jnp.maximum(jnp.abs(den), floor)
    return hh, ct_new, n_new


def _mixer_kernel(x_ref, mod_ref, gmix_ref, win_ref, wout_ref, convw_ref, sp_ref, gn_ref, lbl_ref, lbs_ref,
                  wup_ref, bup_ref,
                  masks_ref, ec_ref, ed_ref, bd256_ref, bd128_ref, bdp_ref, st256_ref, st128_ref,
                  ie256_ref, ie128_ref, iep_ref, ones256_ref, lmat_ref, sel64_ref, sel32_ref,
                  conv0_ref, sdn0_ref, shg0_ref, sgla0_ref, c0_ref, n0_ref, m0_ref,
                  y_ref, convo_ref, sdno_ref, shgo_ref, sglao_ref, cmlo_ref, nml_ref, mml_ref,
                  z_ref, xp_ref, qkv_ref, mix_ref, sdn_ref, shg_ref, sgla_ref, cml_ref, *, nb, tb, c, ilv):
    ti = pl.program_id(1)
    rows = nb * tb
    nchunk = tb // c
    mats = ((sdn0_ref, sdno_ref, sdn_ref, sel64_ref), (shg0_ref, shgo_ref, shg_ref, sel64_ref),
            (sgla0_ref, sglao_ref, sgla_ref, sel32_ref), (c0_ref, cmlo_ref, cml_ref, sel64_ref))

    @pl.when(ti == 0)
    def _():
        convo_ref[...] = conv0_ref[...]
        nml_ref[...] = n0_ref[...]
        mml_ref[...] = m0_ref[...]
        for raw_ref, _, st_ref, sel_ref in mats:
            for b in range(nb):
                st_ref[b] = jnp.concatenate([_move_tn(raw_ref[b, hd], sel_ref[hd]) for hd in range(N_HEADS)], axis=0)

    x = x_ref[...]
    mod = mod_ref[...]
    h = _rmsnorm_rows(x, gmix_ref[...]) * (1.0 + mod[:, 1:2, :]) + mod[:, 0:1, :]
    z_ref[...] = jnp.dot(h.reshape(rows, D_MODEL).astype(BF16), win_ref[...], preferred_element_type=F32)

    convw = convw_ref[...]
    for b in range(nb):
        xp_ref[b, SUB - (DN_CONV - 1):SUB, :] = convo_ref[b]
        xp_ref[b, SUB:SUB + tb, :] = z_ref[b * tb:(b + 1) * tb, Z_DNQKV:Z_DNQKV + DN_CONV_CH]
        acc = xp_ref[b, SUB - 3:SUB - 3 + tb, :] * convw[0:1, :]
        for j in range(1, DN_CONV):
            acc = acc + xp_ref[b, SUB - 3 + j:SUB - 3 + j + tb, :] * convw[j:j + 1, :]
        qkv_ref[b * tb:(b + 1) * tb, :] = _silu(acc)
        convo_ref[b] = xp_ref[b, SUB + tb - (DN_CONV - 1):SUB + tb, :]

    kc = dict(masks=masks_ref, e_c=ec_ref, e_d=ed_ref, bd256=bd256_ref, bd128=bd128_ref, bdp=bdp_ref,
              st256=st256_ref, st128=st128_ref, ie256=ie256_ref, ie128=ie128_ref, iep=iep_ref,
              ones256=ones256_ref, lmat=lmat_ref)
    sp = sp_ref[...]
    gn = gn_ref[...]
    lbl = lbl_ref[...]
    lbs = lbs_ref[...]
    lbe = jnp.exp(lbl - jnp.max(lbl, axis=0, keepdims=True))
    lb = jnp.sum(lbs * (lbe / jnp.sum(lbe, axis=0, keepdims=True)), axis=0, keepdims=True)
    log_lb = jnp.log(lb)
    log_1mlb = jnp.log1p(-lb)
    neg_a = -jnp.exp(sp[1:2, :])
    chain = nchunk > 1

    def chunk_gens(i, u, boxes):
        if nchunk == 1:
            b = i
        elif nb == 1:
            b = 0
        else:
            b = i // nchunk
        rs = pl.ds(pl.multiple_of(i * c, c), c)
        zc = lambda off, w: z_ref[rs, off:off + w]

        def getter(key, read):
            def get():
                if chain and u > 0:
                    return (yield from _await(boxes, (key, u - 1)))
                return read()
                yield
            return get

        def norm_gate(o, row, gate):
            return _head_norm_gate(o, kc['ones256'][...], gn[row:row + 1, :], gate).astype(BF16)

        def dn():
            small = zc(Z_SMALL, 128)
            sb = small + sp[0:1, :]
            qkv = qkv_ref[rs, :]
            cq, ck, cv = qkv[:, 0:GROUP_W], qkv[:, GROUP_W:2 * GROUP_W], qkv[:, 2 * GROUP_W:3 * GROUP_W]
            ssq = _dot(cq * cq, kc['ones256'][...])
            ssk = _dot(ck * ck, kc['ones256'][...])
            beta_s = _sigmoid(small)
            gam_s = _move_l(kc['lmat'][0:c, :], neg_a * _softplus(sb))
            yield
            dq = cq * lax.rsqrt(ssq + EPS) * (DN_DK ** -0.5)
            dk = ck * lax.rsqrt(ssk + EPS)
            o, st_new = yield from _deltanet_chunk(dq, dk, cv, beta_s, gam_s,
                                                   getter('dn', lambda: sdn_ref[b]), kc, c)
            boxes[('dn', u)] = st_new
            sdn_ref[b] = st_new
            yield
            mix_ref[rs, 0:GROUP_W] = norm_gate(o, 0, _silu(zc(Z_DNG, GROUP_W)))

        def hg():
            zf = zc(Z_HGF, GROUP_W)
            lsz = _log_sigmoid(zf)
            t2 = log_1mlb + lsz
            mx = jnp.maximum(log_lb, t2)
            log_f = mx + jnp.log(jnp.exp(log_lb - mx) + jnp.exp(t2 - mx))
            key_hg = (1.0 - lb) * _sigmoid(-zf)
            o, st_new = yield from _gla_chunk(_silu(zc(Z_HGQ, GROUP_W)), key_hg, zc(Z_HGI, GROUP_W), log_f,
                                              getter('hg', lambda: shg_ref[b]), kc, c, True)
            boxes[('hg', u)] = st_new
            shg_ref[b] = st_new
            yield
            mix_ref[rs, GROUP_W:2 * GROUP_W] = norm_gate(o, 1, _silu(zc(Z_HGG, GROUP_W)))

        def gla():
            small = zc(Z_SMALL, 128)
            g_gla = _log_sigmoid(_dot(small, wup_ref[...]) + bup_ref[...]) * (1.0 / GLA_TAU)
            yield
            o, st_new = yield from _gla_chunk(zc(Z_GLAQ, GLA_W) * (GLA_DK ** -0.5), zc(Z_GLAK, GLA_W),
                                              zc(Z_GLAV, GROUP_W), g_gla,
                                              getter('gla', lambda: sgla_ref[b]), kc, c, False)
            boxes[('gla', u)] = st_new
            sgla_ref[b] = st_new
            yield
            mix_ref[rs, 2 * GROUP_W:3 * GROUP_W] = norm_gate(o, 2, _silu(zc(Z_GLAG, GROUP_W)))

        def ml():
            sb = zc(Z_SMALL, 128) + sp[0:1, :]
            ig_s = pltpu.roll(sb, S_MLF - S_MLI, axis=1)
            lf_s = _log_sigmoid(sb)

            def put_m(m_new):
                boxes[('ml_m', u)] = m_new
                mml_ref[b] = m_new

            hh, c_new, n_new = yield from _mlstm_chunk(
                zc(Z_MLQ, GROUP_W) * (ML_DK ** -0.5), zc(Z_MLK, GROUP_W), zc(Z_MLV, GROUP_W), ig_s, lf_s,
                getter('ml_m', lambda: mml_ref[b]), put_m,
                getter('ml_cn', lambda: (cml_ref[b], nml_ref[b])), kc, c)
            boxes[('ml_cn', u)] = (c_new, n_new)
            cml_ref[b] = c_new
            nml_ref[b] = n_new
            yield
            mix_ref[rs, 3 * GROUP_W:4 * GROUP_W] = norm_gate(hh, 3, _sigmoid(zc(Z_MLO, GROUP_W)))

        return [dn(), hg(), gla(), ml()]

    def chunk_body(it, carry):
        boxes = {}
        gens = []
        for u in range(ilv):
            gens += chunk_gens(it * ilv + u, u, boxes)
        _run_interleaved(gens)
        return carry

    lax.fori_loop(0, nb * nchunk // ilv, chunk_body, 0)

    @pl.when(ti == pl.num_programs(1) - 1)
    def _():
        for _, out_ref, st_ref, sel_ref in mats:
            for b in range(nb):
                for hd in range(N_HEADS):
                    out_ref[b, hd] = _move_nt(sel_ref[hd], st_ref[b, hd * HEAD_V:(hd + 1) * HEAD_V, :])

    out = jnp.dot(mix_ref[...], wout_ref[...], preferred_element_type=F32).reshape(nb, tb, D_MODEL)
    y_ref[...] = x + mod[:, 2:3, :] * out


def _const_tables(c):
    pc = N_HEADS * c
    t = np.arange(c)[:, None]
    lane = np.arange(pc)[None, :]
    hs, s = lane // c, lane % c
    masks = [s <= t, s < t, s == t, (s // 16) == (t // 16)]
    sz = c // 2
    while sz >= 1:
        masks.append(((s // (2 * sz)) == (t // (2 * sz))) & ((t // sz) % 2 == 1) & ((s // sz) % 2 == 0))
        sz //= 2
    masks = np.stack([np.broadcast_to(m, (c, pc)) for m in masks]).astype(np.float32)
    r = np.arange(c)[None, :]
    tri = (r <= t).astype(np.float32)
    lmat = np.concatenate([tri, tri - tri[(np.arange(c) // 4) * 4 + 1]], axis=0)

    def expand(col0, w):
        j = np.arange(128)[:, None]
        l = np.arange(N_HEADS * w)[None, :]
        return (j == col0 + l // w).astype(np.float32)

    e_c = np.stack([expand(S_DNB, c), expand(S_DNA, c), expand(S_MLF, c)])
    e_d = np.stack([expand(S_DNB, HEAD_V), expand(S_DNA, HEAD_V), expand(S_MLF, HEAD_V)])

    def headsel(dk):
        d = np.arange(dk)[None, :, None]
        l = np.arange(N_HEADS * dk)[None, None, :]
        return (l == np.arange(N_HEADS)[:, None, None] * dk + d).astype(np.float32)

    def blk(nr, rg, nl, lg):
        return ((np.arange(nr)[:, None] // rg) == (np.arange(nl)[None, :] // lg)).astype(np.float32)

    tabs = dict(
        masks=jnp.asarray(masks), e_c=jnp.asarray(e_c, BF16), e_d=jnp.asarray(e_d, BF16),
        bd256=jnp.asarray(blk(pc, c, GROUP_W, HEAD_V), BF16), bd128=jnp.asarray(blk(pc, c, GLA_W, GLA_DK), BF16),
        bdp=jnp.asarray(blk(pc, c, pc, c), BF16),
        st256=jnp.asarray(blk(GROUP_W, HEAD_V, GROUP_W, HEAD_V)), st128=jnp.asarray(blk(GROUP_W, HEAD_V, GLA_W, GLA_DK)),
        ie256=jnp.asarray(blk(GROUP_W, HEAD_V, pc, c), BF16), ie128=jnp.asarray(blk(GLA_W, GLA_DK, pc, c), BF16),
        iep=jnp.asarray(blk(pc, c, GROUP_W, HEAD_V), BF16),
        ones256=jnp.asarray(blk(GROUP_W, HEAD_V, GROUP_W, HEAD_V), BF16), lmat=jnp.asarray(lmat, BF16),
        sel64=jnp.asarray(headsel(HEAD_V), BF16), sel32=jnp.asarray(headsel(GLA_DK), BF16))
    order = ['masks', 'e_c', 'e_d', 'bd256', 'bd128', 'bdp', 'st256', 'st128', 'ie256', 'ie128', 'iep', 'ones256',
             'lmat', 'sel64', 'sel32']
    return [tabs[k] for k in order]


def _full_spec(a):
    nd = a.ndim
    return pl.BlockSpec(a.shape, lambda bi, ti, _n=nd: (0,) * _n, pipeline_mode=pl.Buffered(1))


def _mixer_call(x, mod, lw, states, nb, tb):
    bsz, t, _ = x.shape
    c = min(CHUNK, t)
    assert t % tb == 0 and tb % c == 0 and bsz % nb == 0 and c % 16 == 0
    rows = nb * tb
    consts = _const_tables(c)
    params = [lw['g_mix'], lw['w_in'], lw['w_out'], lw['conv_w'], lw['sp'], lw['gn'], lw['lb_logits'], lw['lb_sel'],
              lw['wup'], lw['bup']]
    xspec = pl.BlockSpec((nb, tb, D_MODEL), lambda bi, ti: (bi, ti, 0))

    def bspec(a):
        nd = a.ndim
        return pl.BlockSpec((nb,) + a.shape[1:], lambda bi, ti, _n=nd: (bi,) + (0,) * (_n - 1))

    in_specs = ([xspec, bspec(mod)] + [_full_spec(a) for a in params] + [_full_spec(a) for a in consts]
                + [bspec(s) for s in states])
    out_shape = [jax.ShapeDtypeStruct(x.shape, F32)] + [jax.ShapeDtypeStruct(s.shape, F32) for s in states]
    out_specs = [xspec] + [bspec(s) for s in states]
    scratch = [pltpu.VMEM((rows, N_IN), F32),
               pltpu.VMEM((nb, SUB + tb, DN_CONV_CH), F32),
               pltpu.VMEM((rows, DN_CONV_CH), F32),
               pltpu.VMEM((rows, MIX_W), BF16),
               pltpu.VMEM((nb, GROUP_W, N_HEADS * DN_DK), F32),
               pltpu.VMEM((nb, GROUP_W, N_HEADS * HG_DK), F32),
               pltpu.VMEM((nb, GROUP_W, GLA_W), F32),
               pltpu.VMEM((nb, GROUP_W, N_HEADS * ML_DK), F32)]
    ilv = MAX_INTERLEAVE
    while (nb * (tb // c)) % ilv or (tb // c > 1 and (tb // c) % ilv):
        ilv //= 2
    kern = functools.partial(_mixer_kernel, nb=nb, tb=tb, c=c, ilv=ilv)
    return pl.pallas_call(
        kern, grid=(bsz // nb, t // tb), in_specs=in_specs, out_specs=out_specs, out_shape=out_shape,
        scratch_shapes=scratch, name='mixer',
        compiler_params=pltpu.CompilerParams(dimension_semantics=('arbitrary', 'arbitrary'),
                                             vmem_limit_bytes=VMEM_LIMIT),
    )(x, mod, *params, *consts, *states)


FF_TILE = 256


def _ffn_kernel(x_ref, mod_ref, gffn_ref, wup_ref, wdown_ref, gfin_ref, y_ref, *, nb, tb, final):
    rows = nb * tb
    x = x_ref[...]
    mod = mod_ref[...]
    h = _rmsnorm_rows(x, gffn_ref[...]) * (1.0 + mod[:, 4:5, :]) + mod[:, 3:4, :]
    hb = h.reshape(rows, D_MODEL).astype(BF16)
    acc = jnp.zeros((rows, D_MODEL), F32)
    for j in range(D_FF // FF_TILE):
        gate = jnp.dot(hb, wup_ref[:, j * FF_TILE:(j + 1) * FF_TILE], preferred_element_type=F32)
        up = jnp.dot(hb, wup_ref[:, D_FF + j * FF_TILE:D_FF + (j + 1) * FF_TILE], preferred_element_type=F32)
        act = (_silu(gate) * up).astype(BF16)
        acc = acc + jnp.dot(act, wdown_ref[j * FF_TILE:(j + 1) * FF_TILE, :], preferred_element_type=F32)
    y = x + mod[:, 5:6, :] * acc.reshape(nb, tb, D_MODEL)
    if final:
        y = _rmsnorm_rows(y, gfin_ref[...])
    y_ref[...] = y


def _ffn_call(x, mod, lw, g_final, nb, tb, final):
    bsz, t, _ = x.shape
    xspec = pl.BlockSpec((nb, tb, D_MODEL), lambda bi, ti: (bi, ti, 0))
    mspec = pl.BlockSpec((nb, 6, D_MODEL), lambda bi, ti: (bi, 0, 0))
    params = [lw['g_ffn'], lw['w_up'], lw['w_down'], g_final]
    kern = functools.partial(_ffn_kernel, nb=nb, tb=tb, final=final)
    return pl.pallas_call(
        kern, grid=(bsz // nb, t // tb), in_specs=[xspec, mspec] + [_full_spec(a) for a in params],
        out_specs=xspec, out_shape=jax.ShapeDtypeStruct(x.shape, F32), name='ffn',
        compiler_params=pltpu.CompilerParams(dimension_semantics=('arbitrary', 'arbitrary'),
                                             vmem_limit_bytes=VMEM_LIMIT),
    )(x, mod, *params)


ADA_TILE = 1536


def _ada_kernel(c_ref, w_ref, b_ref, o_ref):
    o_ref[0] = jnp.dot(_silu(c_ref[...]).astype(BF16), w_ref[0], preferred_element_type=F32) + b_ref[0]


def _ada_call(c_all, w_ada, b_ada):
    n = c_all.shape[0]
    nt = 6 * D_MODEL // ADA_TILE
    return pl.pallas_call(
        _ada_kernel, grid=(DEPTH, nt),
        in_specs=[pl.BlockSpec((n, D_MODEL), lambda l, j: (0, 0)),
                  pl.BlockSpec((1, D_MODEL, ADA_TILE), lambda l, j: (l, 0, j)),
                  pl.BlockSpec((1, 1, ADA_TILE), lambda l, j: (l, 0, j))],
        out_specs=pl.BlockSpec((1, n, ADA_TILE), lambda l, j: (l, 0, j)),
        out_shape=jax.ShapeDtypeStruct((DEPTH, n, 6 * D_MODEL), F32), name='ada',
        compiler_params=pltpu.CompilerParams(dimension_semantics=('arbitrary', 'arbitrary')),
    )(c_all, w_ada, b_ada.reshape(DEPTH, 1, 6 * D_MODEL))


def _permute_w_in(w):
    d = w.shape[0]
    zeros = lambda n: jnp.zeros((d, n), w.dtype)
    small = jnp.concatenate([w[:, 768:772], zeros(12), w[:, 2568:2584], w[:, 772:776], zeros(28),
                             w[:, 3608:3612], zeros(28), w[:, 3612:3616], zeros(28)], axis=1)
    return jnp.concatenate([w[:, 0:768], w[:, 776:2568], w[:, 2584:3608], w[:, 3616:3872], small], axis=1)


def _lane_row(pairs, width=128):
    row = jnp.zeros((width,), F32)
    for off, val in pairs:
        row = row.at[off:off + val.shape[0]].set(val.astype(F32))
    return row


def _layer_weights(p, l):
    sp = jnp.zeros((SUB, 128), F32)
    sp = sp.at[0].set(_lane_row([(S_DNA, p['dn_dt_bias'][l]), (S_MLI, p['ml_i_bias'][l]), (S_MLF, p['ml_f_bias'][l])]))
    sp = sp.at[1].set(_lane_row([(S_DNA, p['dn_a_log'][l])]))
    gn = jnp.stack([jnp.tile(p[k][l].astype(F32), N_HEADS) for k in ('dn_norm_g', 'hg_norm_g', 'gla_norm_g', 'ml_norm_g')])
    wup = jnp.zeros((128, GLA_W), F32).at[S_GLAR:S_GLAR + GLA_RANK].set(p['gla_w_up'][l]).astype(BF16)
    lb_sel = (jnp.arange(DEPTH) >= 1) & (jnp.arange(DEPTH) <= l)
    return dict(
        g_mix=p['g_mix'][l].reshape(1, D_MODEL), g_ffn=p['g_ffn'][l].reshape(1, D_MODEL),
        w_in=_permute_w_in(p['w_in'][l]).astype(BF16), w_out=p['w_out'][l].astype(BF16),
        conv_w=p['dn_conv_w'][l], sp=sp, gn=gn, lb_logits=p['hg_lb_logits'].astype(F32),
        lb_sel=lb_sel.astype(F32).reshape(DEPTH, 1), wup=wup, bup=p['gla_b_up'][l].reshape(1, GLA_W).astype(F32),
        w_up=p['w_up'][l].astype(BF16), w_down=p['w_down'][l].astype(BF16))


def _trunk(x, mods, states, lws, g_final, nb, tb):
    new_states = []
    for l in range(DEPTH):
        outs = _mixer_call(x, mods[l], lws[l], states[l], nb, tb)
        x = outs[0]
        new_states.append(outs[1:])
        x = _ffn_call(x, mods[l], lws[l], g_final, *_tiling(x.shape[0], x.shape[1], FFN_ROWS), l == DEPTH - 1)
    return x, new_states


def _pack_states(conv, s_dn, s_hg, s_gla, c_ml, n_ml, m_ml, l):
    b = conv.shape[1]
    m_row = jnp.zeros((b, 1, 128), F32).at[:, 0, S_MLF:S_MLF + N_HEADS].set(m_ml[l].astype(F32))
    return (conv[l].astype(F32), s_dn[l].astype(F32), s_hg[l].astype(F32), s_gla[l].astype(F32),
            c_ml[l].astype(F32), n_ml[l].astype(F32).reshape(b, 1, N_HEADS * ML_DK), m_row)


def _unpack_states(sts):
    conv, s_dn, s_hg, s_gla, c_ml = (jnp.stack([s[i] for s in sts]) for i in range(5))
    n_ml = jnp.stack([s[5].reshape(s[5].shape[0], N_HEADS, ML_DK) for s in sts])
    m_ml = jnp.stack([s[6][:, 0, S_MLF:S_MLF + N_HEADS] for s in sts])
    return conv, s_dn, s_hg, s_gla, c_ml, n_ml, m_ml


def _zero_states(b):
    z = lambda *s: jnp.zeros(s, F32)
    return (z(b, DN_CONV - 1, DN_CONV_CH), z(b, N_HEADS, DN_DK, HEAD_V), z(b, N_HEADS, HG_DK, HEAD_V),
            z(b, N_HEADS, GLA_DK, HEAD_V), z(b, N_HEADS, ML_DK, HEAD_V), z(b, 1, N_HEADS * ML_DK), z(b, 1, 128))


def _tiling(bsz, t, rows):
    tb = min(t, rows)
    nb = max(1, min(bsz, rows // tb))
    while bsz % nb:
        nb -= 1
    return nb, tb


def kernel(x_prompt, x_sample, c_prompt, c_sample, cache_dn_conv, state_dn, state_hgrn, state_gla, state_mlstm_c, state_mlstm_n, state_mlstm_m, w_ada, b_ada, g_mix, g_ffn, w_in, dn_conv_w, dn_a_log, dn_dt_bias, dn_norm_g, hg_lb_logits, hg_norm_g, gla_w_up, gla_b_up, gla_norm_g, ml_i_bias, ml_f_bias, ml_norm_g, w_out, w_up, w_down, g_final):
    p = dict(g_mix=g_mix, g_ffn=g_ffn, w_in=w_in, dn_conv_w=dn_conv_w, dn_a_log=dn_a_log, dn_dt_bias=dn_dt_bias,
             dn_norm_g=dn_norm_g, hg_lb_logits=hg_lb_logits, hg_norm_g=hg_norm_g, gla_w_up=gla_w_up,
             gla_b_up=gla_b_up, gla_norm_g=gla_norm_g, ml_i_bias=ml_i_bias, ml_f_bias=ml_f_bias,
             ml_norm_g=ml_norm_g, w_out=w_out, w_up=w_up, w_down=w_down)
    lws = [_layer_weights(p, l) for l in range(DEPTH)]
    gfin = g_final.reshape(1, D_MODEL).astype(F32)
    bp, bs = x_prompt.shape[0], x_sample.shape[0]
    mod = _ada_call(jnp.concatenate([c_prompt, c_sample], axis=0).astype(F32), w_ada.astype(BF16),
                    b_ada.astype(F32)).reshape(DEPTH, bp + bs, 6, D_MODEL)

    outs = []
    raw = (cache_dn_conv, state_dn, state_hgrn, state_gla, state_mlstm_c, state_mlstm_n, state_mlstm_m)
    for x, lo, hi, states in ((x_prompt, 0, bp, [_zero_states(bp)] * DEPTH),
                              (x_sample, bp, bp + bs, [_pack_states(*raw, l) for l in range(DEPTH)])):
        nb, tb = _tiling(x.shape[0], x.shape[1], MIXER_ROWS)
        y, new = _trunk(x.astype(F32), [mod[l, lo:hi] for l in range(DEPTH)], states, lws, gfin, nb, tb)
        outs.append((y, _unpack_states(new)))
    (y_p, st_p), (y_s, st_s) = outs
    return (y_p, y_s) + tuple(st_p) + tuple(st_s)
```

```python
import functools

import numpy as np
import jax
import jax.numpy as jnp
from jax import lax
from jax.experimental import pallas as pl
from jax.experimental.pallas import tpu as pltpu

F32 = jnp.float32
BF16 = jnp.bfloat16

D_MODEL = 1024
DEPTH = 2
CHUNK = 64
N_HEADS = 4
HEAD_V = 64
GROUP_W = N_HEADS * HEAD_V
MIX_W = 4 * GROUP_W
DN_DK = 64
DN_CONV = 4
DN_CONV_CH = 3 * GROUP_W
HG_DK = 64
GLA_DK = 32
GLA_W = N_HEADS * GLA_DK
GLA_RANK = 16
GLA_TAU = 16.0
ML_DK = 64
D_FF = 2816
EPS = 1e-6

Z_DNQKV = 0
Z_DNG = 768
Z_HGQ = 1024
Z_HGF = 1280
Z_HGI = 1536
Z_HGG = 1792
Z_GLAQ = 2048
Z_GLAK = 2176
Z_GLAV = 2304
Z_GLAG = 2560
Z_MLQ = 2816
Z_MLK = 3072
Z_MLV = 3328
Z_MLO = 3584
Z_SMALL = 3840
N_IN = 3968
S_DNB = 0
S_GLAR = 16
S_DNA = 32
S_MLI = 64
S_MLF = 96

SUB = 8
VMEM_LIMIT = 56 * 1024 * 1024

M_INCL, M_STRICT, M_DIAG, M_SAME16 = 0, 1, 2, 3
M_LEVEL0 = 4
LOG2E = 1.4426950408889634
MIXER_ROWS = 256
FFN_ROWS = 512
MAX_INTERLEAVE = 4


def _sigmoid(x):
    return 1.0 / (1.0 + jnp.exp(-x))


def _silu(x):
    return x * _sigmoid(x)


def _log_sigmoid(x):
    return jnp.minimum(x, 0.0) - jnp.log1p(jnp.exp(-jnp.abs(x)))


def _softplus(x):
    return jnp.maximum(x, 0.0) + jnp.log1p(jnp.exp(-jnp.abs(x)))


def _dot(a, b):
    return jnp.dot(a.astype(BF16), b.astype(BF16), preferred_element_type=F32)


def _dot_nt(a, b):
    return lax.dot_general(a.astype(BF16), b.astype(BF16), (((1,), (1,)), ((), ())),
                           preferred_element_type=F32)


def _split3(x):
    x1 = x.astype(BF16)
    r = x - x1.astype(F32)
    x2 = r.astype(BF16)
    r = r - x2.astype(F32)
    return x1, x2, r.astype(BF16)


def _move_rows(xs, sel):
    parts = [_split3(x) for x in xs]
    y = jnp.dot(jnp.concatenate([p[i] for i in range(3) for p in parts], axis=0), sel, preferred_element_type=F32)
    n = sum(x.shape[0] for x in xs)
    outs = []
    off = 0
    for x in xs:
        r = x.shape[0]
        outs.append(y[off:off + r] + (y[n + off:n + off + r] + y[2 * n + off:2 * n + off + r]))
        off += r
    return outs


def _move_r(x, sel):
    return _move_rows([x], sel)[0]


def _move_l(sel, x):
    x1, x2, x3 = _split3(x)
    d = lambda a: jnp.dot(sel, a, preferred_element_type=F32)
    return d(x1) + (d(x2) + d(x3))


def _move_tn(x, sel):
    x1, x2, x3 = _split3(x)
    d = lambda a: lax.dot_general(a, sel, (((0,), (0,)), ((), ())), preferred_element_type=F32)
    return d(x1) + (d(x2) + d(x3))


def _move_nt(a, b):
    nt = lambda x, y: lax.dot_general(x, y, (((1,), (1,)), ((), ())), preferred_element_type=F32)
    if a.dtype == BF16:
        d = lambda p: nt(a, p)
        x1, x2, x3 = _split3(b)
    else:
        d = lambda p: nt(p, b)
        x1, x2, x3 = _split3(a)
    return d(x1) + (d(x2) + d(x3))


def _bd(x, mask):
    xb = x.astype(BF16)
    return jnp.concatenate([xb] * N_HEADS, axis=0) * mask


def _scan0(x, op, fill):
    n = x.shape[0]
    row = lax.broadcasted_iota(jnp.int32, x.shape, 0)
    sh = 1
    while sh < n:
        r = pltpu.roll(x, sh, axis=0)
        x = op(x, jnp.where(row >= sh, r, fill))
        sh *= 2
    return x


def _rowform(xe, diag):
    return jnp.sum(xe * diag, axis=0, keepdims=True)


def _rmsnorm_rows(x, g):
    return x * lax.rsqrt(jnp.mean(x * x, axis=-1, keepdims=True) + EPS) * g


def _run_interleaved(gens):
    live = list(gens)
    while live:
        alive = []
        for g in live:
            try:
                next(g)
                alive.append(g)
            except StopIteration:
                pass
        live = alive


def _await(boxes, key):
    while key not in boxes:
        yield
    return boxes[key]


def _tri_solve(mm, rhs, masks, bdp, bdr, c):
    assert c // 16 <= 4
    eye = masks[M_DIAG]
    mul = lambda a, b: _dot(a, _bd(b, bdp))
    app = lambda a, r: _dot(a, _bd(r, bdr))
    md = mm * masks[M_SAME16]
    mo = mm - md
    p2 = mul(md, md)
    yield
    d = eye - md
    d = d + mul(d, p2)
    p4 = mul(p2, p2)
    yield
    d = d + mul(d, p4)
    p8 = mul(p4, p4)
    yield
    d = d + mul(d, p8)
    yield
    n = mul(d, mo)
    ys = [app(d, r) for r in rhs]
    yield
    zs = [y - app(n, y) for y in ys]
    if c // 16 <= 2:
        yield
        return zs
    n2 = mul(n, n)
    yield
    ws = [z + app(n2, z) for z in zs]
    yield
    return ws


def _deltanet_chunk(q, k, v, beta_s, gam_s, get_state, kc, c):
    masks = kc['masks']
    gam_e = _move_r(gam_s, kc['e_c'][1])
    beta_e = _move_r(beta_s, kc['e_c'][0])
    if c == HEAD_V:
        gam_d, beta_d = gam_e, beta_e
    else:
        gam_d = _move_r(gam_s, kc['e_d'][1])
        beta_d = _move_r(beta_s, kc['e_d'][0])
    kq = _dot_nt(jnp.concatenate([k, q], axis=0), _bd(k, kc['bd256'][...]))
    kk, qk = kq[0:c], kq[c:2 * c]
    yield
    gam_r = _rowform(gam_e, masks[M_DIAG])
    dec = jnp.exp(jnp.minimum(gam_e - gam_r, 0.0))
    mm = beta_e * kk * dec * masks[M_STRICT]
    eg = jnp.exp(gam_d)
    w, u0 = yield from _tri_solve(mm, [beta_d * eg * k, beta_d * v], masks, kc['bdp'][...], kc['bd256'][...], c)
    gl = gam_d[c - 1:c, :]
    kdec_t = (k * jnp.exp(gl - gam_d)).T
    st = yield from get_state()
    wq = _dot(jnp.concatenate([w, q * eg], axis=0), st)
    u = u0 - wq[0:c]
    qs = wq[c:2 * c]
    yield
    st_new = jnp.exp(gl) * st + kc['st256'][...] * _dot(kdec_t, u)
    o = qs + _dot(qk * dec * masks[M_INCL], _bd(u, kc['bd256'][...]))
    return o, st_new


def _block_ref(b, sz):
    c, w = b.shape
    g3 = b.reshape(c // (2 * sz), 2 * sz, w)
    return jnp.broadcast_to(g3[:, sz - 1:sz, :], g3.shape).reshape(c, w)


def _gla_chunk(q, k, v, g, get_state, kc, c, wide):
    masks = kc['masks']
    bdk = kc['bd256'][...] if wide else kc['bd128'][...]
    ie = kc['ie256'][...] if wide else kc['ie128'][...]
    stm = kc['st256'][...] if wide else kc['st128'][...]
    g2 = g * LOG2E
    v_t = v.T
    bd2 = _move_l(kc['lmat'][...], g2)
    b = bd2[0:c, :]
    yield
    row = lax.broadcasted_iota(jnp.int32, b.shape, 0)
    x1 = jnp.where(jnp.bitwise_and(row, 1) == 1, q * pltpu.roll(k, 1, axis=0) * jnp.exp2(g2), 0.0)
    dd = _dot(jnp.concatenate([q * k, x1], axis=0), ie)
    nlvl = c.bit_length() - 1
    attn = dd[0:c] * masks[M_DIAG] + dd[c:2 * c] * masks[M_LEVEL0 + nlvl - 1]
    lvl = 0
    sz = c // 2
    while sz >= 2:
        d = b - _block_ref(b, sz) if sz >= 4 else bd2[c:2 * c, :]
        e = jnp.exp2(jnp.minimum(d, -d))
        attn = attn + _dot_nt(q * e, _bd(k * e, bdk)) * masks[M_LEVEL0 + lvl]
        yield
        sz //= 2
        lvl += 1
    last = b[c - 1:c, :]
    upd = stm * _dot(v_t, k * jnp.exp2(last - b))
    o = _dot(attn, _bd(v, kc['bd256'][...]))
    yield
    st = yield from get_state()
    st_new = jnp.exp2(last) * st + upd
    o = o + _dot_nt(q * jnp.exp2(b), st)
    return o, st_new


def _mlstm_chunk(q, k, v, ig_s, lf_s, get_m, put_m, get_cn, kc, c):
    masks = kc['masks']
    fcum = _move_l(kc['lmat'][0:c, :], lf_s)
    a = ig_s - fcum
    imax = fcum + _scan0(a, jnp.maximum, -jnp.inf)
    fl = fcum[c - 1:c, :]
    lw = fl - fcum + ig_s
    lw_max = jnp.max(lw, axis=0, keepdims=True)
    kb = _bd(k, kc['bd256'][...])
    qk = _dot_nt(q, kb)
    a_e = _move_r(a, kc['e_c'][2])
    a_r = _rowform(a_e, masks[M_DIAG])
    yield
    m_row = yield from get_m()
    m_new = jnp.maximum(fl + m_row, lw_max)
    put_m(m_new)
    mt = jnp.maximum(fcum + m_row, imax)
    rows = lambda r: jnp.broadcast_to(r, (2 * SUB, 128))
    consts = [rows(fl + m_row - m_new), rows(m_row), rows(fl - m_new)]
    x1 = fcum - mt
    if c == HEAD_V:
        x1_e, lfl, lwc, m_d, sh_d = _move_rows([x1, -mt] + consts, kc['e_d'][2])
        x1_d, a_d = x1_e, a_e
    else:
        x1_e = _move_r(x1, kc['e_c'][2])
        x1_d, lfl, a_d, lwc, m_d, sh_d = _move_rows([x1, -mt, a] + consts, kc['e_d'][2])
    w_inter = jnp.exp(x1_d + m_d[0:1, :])
    floor = jnp.exp(lfl)
    ws = jnp.exp(a_d + sh_d[0:1, :])
    wc = jnp.exp(lwc)[0:1, :]
    wsv_t = (ws * v).T
    yield
    w_intra = jnp.exp(jnp.minimum(x1_e + a_r, 0.0)) * masks[M_INCL] * qk
    num = _dot(w_intra, _bd(v, kc['bd256'][...]))
    den = _dot(w_intra, kc['iep'][...])
    upd = kc['st256'][...] * _dot(wsv_t, k)
    n_upd = jnp.sum(ws * k, axis=0, keepdims=True)
    yield
    ct, n_row = yield from get_cn()
    ct_new = wc * ct + upd
    n_new = wc * n_row + n_upd
    num = num + w_inter * _dot_nt(q, ct)
    den = den + w_inter * _dot(q * n_row, kc['ones256'][...])
    hh = num / jnp.maximum(jnp.abs(den), floor)
    return hh, ct_new, n_new


def _mixer_kernel(x_ref, mod_ref, gmix_ref, win_ref, wout_ref, convw_ref, sp_ref, gn_ref, lbl_ref, lbs_ref,
                  wup_ref, bup_ref,
                  masks_ref, ec_ref, ed_ref, bd256_ref, bd128_ref, bdp_ref, st256_ref, st128_ref,
                  ie256_ref, ie128_ref, iep_ref, ones256_ref, lmat_ref, sel64_ref, sel32_ref,
                  conv0_ref, sdn0_ref, shg0_ref, sgla0_ref, c0_ref, n0_ref, m0_ref,
                  y_ref, convo_ref, sdno_ref, shgo_ref, sglao_ref, cmlo_ref, nml_ref, mml_ref,
                  z_ref, xp_ref, qkv_ref, mix_ref, sdn_ref, shg_ref, sgla_ref, cml_ref, *, nb, tb, c, ilv):
    ti = pl.program_id(1)
    rows = nb * tb
    nchunk = tb // c
    mats = ((sdn0_ref, sdno_ref, sdn_ref, sel64_ref), (shg0_ref, shgo_ref, shg_ref, sel64_ref),
            (sgla0_ref, sglao_ref, sgla_ref, sel32_ref), (c0_ref, cmlo_ref, cml_ref, sel64_ref))

    @pl.when(ti == 0)
    def _():
        convo_ref[...] = conv0_ref[...]
        nml_ref[...] = n0_ref[...]
        mml_ref[...] = m0_ref[...]
        for raw_ref, _, st_ref, sel_ref in mats:
            place = _move_r if st_ref is sdn_ref else _move_tn
            for b in range(nb):
                st_ref[b] = jnp.concatenate([place(raw_ref[b, hd], sel_ref[hd]) for hd in range(N_HEADS)], axis=0)

    x = x_ref[...]
    mod = mod_ref[...]
    h = _rmsnorm_rows(x, gmix_ref[...]) * (1.0 + mod[:, 1:2, :]) + mod[:, 0:1, :]
    z_ref[...] = jnp.dot(h.reshape(rows, D_MODEL).astype(BF16), win_ref[...], preferred_element_type=F32)

    convw = convw_ref[...]
    for b in range(nb):
        xp_ref[b, SUB - (DN_CONV - 1):SUB, :] = convo_ref[b]
        xp_ref[b, SUB:SUB + tb, :] = z_ref[b * tb:(b + 1) * tb, Z_DNQKV:Z_DNQKV + DN_CONV_CH]
        acc = xp_ref[b, SUB - 3:SUB - 3 + tb, :] * convw[0:1, :]
        for j in range(1, DN_CONV):
            acc = acc + xp_ref[b, SUB - 3 + j:SUB - 3 + j + tb, :] * convw[j:j + 1, :]
        qkv_ref[b * tb:(b + 1) * tb, :] = _silu(acc)
        convo_ref[b] = xp_ref[b, SUB + tb - (DN_CONV - 1):SUB + tb, :]

    kc = dict(masks=masks_ref, e_c=ec_ref, e_d=ed_ref, bd256=bd256_ref, bd128=bd128_ref, bdp=bdp_ref,
              st256=st256_ref, st128=st128_ref, ie256=ie256_ref, ie128=ie128_ref, iep=iep_ref,
              ones256=ones256_ref, lmat=lmat_ref)
    sp = sp_ref[...]
    gn = gn_ref[...]
    lbl = lbl_ref[...]
    lbs = lbs_ref[...]
    lbe = jnp.exp(lbl - jnp.max(lbl, axis=0, keepdims=True))
    lb = jnp.sum(lbs * (lbe / jnp.sum(lbe, axis=0, keepdims=True)), axis=0, keepdims=True)
    log_lb = jnp.log(lb)
    log_1mlb = jnp.log1p(-lb)
    neg_a = -jnp.exp(sp[1:2, :])
    chain = nchunk > 1

    def chunk_gens(i, u, boxes):
        if nchunk == 1:
            b = i
        elif nb == 1:
            b = 0
        else:
            b = i // nchunk
        rs = pl.ds(pl.multiple_of(i * c, c), c)
        zc = lambda off, w: z_ref[rs, off:off + w]

        def getter(key, read):
            def get():
                if chain and u > 0:
                    return (yield from _await(boxes, (key, u - 1)))
                return read()
                yield
            return get

        def dn():
            small = zc(Z_SMALL, 128)
            sb = small + sp[0:1, :]
            qkv = qkv_ref[rs, :]
            cq, ck, cv = qkv[:, 0:GROUP_W], qkv[:, GROUP_W:2 * GROUP_W], qkv[:, 2 * GROUP_W:3 * GROUP_W]
            ss = _dot(jnp.concatenate([cq * cq, ck * ck], axis=0), kc['ones256'][...])
            ssq, ssk = ss[0:c], ss[c:2 * c]
            beta_s = _sigmoid(small)
            gam_s = _move_l(kc['lmat'][0:c, :], neg_a * _softplus(sb))
            yield
            dq = cq * lax.rsqrt(ssq + EPS) * (DN_DK ** -0.5)
            dk = ck * lax.rsqrt(ssk + EPS)
            o, st_new = yield from _deltanet_chunk(dq, dk, cv, beta_s, gam_s,
                                                   getter('dn', lambda: sdn_ref[b]), kc, c)
            boxes[('dn', u)] = st_new
            sdn_ref[b] = st_new
            boxes[('out', u, 0)] = o

        def hg():
            zf = zc(Z_HGF, GROUP_W)
            lsz = _log_sigmoid(zf)
            t2 = log_1mlb + lsz
            mx = jnp.maximum(log_lb, t2)
            log_f = mx + jnp.log(jnp.exp(log_lb - mx) + jnp.exp(t2 - mx))
            key_hg = (1.0 - lb) * _sigmoid(-zf)
            o, st_new = yield from _gla_chunk(_silu(zc(Z_HGQ, GROUP_W)), key_hg, zc(Z_HGI, GROUP_W), log_f,
                                              getter('hg', lambda: shg_ref[b]), kc, c, True)
            boxes[('hg', u)] = st_new
            shg_ref[b] = st_new
            boxes[('out', u, 1)] = o

        def gla():
            small = zc(Z_SMALL, 128)
            g_gla = _log_sigmoid(_dot(small, wup_ref[...]) + bup_ref[...]) * (1.0 / GLA_TAU)
            yield
            o, st_new = yield from _gla_chunk(zc(Z_GLAQ, GLA_W) * (GLA_DK ** -0.5), zc(Z_GLAK, GLA_W),
                                              zc(Z_GLAV, GROUP_W), g_gla,
                                              getter('gla', lambda: sgla_ref[b]), kc, c, False)
            boxes[('gla', u)] = st_new
            sgla_ref[b] = st_new
            boxes[('out', u, 2)] = o

        def ml():
            sb = zc(Z_SMALL, 128) + sp[0:1, :]
            ig_s = pltpu.roll(sb, S_MLF - S_MLI, axis=1)
            lf_s = _log_sigmoid(sb)

            def put_m(m_new):
                boxes[('ml_m', u)] = m_new
                mml_ref[b] = m_new

            hh, c_new, n_new = yield from _mlstm_chunk(
                zc(Z_MLQ, GROUP_W) * (ML_DK ** -0.5), zc(Z_MLK, GROUP_W), zc(Z_MLV, GROUP_W), ig_s, lf_s,
                getter('ml_m', lambda: mml_ref[b]), put_m,
                getter('ml_cn', lambda: (cml_ref[b], nml_ref[b])), kc, c)
            boxes[('ml_cn', u)] = (c_new, n_new)
            cml_ref[b] = c_new
            nml_ref[b] = n_new
            boxes[('out', u, 3)] = hh

        return [dn(), hg(), gla(), ml()]

    def chunk_body(it, carry):
        boxes = {}
        gens = []
        for u in range(ilv):
            gens += chunk_gens(it * ilv + u, u, boxes)
        _run_interleaved(gens)
        outs = jnp.concatenate([boxes[('out', u, m)] for u in range(ilv) for m in range(4)], axis=0)
        ms = _dot(outs * outs, kc['ones256'][...]) * (1.0 / HEAD_V)
        normed = outs * lax.rsqrt(ms + EPS)
        gate_cols = (Z_DNG, Z_HGG, Z_GLAG, Z_MLO)
        for u in range(ilv):
            rs = pl.ds(pl.multiple_of((it * ilv + u) * c, c), c)
            for m in range(4):
                zg = z_ref[rs, gate_cols[m]:gate_cols[m] + GROUP_W]
                gate = _sigmoid(zg) if m == 3 else _silu(zg)
                r0 = (u * 4 + m) * c
                mix_ref[rs, m * GROUP_W:(m + 1) * GROUP_W] = (normed[r0:r0 + c] * gn[m:m + 1, :] * gate).astype(BF16)
        return carry

    lax.fori_loop(0, nb * nchunk // ilv, chunk_body, 0)

    @pl.when(ti == pl.num_programs(1) - 1)
    def _():
        for _, out_ref, st_ref, sel_ref in mats:
            for b in range(nb):
                for hd in range(N_HEADS):
                    blk = st_ref[b, hd * HEAD_V:(hd + 1) * HEAD_V, :]
                    out_ref[b, hd] = _move_nt(blk, sel_ref[hd]) if st_ref is sdn_ref else _move_nt(sel_ref[hd], blk)

    out = jnp.dot(mix_ref[...], wout_ref[...], preferred_element_type=F32).reshape(nb, tb, D_MODEL)
    y_ref[...] = x + mod[:, 2:3, :] * out


def _const_tables(c):
    pc = N_HEADS * c
    t = np.arange(c)[:, None]
    lane = np.arange(pc)[None, :]
    hs, s = lane // c, lane % c
    masks = [s <= t, s < t, s == t, (s // 16) == (t // 16)]
    sz = c // 2
    while sz >= 1:
        masks.append(((s // (2 * sz)) == (t // (2 * sz))) & ((t // sz) % 2 == 1) & ((s // sz) % 2 == 0))
        sz //= 2
    masks = np.stack([np.broadcast_to(m, (c, pc)) for m in masks]).astype(np.float32)
    r = np.arange(c)[None, :]
    tri = (r <= t).astype(np.float32)
    lmat = np.concatenate([tri, tri - tri[(np.arange(c) // 4) * 4 + 1]], axis=0)

    def expand(col0, w):
        j = np.arange(128)[:, None]
        l = np.arange(N_HEADS * w)[None, :]
        return (j == col0 + l // w).astype(np.float32)

    e_c = np.stack([expand(S_DNB, c), expand(S_DNA, c), expand(S_MLF, c)])
    e_d = np.stack([expand(S_DNB, HEAD_V), expand(S_DNA, HEAD_V), expand(S_MLF, HEAD_V)])

    def headsel(dk):
        d = np.arange(dk)[None, :, None]
        l = np.arange(N_HEADS * dk)[None, None, :]
        return (l == np.arange(N_HEADS)[:, None, None] * dk + d).astype(np.float32)

    def blk(nr, rg, nl, lg):
        return ((np.arange(nr)[:, None] // rg) == (np.arange(nl)[None, :] // lg)).astype(np.float32)

    tabs = dict(
        masks=jnp.asarray(masks), e_c=jnp.asarray(e_c, BF16), e_d=jnp.asarray(e_d, BF16),
        bd256=jnp.asarray(blk(pc, c, GROUP_W, HEAD_V), BF16), bd128=jnp.asarray(blk(pc, c, GLA_W, GLA_DK), BF16),
        bdp=jnp.asarray(blk(pc, c, pc, c), BF16),
        st256=jnp.asarray(blk(GROUP_W, HEAD_V, GROUP_W, HEAD_V)), st128=jnp.asarray(blk(GROUP_W, HEAD_V, GLA_W, GLA_DK)),
        ie256=jnp.asarray(blk(GROUP_W, HEAD_V, pc, c), BF16), ie128=jnp.asarray(blk(GLA_W, GLA_DK, pc, c), BF16),
        iep=jnp.asarray(blk(pc, c, GROUP_W, HEAD_V), BF16),
        ones256=jnp.asarray(blk(GROUP_W, HEAD_V, GROUP_W, HEAD_V), BF16), lmat=jnp.asarray(lmat, BF16),
        sel64=jnp.asarray(headsel(HEAD_V), BF16), sel32=jnp.asarray(headsel(GLA_DK), BF16))
    order = ['masks', 'e_c', 'e_d', 'bd256', 'bd128', 'bdp', 'st256', 'st128', 'ie256', 'ie128', 'iep', 'ones256',
             'lmat', 'sel64', 'sel32']
    return [tabs[k] for k in order]


def _full_spec(a):
    nd = a.ndim
    return pl.BlockSpec(a.shape, lambda bi, ti, _n=nd: (0,) * _n, pipeline_mode=pl.Buffered(1))


def _mixer_call(x, mod, lw, states, nb, tb):
    bsz, t, _ = x.shape
    c = min(CHUNK, t)
    assert t % tb == 0 and tb % c == 0 and bsz % nb == 0 and c % 16 == 0
    rows = nb * tb
    consts = _const_tables(c)
    params = [lw['g_mix'], lw['w_in'], lw['w_out'], lw['conv_w'], lw['sp'], lw['gn'], lw['lb_logits'], lw['lb_sel'],
              lw['wup'], lw['bup']]
    xspec = pl.BlockSpec((nb, tb, D_MODEL), lambda bi, ti: (bi, ti, 0))

    def bspec(a):
        nd = a.ndim
        return pl.BlockSpec((nb,) + a.shape[1:], lambda bi, ti, _n=nd: (bi,) + (0,) * (_n - 1))

    in_specs = ([xspec, bspec(mod)] + [_full_spec(a) for a in params] + [_full_spec(a) for a in consts]
                + [bspec(s) for s in states])
    out_shape = [jax.ShapeDtypeStruct(x.shape, F32)] + [jax.ShapeDtypeStruct(s.shape, F32) for s in states]
    out_specs = [xspec] + [bspec(s) for s in states]
    scratch = [pltpu.VMEM((rows, N_IN), F32),
               pltpu.VMEM((nb, SUB + tb, DN_CONV_CH), F32),
               pltpu.VMEM((rows, DN_CONV_CH), F32),
               pltpu.VMEM((rows, MIX_W), BF16),
               pltpu.VMEM((nb, GROUP_W, N_HEADS * DN_DK), F32),
               pltpu.VMEM((nb, GROUP_W, N_HEADS * HG_DK), F32),
               pltpu.VMEM((nb, GROUP_W, GLA_W), F32),
               pltpu.VMEM((nb, GROUP_W, N_HEADS * ML_DK), F32)]
    ilv = MAX_INTERLEAVE
    while (nb * (tb // c)) % ilv or (tb // c > 1 and (tb // c) % ilv):
        ilv //= 2
    kern = functools.partial(_mixer_kernel, nb=nb, tb=tb, c=c, ilv=ilv)
    return pl.pallas_call(
        kern, grid=(bsz // nb, t // tb), in_specs=in_specs, out_specs=out_specs, out_shape=out_shape,
        scratch_shapes=scratch, name='mixer',
        compiler_params=pltpu.CompilerParams(dimension_semantics=('arbitrary', 'arbitrary'),
                                             vmem_limit_bytes=VMEM_LIMIT),
    )(x, mod, *params, *consts, *states)


FF_TILE = 256


def _ffn_kernel(x_ref, mod_ref, gffn_ref, wup_ref, wdown_ref, gfin_ref, y_ref, *, nb, tb, final):
    rows = nb * tb
    x = x_ref[...]
    mod = mod_ref[...]
    h = _rmsnorm_rows(x, gffn_ref[...]) * (1.0 + mod[:, 4:5, :]) + mod[:, 3:4, :]
    hb = h.reshape(rows, D_MODEL).astype(BF16)
    acc = jnp.zeros((rows, D_MODEL), F32)
    for j in range(D_FF // FF_TILE):
        gate = jnp.dot(hb, wup_ref[:, j * FF_TILE:(j + 1) * FF_TILE], preferred_element_type=F32)
        up = jnp.dot(hb, wup_ref[:, D_FF + j * FF_TILE:D_FF + (j + 1) * FF_TILE], preferred_element_type=F32)
        act = (_silu(gate) * up).astype(BF16)
        acc = acc + jnp.dot(act, wdown_ref[j * FF_TILE:(j + 1) * FF_TILE, :], preferred_element_type=F32)
    y = x + mod[:, 5:6, :] * acc.reshape(nb, tb, D_MODEL)
    if final:
        y = _rmsnorm_rows(y, gfin_ref[...])
    y_ref[...] = y


def _ffn_call(x, mod, lw, g_final, nb, tb, final):
    bsz, t, _ = x.shape
    xspec = pl.BlockSpec((nb, tb, D_MODEL), lambda bi, ti: (bi, ti, 0))
    mspec = pl.BlockSpec((nb, 6, D_MODEL), lambda bi, ti: (bi, 0, 0))
    params = [lw['g_ffn'], lw['w_up'], lw['w_down'], g_final]
    kern = functools.partial(_ffn_kernel, nb=nb, tb=tb, final=final)
    return pl.pallas_call(
        kern, grid=(bsz // nb, t // tb), in_specs=[xspec, mspec] + [_full_spec(a) for a in params],
        out_specs=xspec, out_shape=jax.ShapeDtypeStruct(x.shape, F32), name='ffn',
        compiler_params=pltpu.CompilerParams(dimension_semantics=('arbitrary', 'arbitrary'),
                                             vmem_limit_bytes=VMEM_LIMIT),
    )(x, mod, *params)


ADA_TILE = 1536


def _ada_kernel(c_ref, w_ref, b_ref, o_ref):
    o_ref[0] = jnp.dot(_silu(c_ref[...]).astype(BF16), w_ref[0], preferred_element_type=F32) + b_ref[0]


def _ada_call(c_all, w_ada, b_ada):
    n = c_all.shape[0]
    nt = 6 * D_MODEL // ADA_TILE
    return pl.pallas_call(
        _ada_kernel, grid=(DEPTH, nt),
        in_specs=[pl.BlockSpec((n, D_MODEL), lambda l, j: (0, 0)),
                  pl.BlockSpec((1, D_MODEL, ADA_TILE), lambda l, j: (l, 0, j)),
                  pl.BlockSpec((1, 1, ADA_TILE), lambda l, j: (l, 0, j))],
        out_specs=pl.BlockSpec((1, n, ADA_TILE), lambda l, j: (l, 0, j)),
        out_shape=jax.ShapeDtypeStruct((DEPTH, n, 6 * D_MODEL), F32), name='ada',
        compiler_params=pltpu.CompilerParams(dimension_semantics=('arbitrary', 'arbitrary')),
    )(c_all, w_ada, b_ada.reshape(DEPTH, 1, 6 * D_MODEL))


def _permute_w_in(w):
    d = w.shape[0]
    zeros = lambda n: jnp.zeros((d, n), w.dtype)
    small = jnp.concatenate([w[:, 768:772], zeros(12), w[:, 2568:2584], w[:, 772:776], zeros(28),
                             w[:, 3608:3612], zeros(28), w[:, 3612:3616], zeros(28)], axis=1)
    return jnp.concatenate([w[:, 0:768], w[:, 776:2568], w[:, 2584:3608], w[:, 3616:3872], small], axis=1)


def _lane_row(pairs, width=128):
    row = jnp.zeros((width,), F32)
    for off, val in pairs:
        row = row.at[off:off + val.shape[0]].set(val.astype(F32))
    return row


def _layer_weights(p, l):
    sp = jnp.zeros((SUB, 128), F32)
    sp = sp.at[0].set(_lane_row([(S_DNA, p['dn_dt_bias'][l]), (S_MLI, p['ml_i_bias'][l]), (S_MLF, p['ml_f_bias'][l])]))
    sp = sp.at[1].set(_lane_row([(S_DNA, p['dn_a_log'][l])]))
    gn = jnp.stack([jnp.tile(p[k][l].astype(F32), N_HEADS) for k in ('dn_norm_g', 'hg_norm_g', 'gla_norm_g', 'ml_norm_g')])
    wup = jnp.zeros((128, GLA_W), F32).at[S_GLAR:S_GLAR + GLA_RANK].set(p['gla_w_up'][l]).astype(BF16)
    lb_sel = (jnp.arange(DEPTH) >= 1) & (jnp.arange(DEPTH) <= l)
    return dict(
        g_mix=p['g_mix'][l].reshape(1, D_MODEL), g_ffn=p['g_ffn'][l].reshape(1, D_MODEL),
        w_in=_permute_w_in(p['w_in'][l]).astype(BF16), w_out=p['w_out'][l].astype(BF16),
        conv_w=p['dn_conv_w'][l], sp=sp, gn=gn, lb_logits=p['hg_lb_logits'].astype(F32),
        lb_sel=lb_sel.astype(F32).reshape(DEPTH, 1), wup=wup, bup=p['gla_b_up'][l].reshape(1, GLA_W).astype(F32),
        w_up=p['w_up'][l].astype(BF16), w_down=p['w_down'][l].astype(BF16))


def _trunk(x, mods, states, lws, g_final, nb, tb):
    new_states = []
    for l in range(DEPTH):
        outs = _mixer_call(x, mods[l], lws[l], states[l], nb, tb)
        x = outs[0]
        new_states.append(outs[1:])
        x = _ffn_call(x, mods[l], lws[l], g_final, *_tiling(x.shape[0], x.shape[1], FFN_ROWS), l == DEPTH - 1)
    return x, new_states


def _pack_states(conv, s_dn, s_hg, s_gla, c_ml, n_ml, m_ml, l):
    b = conv.shape[1]
    m_row = jnp.zeros((b, 1, 128), F32).at[:, 0, S_MLF:S_MLF + N_HEADS].set(m_ml[l].astype(F32))
    return (conv[l].astype(F32), s_dn[l].astype(F32), s_hg[l].astype(F32), s_gla[l].astype(F32),
            c_ml[l].astype(F32), n_ml[l].astype(F32).reshape(b, 1, N_HEADS * ML_DK), m_row)


def _unpack_states(sts):
    conv, s_dn, s_hg, s_gla, c_ml = (jnp.stack([s[i] for s in sts]) for i in range(5))
    n_ml = jnp.stack([s[5].reshape(s[5].shape[0], N_HEADS, ML_DK) for s in sts])
    m_ml = jnp.stack([s[6][:, 0, S_MLF:S_MLF + N_HEADS] for s in sts])
    return conv, s_dn, s_hg, s_gla, c_ml, n_ml, m_ml


def _zero_states(b):
    z = lambda *s: jnp.zeros(s, F32)
    return (z(b, DN_CONV - 1, DN_CONV_CH), z(b, N_HEADS, DN_DK, HEAD_V), z(b, N_HEADS, HG_DK, HEAD_V),
            z(b, N_HEADS, GLA_DK, HEAD_V), z(b, N_HEADS, ML_DK, HEAD_V), z(b, 1, N_HEADS * ML_DK), z(b, 1, 128))


def _tiling(bsz, t, rows):
    tb = min(t, rows)
    nb = max(1, min(bsz, rows // tb))
    while bsz % nb:
        nb -= 1
    return nb, tb


def kernel(x_prompt, x_sample, c_prompt, c_sample, cache_dn_conv, state_dn, state_hgrn, state_gla, state_mlstm_c, state_mlstm_n, state_mlstm_m, w_ada, b_ada, g_mix, g_ffn, w_in, dn_conv_w, dn_a_log, dn_dt_bias, dn_norm_g, hg_lb_logits, hg_norm_g, gla_w_up, gla_b_up, gla_norm_g, ml_i_bias, ml_f_bias, ml_norm_g, w_out, w_up, w_down, g_final):
    p = dict(g_mix=g_mix, g_ffn=g_ffn, w_in=w_in, dn_conv_w=dn_conv_w, dn_a_log=dn_a_log, dn_dt_bias=dn_dt_bias,
             dn_norm_g=dn_norm_g, hg_lb_logits=hg_lb_logits, hg_norm_g=hg_norm_g, gla_w_up=gla_w_up,
             gla_b_up=gla_b_up, gla_norm_g=gla_norm_g, ml_i_bias=ml_i_bias, ml_f_bias=ml_f_bias,
             ml_norm_g=ml_norm_g, w_out=w_out, w_up=w_up, w_down=w_down)
    lws = [_layer_weights(p, l) for l in range(DEPTH)]
    gfin = g_final.reshape(1, D_MODEL).astype(F32)
    bp, bs = x_prompt.shape[0], x_sample.shape[0]
    mod = _ada_call(jnp.concatenate([c_prompt, c_sample], axis=0).astype(F32), w_ada.astype(BF16),
                    b_ada.astype(F32)).reshape(DEPTH, bp + bs, 6, D_MODEL)

    outs = []
    raw = (cache_dn_conv, state_dn, state_hgrn, state_gla, state_mlstm_c, state_mlstm_n, state_mlstm_m)
    for x, lo, hi, states in ((x_prompt, 0, bp, [_zero_states(bp)] * DEPTH),
                              (x_sample, bp, bp + bs, [_pack_states(*raw, l) for l in range(DEPTH)])):
        nb, tb = _tiling(x.shape[0], x.shape[1], MIXER_ROWS)
        y, new = _trunk(x.astype(F32), [mod[l, lo:hi] for l in range(DEPTH)], states, lws, gfin, nb, tb)
        outs.append((y, _unpack_states(new)))
    (y_p, st_p), (y_s, st_s) = outs
    return (y_p, y_s) + tuple(st_p) + tuple(st_s)
```

```python
import functools

import numpy as np
import jax
import jax.numpy as jnp
from jax import lax
from jax.experimental import pallas as pl
from jax.experimental.pallas import tpu as pltpu

F32 = jnp.float32
BF16 = jnp.bfloat16

D_MODEL = 1024
DEPTH = 2
CHUNK = 64
N_HEADS = 4
HEAD_V = 64
GROUP_W = N_HEADS * HEAD_V
MIX_W = 4 * GROUP_W
DN_DK = 64
DN_CONV = 4
DN_CONV_CH = 3 * GROUP_W
HG_DK = 64
GLA_DK = 32
GLA_W = N_HEADS * GLA_DK
GLA_RANK = 16
GLA_TAU = 16.0
ML_DK = 64
D_FF = 2816
EPS = 1e-6

Z_DNQKV = 0
Z_DNG = 768
Z_HGQ = 1024
Z_HGF = 1280
Z_HGI = 1536
Z_HGG = 1792
Z_GLAQ = 2048
Z_GLAK = 2176
Z_GLAV = 2304
Z_GLAG = 2560
Z_MLQ = 2816
Z_MLK = 3072
Z_MLV = 3328
Z_MLO = 3584
Z_SMALL = 3840
N_IN = 3968
S_DNB = 0
S_GLAR = 16
S_DNA = 32
S_MLI = 64
S_MLF = 96

SUB = 8
VMEM_LIMIT = 56 * 1024 * 1024

M_INCL, M_STRICT, M_DIAG, M_SAME16 = 0, 1, 2, 3
M_LEVEL0 = 4
LOG2E = 1.4426950408889634
MIXER_ROWS = 256
FFN_ROWS = 512
STAGGER = 3
MAX_INTERLEAVE = 4


def _sigmoid(x):
    return 1.0 / (1.0 + jnp.exp(-x))


def _silu(x):
    return x * _sigmoid(x)


def _log_sigmoid(x):
    return jnp.minimum(x, 0.0) - jnp.log1p(jnp.exp(-jnp.abs(x)))


def _softplus(x):
    return jnp.maximum(x, 0.0) + jnp.log1p(jnp.exp(-jnp.abs(x)))


def _dot(a, b):
    return jnp.dot(a.astype(BF16), b.astype(BF16), preferred_element_type=F32)


def _dot_nt(a, b):
    return lax.dot_general(a.astype(BF16), b.astype(BF16), (((1,), (1,)), ((), ())),
                           preferred_element_type=F32)


def _split3(x):
    x1 = x.astype(BF16)
    r = x - x1.astype(F32)
    x2 = r.astype(BF16)
    r = r - x2.astype(F32)
    return x1, x2, r.astype(BF16)


def _move_rows(xs, sel):
    parts = [_split3(x) for x in xs]
    y = jnp.dot(jnp.concatenate([p[i] for i in range(3) for p in parts], axis=0), sel, preferred_element_type=F32)
    n = sum(x.shape[0] for x in xs)
    outs = []
    off = 0
    for x in xs:
        r = x.shape[0]
        outs.append(y[off:off + r] + (y[n + off:n + off + r] + y[2 * n + off:2 * n + off + r]))
        off += r
    return outs


def _move_r(x, sel):
    return _move_rows([x], sel)[0]


def _move_l(sel, x):
    x1, x2, x3 = _split3(x)
    d = lambda a: jnp.dot(sel, a, preferred_element_type=F32)
    return d(x1) + (d(x2) + d(x3))


def _move_tn(x, sel):
    x1, x2, x3 = _split3(x)
    d = lambda a: lax.dot_general(a, sel, (((0,), (0,)), ((), ())), preferred_element_type=F32)
    return d(x1) + (d(x2) + d(x3))


def _move_nt(a, b):
    nt = lambda x, y: lax.dot_general(x, y, (((1,), (1,)), ((), ())), preferred_element_type=F32)
    if a.dtype == BF16:
        d = lambda p: nt(a, p)
        x1, x2, x3 = _split3(b)
    else:
        d = lambda p: nt(p, b)
        x1, x2, x3 = _split3(a)
    return d(x1) + (d(x2) + d(x3))


def _bd(x, mask):
    xb = x.astype(BF16)
    return jnp.concatenate([xb] * N_HEADS, axis=0) * mask


def _scan0(x, op, fill):
    n = x.shape[0]
    row = lax.broadcasted_iota(jnp.int32, x.shape, 0)
    sh = 1
    while sh < n:
        r = pltpu.roll(x, sh, axis=0)
        x = op(x, jnp.where(row >= sh, r, fill))
        sh *= 2
    return x


def _rowform(xe, diag):
    return jnp.sum(xe * diag, axis=0, keepdims=True)


def _rmsnorm_rows(x, g):
    return x * lax.rsqrt(jnp.mean(x * x, axis=-1, keepdims=True) + EPS) * g


def _run_interleaved(gens):
    live = list(gens)
    rnd = 0
    while live:
        alive = []
        for start, g in live:
            if rnd >= start:
                try:
                    next(g)
                except StopIteration:
                    continue
            alive.append((start, g))
        live = alive
        rnd += 1


def _await(boxes, key):
    while key not in boxes:
        yield
    return boxes[key]


def _tri_solve(mm, rhs, masks, bdp, bdr, c):
    assert c // 16 <= 4
    eye = masks[M_DIAG]
    mul = lambda a, b: _dot(a, _bd(b, bdp))
    app = lambda a, r: _dot(a, _bd(r, bdr))
    md = mm * masks[M_SAME16]
    mo = mm - md
    p2 = mul(md, md)
    yield
    d = eye - md
    d = d + mul(d, p2)
    p4 = mul(p2, p2)
    yield
    d = d + mul(d, p4)
    p8 = mul(p4, p4)
    yield
    d = d + mul(d, p8)
    yield
    n = mul(d, mo)
    ys = [app(d, r) for r in rhs]
    yield
    zs = [y - app(n, y) for y in ys]
    if c // 16 <= 2:
        yield
        return zs
    n2 = mul(n, n)
    yield
    ws = [z + app(n2, z) for z in zs]
    yield
    return ws


def _deltanet_chunk(q, k, v, beta_s, gam_s, get_state, kc, c):
    masks = kc['masks']
    gam_e = _move_r(gam_s, kc['e_c'][1])
    beta_e = _move_r(beta_s, kc['e_c'][0])
    if c == HEAD_V:
        gam_d, beta_d = gam_e, beta_e
    else:
        gam_d = _move_r(gam_s, kc['e_d'][1])
        beta_d = _move_r(beta_s, kc['e_d'][0])
    kq = _dot_nt(jnp.concatenate([k, q], axis=0), _bd(k, kc['bd256'][...]))
    kk, qk = kq[0:c], kq[c:2 * c]
    yield
    gam_r = _rowform(gam_e, masks[M_DIAG])
    dec = jnp.exp(jnp.minimum(gam_e - gam_r, 0.0))
    mm = beta_e * kk * dec * masks[M_STRICT]
    eg = jnp.exp(gam_d)
    w, u0 = yield from _tri_solve(mm, [beta_d * eg * k, beta_d * v], masks, kc['bdp'][...], kc['bd256'][...], c)
    gl = gam_d[c - 1:c, :]
    kdec_t = (k * jnp.exp(gl - gam_d)).T
    st = yield from get_state()
    wq = _dot(jnp.concatenate([w, q * eg], axis=0), st)
    u = u0 - wq[0:c]
    qs = wq[c:2 * c]
    yield
    st_new = jnp.exp(gl) * st + kc['st256'][...] * _dot(kdec_t, u)
    o = qs + _dot(qk * dec * masks[M_INCL], _bd(u, kc['bd256'][...]))
    return o, st_new


def _block_ref(b, sz):
    c, w = b.shape
    g3 = b.reshape(c // (2 * sz), 2 * sz, w)
    return jnp.broadcast_to(g3[:, sz - 1:sz, :], g3.shape).reshape(c, w)


def _gla_chunk(q, k, v, g, get_state, kc, c, wide):
    masks = kc['masks']
    bdk = kc['bd256'][...] if wide else kc['bd128'][...]
    ie = kc['ie256'][...] if wide else kc['ie128'][...]
    stm = kc['st256'][...] if wide else kc['st128'][...]
    g2 = g * LOG2E
    v_t = v.T
    bd2 = _move_l(kc['lmat'][...], g2)
    b = bd2[0:c, :]
    yield
    row = lax.broadcasted_iota(jnp.int32, b.shape, 0)
    x1 = jnp.where(jnp.bitwise_and(row, 1) == 1, q * pltpu.roll(k, 1, axis=0) * jnp.exp2(g2), 0.0)
    dd = _dot(jnp.concatenate([q * k, x1], axis=0), ie)
    nlvl = c.bit_length() - 1
    attn = dd[0:c] * masks[M_DIAG] + dd[c:2 * c] * masks[M_LEVEL0 + nlvl - 1]
    lvl = 0
    sz = c // 2
    while sz >= 2:
        d = b - _block_ref(b, sz) if sz >= 4 else bd2[c:2 * c, :]
        e = jnp.exp2(jnp.minimum(d, -d))
        attn = attn + _dot_nt(q * e, _bd(k * e, bdk)) * masks[M_LEVEL0 + lvl]
        yield
        sz //= 2
        lvl += 1
    last = b[c - 1:c, :]
    upd = stm * _dot(v_t, k * jnp.exp2(last - b))
    o = _dot(attn, _bd(v, kc['bd256'][...]))
    yield
    st = yield from get_state()
    st_new = jnp.exp2(last) * st + upd
    o = o + _dot_nt(q * jnp.exp2(b), st)
    return o, st_new


def _mlstm_chunk(q, k, v, ig_s, lf_s, get_m, put_m, get_cn, kc, c):
    masks = kc['masks']
    fcum = _move_l(kc['lmat'][0:c, :], lf_s)
    a = ig_s - fcum
    imax = fcum + _scan0(a, jnp.maximum, -jnp.inf)
    fl = fcum[c - 1:c, :]
    lw = fl - fcum + ig_s
    lw_max = jnp.max(lw, axis=0, keepdims=True)
    kb = _bd(k, kc['bd256'][...])
    qk = _dot_nt(q, kb)
    a_e = _move_r(a, kc['e_c'][2])
    a_r = _rowform(a_e, masks[M_DIAG])
    yield
    m_row = yield from get_m()
    m_new = jnp.maximum(fl + m_row, lw_max)
    put_m(m_new)
    mt = jnp.maximum(fcum + m_row, imax)
    rows = lambda r: jnp.broadcast_to(r, (2 * SUB, 128))
    consts = [rows(fl + m_row - m_new), rows(m_row), rows(fl - m_new)]
    x1 = fcum - mt
    if c == HEAD_V:
        x1_e, lfl, lwc, m_d, sh_d = _move_rows([x1, -mt] + consts, kc['e_d'][2])
        x1_d, a_d = x1_e, a_e
    else:
        x1_e = _move_r(x1, kc['e_c'][2])
        x1_d, lfl, a_d, lwc, m_d, sh_d = _move_rows([x1, -mt, a] + consts, kc['e_d'][2])
    w_inter = jnp.exp(x1_d + m_d[0:1, :])
    floor = jnp.exp(lfl)
    ws = jnp.exp(a_d + sh_d[0:1, :])
    wc = jnp.exp(lwc)[0:1, :]
    wsv_t = (ws * v).T
    yield
    w_intra = jnp.exp(jnp.minimum(x1_e + a_r, 0.0)) * masks[M_INCL] * qk
    num = _dot(w_intra, _bd(v, kc['bd256'][...]))
    den = _dot(w_intra, kc['iep'][...])
    upd = kc['st256'][...] * _dot(wsv_t, k)
    n_upd = jnp.sum(ws * k, axis=0, keepdims=True)
    yield
    ct, n_row = yield from get_cn()
    ct_new = wc * ct + upd
    n_new = wc * n_row + n_upd
    num = num + w_inter * _dot_nt(q, ct)
    den = den + w_inter * _dot(q * n_row, kc['ones256'][...])
    hh = num / jnp.maximum(jnp.abs(den), floor)
    return hh, ct_new, n_new


def _mixer_kernel(x_ref, mod_ref, gmix_ref, win_ref, wout_ref, convw_ref, sp_ref, gn_ref, lbl_ref, lbs_ref,
                  wup_ref, bup_ref,
                  masks_ref, ec_ref, ed_ref, bd256_ref, bd128_ref, bdp_ref, st256_ref, st128_ref,
                  ie256_ref, ie128_ref, iep_ref, ones256_ref, lmat_ref, sel64_ref, sel32_ref,
                  conv0_ref, sdn0_ref, shg0_ref, sgla0_ref, c0_ref, n0_ref, m0_ref,
                  y_ref, convo_ref, sdno_ref, shgo_ref, sglao_ref, cmlo_ref, nml_ref, mml_ref,
                  z_ref, xp_ref, qkv_ref, mix_ref, sdn_ref, shg_ref, sgla_ref, cml_ref, *, nb, tb, c, ilv):
    ti = pl.program_id(1)
    rows = nb * tb
    nchunk = tb // c
    mats = ((sdn0_ref, sdno_ref, sdn_ref, sel64_ref), (shg0_ref, shgo_ref, shg_ref, sel64_ref),
            (sgla0_ref, sglao_ref, sgla_ref, sel32_ref), (c0_ref, cmlo_ref, cml_ref, sel64_ref))

    @pl.when(ti == 0)
    def _():
        convo_ref[...] = conv0_ref[...]
        nml_ref[...] = n0_ref[...]
        mml_ref[...] = m0_ref[...]
        for raw_ref, _, st_ref, sel_ref in mats:
            place = _move_r if st_ref is sdn_ref else _move_tn
            for b in range(nb):
                st_ref[b] = jnp.concatenate([place(raw_ref[b, hd], sel_ref[hd]) for hd in range(N_HEADS)], axis=0)

    x = x_ref[...]
    mod = mod_ref[...]
    h = _rmsnorm_rows(x, gmix_ref[...]) * (1.0 + mod[:, 1:2, :]) + mod[:, 0:1, :]
    z_ref[...] = jnp.dot(h.reshape(rows, D_MODEL).astype(BF16), win_ref[...], preferred_element_type=F32)

    convw = convw_ref[...]
    for b in range(nb):
        xp_ref[b, SUB - (DN_CONV - 1):SUB, :] = convo_ref[b]
        xp_ref[b, SUB:SUB + tb, :] = z_ref[b * tb:(b + 1) * tb, Z_DNQKV:Z_DNQKV + DN_CONV_CH]
        acc = xp_ref[b, SUB - 3:SUB - 3 + tb, :] * convw[0:1, :]
        for j in range(1, DN_CONV):
            acc = acc + xp_ref[b, SUB - 3 + j:SUB - 3 + j + tb, :] * convw[j:j + 1, :]
        qkv_ref[b * tb:(b + 1) * tb, :] = _silu(acc)
        convo_ref[b] = xp_ref[b, SUB + tb - (DN_CONV - 1):SUB + tb, :]

    kc = dict(masks=masks_ref, e_c=ec_ref, e_d=ed_ref, bd256=bd256_ref, bd128=bd128_ref, bdp=bdp_ref,
              st256=st256_ref, st128=st128_ref, ie256=ie256_ref, ie128=ie128_ref, iep=iep_ref,
              ones256=ones256_ref, lmat=lmat_ref)
    sp = sp_ref[...]
    gn = gn_ref[...]
    lbl = lbl_ref[...]
    lbs = lbs_ref[...]
    lbe = jnp.exp(lbl - jnp.max(lbl, axis=0, keepdims=True))
    lb = jnp.sum(lbs * (lbe / jnp.sum(lbe, axis=0, keepdims=True)), axis=0, keepdims=True)
    log_lb = jnp.log(lb)
    log_1mlb = jnp.log1p(-lb)
    neg_a = -jnp.exp(sp[1:2, :])
    chain = nchunk > 1

    def chunk_gens(i, u, boxes):
        if nchunk == 1:
            b = i
        elif nb == 1:
            b = 0
        else:
            b = i // nchunk
        rs = pl.ds(pl.multiple_of(i * c, c), c)
        zc = lambda off, w: z_ref[rs, off:off + w]

        def getter(key, read):
            def get():
                if chain and u > 0:
                    return (yield from _await(boxes, (key, u - 1)))
                return read()
                yield
            return get

        def dn():
            small = zc(Z_SMALL, 128)
            sb = small + sp[0:1, :]
            qkv = qkv_ref[rs, :]
            cq, ck, cv = qkv[:, 0:GROUP_W], qkv[:, GROUP_W:2 * GROUP_W], qkv[:, 2 * GROUP_W:3 * GROUP_W]
            ss = _dot(jnp.concatenate([cq * cq, ck * ck], axis=0), kc['ones256'][...])
            ssq, ssk = ss[0:c], ss[c:2 * c]
            beta_s = _sigmoid(small)
            gam_s = _move_l(kc['lmat'][0:c, :], neg_a * _softplus(sb))
            yield
            dq = cq * lax.rsqrt(ssq + EPS) * (DN_DK ** -0.5)
            dk = ck * lax.rsqrt(ssk + EPS)
            o, st_new = yield from _deltanet_chunk(dq, dk, cv, beta_s, gam_s,
                                                   getter('dn', lambda: sdn_ref[b]), kc, c)
            boxes[('dn', u)] = st_new
            sdn_ref[b] = st_new
            boxes[('out', u, 0)] = o

        def hg():
            zf = zc(Z_HGF, GROUP_W)
            lsz = _log_sigmoid(zf)
            t2 = log_1mlb + lsz
            mx = jnp.maximum(log_lb, t2)
            log_f = mx + jnp.log(jnp.exp(log_lb - mx) + jnp.exp(t2 - mx))
            key_hg = (1.0 - lb) * _sigmoid(-zf)
            o, st_new = yield from _gla_chunk(_silu(zc(Z_HGQ, GROUP_W)), key_hg, zc(Z_HGI, GROUP_W), log_f,
                                              getter('hg', lambda: shg_ref[b]), kc, c, True)
            boxes[('hg', u)] = st_new
            shg_ref[b] = st_new
            boxes[('out', u, 1)] = o

        def gla():
            small = zc(Z_SMALL, 128)
            g_gla = _log_sigmoid(_dot(small, wup_ref[...]) + bup_ref[...]) * (1.0 / GLA_TAU)
            yield
            o, st_new = yield from _gla_chunk(zc(Z_GLAQ, GLA_W) * (GLA_DK ** -0.5), zc(Z_GLAK, GLA_W),
                                              zc(Z_GLAV, GROUP_W), g_gla,
                                              getter('gla', lambda: sgla_ref[b]), kc, c, False)
            boxes[('gla', u)] = st_new
            sgla_ref[b] = st_new
            boxes[('out', u, 2)] = o

        def ml():
            sb = zc(Z_SMALL, 128) + sp[0:1, :]
            ig_s = pltpu.roll(sb, S_MLF - S_MLI, axis=1)
            lf_s = _log_sigmoid(sb)

            def put_m(m_new):
                boxes[('ml_m', u)] = m_new
                mml_ref[b] = m_new

            hh, c_new, n_new = yield from _mlstm_chunk(
                zc(Z_MLQ, GROUP_W) * (ML_DK ** -0.5), zc(Z_MLK, GROUP_W), zc(Z_MLV, GROUP_W), ig_s, lf_s,
                getter('ml_m', lambda: mml_ref[b]), put_m,
                getter('ml_cn', lambda: (cml_ref[b], nml_ref[b])), kc, c)
            boxes[('ml_cn', u)] = (c_new, n_new)
            cml_ref[b] = c_new
            nml_ref[b] = n_new
            boxes[('out', u, 3)] = hh

        late = STAGGER * u if chain else 0
        return [(0, dn()), (late, hg()), (late, gla()), (late, ml())]

    def chunk_body(it, carry):
        boxes = {}
        gens = []
        for u in range(ilv):
            gens += chunk_gens(it * ilv + u, u, boxes)
        _run_interleaved(gens)
        outs = jnp.concatenate([boxes[('out', u, m)] for u in range(ilv) for m in range(4)], axis=0)
        ms = _dot(outs * outs, kc['ones256'][...]) * (1.0 / HEAD_V)
        normed = outs * lax.rsqrt(ms + EPS)
        gate_cols = (Z_DNG, Z_HGG, Z_GLAG, Z_MLO)
        for u in range(ilv):
            rs = pl.ds(pl.multiple_of((it * ilv + u) * c, c), c)
            for m in range(4):
                zg = z_ref[rs, gate_cols[m]:gate_cols[m] + GROUP_W]
                gate = _sigmoid(zg) if m == 3 else _silu(zg)
                r0 = (u * 4 + m) * c
                mix_ref[rs, m * GROUP_W:(m + 1) * GROUP_W] = (normed[r0:r0 + c] * gn[m:m + 1, :] * gate).astype(BF16)
        return carry

    lax.fori_loop(0, nb * nchunk // ilv, chunk_body, 0)

    @pl.when(ti == pl.num_programs(1) - 1)
    def _():
        for _, out_ref, st_ref, sel_ref in mats:
            for b in range(nb):
                for hd in range(N_HEADS):
                    blk = st_ref[b, hd * HEAD_V:(hd + 1) * HEAD_V, :]
                    out_ref[b, hd] = _move_nt(blk, sel_ref[hd]) if st_ref is sdn_ref else _move_nt(sel_ref[hd], blk)

    out = jnp.dot(mix_ref[...], wout_ref[...], preferred_element_type=F32).reshape(nb, tb, D_MODEL)
    y_ref[...] = x + mod[:, 2:3, :] * out


def _const_tables(c):
    pc = N_HEADS * c
    t = np.arange(c)[:, None]
    lane = np.arange(pc)[None, :]
    hs, s = lane // c, lane % c
    masks = [s <= t, s < t, s == t, (s // 16) == (t // 16)]
    sz = c // 2
    while sz >= 1:
        masks.append(((s // (2 * sz)) == (t // (2 * sz))) & ((t // sz) % 2 == 1) & ((s // sz) % 2 == 0))
        sz //= 2
    masks = np.stack([np.broadcast_to(m, (c, pc)) for m in masks]).astype(np.float32)
    r = np.arange(c)[None, :]
    tri = (r <= t).astype(np.float32)
    lmat = np.concatenate([tri, tri - tri[(np.arange(c) // 4) * 4 + 1]], axis=0)

    def expand(col0, w):
        j = np.arange(128)[:, None]
        l = np.arange(N_HEADS * w)[None, :]
        return (j == col0 + l // w).astype(np.float32)

    e_c = np.stack([expand(S_DNB, c), expand(S_DNA, c), expand(S_MLF, c)])
    e_d = np.stack([expand(S_DNB, HEAD_V), expand(S_DNA, HEAD_V), expand(S_MLF, HEAD_V)])

    def headsel(dk):
        d = np.arange(dk)[None, :, None]
        l = np.arange(N_HEADS * dk)[None, None, :]
        return (l == np.arange(N_HEADS)[:, None, None] * dk + d).astype(np.float32)

    def blk(nr, rg, nl, lg):
        return ((np.arange(nr)[:, None] // rg) == (np.arange(nl)[None, :] // lg)).astype(np.float32)

    tabs = dict(
        masks=jnp.asarray(masks), e_c=jnp.asarray(e_c, BF16), e_d=jnp.asarray(e_d, BF16),
        bd256=jnp.asarray(blk(pc, c, GROUP_W, HEAD_V), BF16), bd128=jnp.asarray(blk(pc, c, GLA_W, GLA_DK), BF16),
        bdp=jnp.asarray(blk(pc, c, pc, c), BF16),
        st256=jnp.asarray(blk(GROUP_W, HEAD_V, GROUP_W, HEAD_V)), st128=jnp.asarray(blk(GROUP_W, HEAD_V, GLA_W, GLA_DK)),
        ie256=jnp.asarray(blk(GROUP_W, HEAD_V, pc, c), BF16), ie128=jnp.asarray(blk(GLA_W, GLA_DK, pc, c), BF16),
        iep=jnp.asarray(blk(pc, c, GROUP_W, HEAD_V), BF16),
        ones256=jnp.asarray(blk(GROUP_W, HEAD_V, GROUP_W, HEAD_V), BF16), lmat=jnp.asarray(lmat, BF16),
        sel64=jnp.asarray(headsel(HEAD_V), BF16), sel32=jnp.asarray(headsel(GLA_DK), BF16))
    order = ['masks', 'e_c', 'e_d', 'bd256', 'bd128', 'bdp', 'st256', 'st128', 'ie256', 'ie128', 'iep', 'ones256',
             'lmat', 'sel64', 'sel32']
    return [tabs[k] for k in order]


def _full_spec(a):
    nd = a.ndim
    return pl.BlockSpec(a.shape, lambda bi, ti, _n=nd: (0,) * _n, pipeline_mode=pl.Buffered(1))


def _mixer_call(x, mod, lw, states, nb, tb):
    bsz, t, _ = x.shape
    c = min(CHUNK, t)
    assert t % tb == 0 and tb % c == 0 and bsz % nb == 0 and c % 16 == 0
    rows = nb * tb
    consts = _const_tables(c)
    params = [lw['g_mix'], lw['w_in'], lw['w_out'], lw['conv_w'], lw['sp'], lw['gn'], lw['lb_logits'], lw['lb_sel'],
              lw['wup'], lw['bup']]
    xspec = pl.BlockSpec((nb, tb, D_MODEL), lambda bi, ti: (bi, ti, 0))

    def bspec(a):
        nd = a.ndim
        return pl.BlockSpec((nb,) + a.shape[1:], lambda bi, ti, _n=nd: (bi,) + (0,) * (_n - 1))

    in_specs = ([xspec, bspec(mod)] + [_full_spec(a) for a in params] + [_full_spec(a) for a in consts]
                + [bspec(s) for s in states])
    out_shape = [jax.ShapeDtypeStruct(x.shape, F32)] + [jax.ShapeDtypeStruct(s.shape, F32) for s in states]
    out_specs = [xspec] + [bspec(s) for s in states]
    scratch = [pltpu.VMEM((rows, N_IN), F32),
               pltpu.VMEM((nb, SUB + tb, DN_CONV_CH), F32),
               pltpu.VMEM((rows, DN_CONV_CH), F32),
               pltpu.VMEM((rows, MIX_W), BF16),
               pltpu.VMEM((nb, GROUP_W, N_HEADS * DN_DK), F32),
               pltpu.VMEM((nb, GROUP_W, N_HEADS * HG_DK), F32),
               pltpu.VMEM((nb, GROUP_W, GLA_W), F32),
               pltpu.VMEM((nb, GROUP_W, N_HEADS * ML_DK), F32)]
    ilv = MAX_INTERLEAVE
    while (nb * (tb // c)) % ilv or (tb // c > 1 and (tb // c) % ilv):
        ilv //= 2
    kern = functools.partial(_mixer_kernel, nb=nb, tb=tb, c=c, ilv=ilv)
    return pl.pallas_call(
        kern, grid=(bsz // nb, t // tb), in_specs=in_specs, out_specs=out_specs, out_shape=out_shape,
        scratch_shapes=scratch, name='mixer',
        compiler_params=pltpu.CompilerParams(dimension_semantics=('arbitrary', 'arbitrary'),
                                             vmem_limit_bytes=VMEM_LIMIT),
    )(x, mod, *params, *consts, *states)


FF_TILE = 256


def _ffn_kernel(x_ref, mod_ref, gffn_ref, wup_ref, wdown_ref, gfin_ref, y_ref, *, nb, tb, final):
    rows = nb * tb
    x = x_ref[...]
    mod = mod_ref[...]
    h = _rmsnorm_rows(x, gffn_ref[...]) * (1.0 + mod[:, 4:5, :]) + mod[:, 3:4, :]
    hb = h.reshape(rows, D_MODEL).astype(BF16)
    acc = jnp.zeros((rows, D_MODEL), F32)
    for j in range(D_FF // FF_TILE):
        gate = jnp.dot(hb, wup_ref[:, j * FF_TILE:(j + 1) * FF_TILE], preferred_element_type=F32)
        up = jnp.dot(hb, wup_ref[:, D_FF + j * FF_TILE:D_FF + (j + 1) * FF_TILE], preferred_element_type=F32)
        act = (_silu(gate) * up).astype(BF16)
        acc = acc + jnp.dot(act, wdown_ref[j * FF_TILE:(j + 1) * FF_TILE, :], preferred_element_type=F32)
    y = x + mod[:, 5:6, :] * acc.reshape(nb, tb, D_MODEL)
    if final:
        y = _rmsnorm_rows(y, gfin_ref[...])
    y_ref[...] = y


def _ffn_call(x, mod, lw, g_final, nb, tb, final):
    bsz, t, _ = x.shape
    xspec = pl.BlockSpec((nb, tb, D_MODEL), lambda bi, ti: (bi, ti, 0))
    mspec = pl.BlockSpec((nb, 6, D_MODEL), lambda bi, ti: (bi, 0, 0))
    params = [lw['g_ffn'], lw['w_up'], lw['w_down'], g_final]
    kern = functools.partial(_ffn_kernel, nb=nb, tb=tb, final=final)
    return pl.pallas_call(
        kern, grid=(bsz // nb, t // tb), in_specs=[xspec, mspec] + [_full_spec(a) for a in params],
        out_specs=xspec, out_shape=jax.ShapeDtypeStruct(x.shape, F32), name='ffn',
        compiler_params=pltpu.CompilerParams(dimension_semantics=('arbitrary', 'arbitrary'),
                                             vmem_limit_bytes=VMEM_LIMIT),
    )(x, mod, *params)


ADA_TILE = 1536


def _ada_kernel(c_ref, w_ref, b_ref, o_ref):
    o_ref[0] = jnp.dot(_silu(c_ref[...]).astype(BF16), w_ref[0], preferred_element_type=F32) + b_ref[0]


def _ada_call(c_all, w_ada, b_ada):
    n = c_all.shape[0]
    nt = 6 * D_MODEL // ADA_TILE
    return pl.pallas_call(
        _ada_kernel, grid=(DEPTH, nt),
        in_specs=[pl.BlockSpec((n, D_MODEL), lambda l, j: (0, 0)),
                  pl.BlockSpec((1, D_MODEL, ADA_TILE), lambda l, j: (l, 0, j)),
                  pl.BlockSpec((1, 1, ADA_TILE), lambda l, j: (l, 0, j))],
        out_specs=pl.BlockSpec((1, n, ADA_TILE), lambda l, j: (l, 0, j)),
        out_shape=jax.ShapeDtypeStruct((DEPTH, n, 6 * D_MODEL), F32), name='ada',
        compiler_params=pltpu.CompilerParams(dimension_semantics=('arbitrary', 'arbitrary')),
    )(c_all, w_ada, b_ada.reshape(DEPTH, 1, 6 * D_MODEL))


def _permute_w_in(w):
    d = w.shape[0]
    zeros = lambda n: jnp.zeros((d, n), w.dtype)
    small = jnp.concatenate([w[:, 768:772], zeros(12), w[:, 2568:2584], w[:, 772:776], zeros(28),
                             w[:, 3608:3612], zeros(28), w[:, 3612:3616], zeros(28)], axis=1)
    return jnp.concatenate([w[:, 0:768], w[:, 776:2568], w[:, 2584:3608], w[:, 3616:3872], small], axis=1)


def _lane_row(pairs, width=128):
    row = jnp.zeros((width,), F32)
    for off, val in pairs:
        row = row.at[off:off + val.shape[0]].set(val.astype(F32))
    return row


def _layer_weights(p, l):
    sp = jnp.zeros((SUB, 128), F32)
    sp = sp.at[0].set(_lane_row([(S_DNA, p['dn_dt_bias'][l]), (S_MLI, p['ml_i_bias'][l]), (S_MLF, p['ml_f_bias'][l])]))
    sp = sp.at[1].set(_lane_row([(S_DNA, p['dn_a_log'][l])]))
    gn = jnp.stack([jnp.tile(p[k][l].astype(F32), N_HEADS) for k in ('dn_norm_g', 'hg_norm_g', 'gla_norm_g', 'ml_norm_g')])
    wup = jnp.zeros((128, GLA_W), F32).at[S_GLAR:S_GLAR + GLA_RANK].set(p['gla_w_up'][l]).astype(BF16)
    lb_sel = (jnp.arange(DEPTH) >= 1) & (jnp.arange(DEPTH) <= l)
    return dict(
        g_mix=p['g_mix'][l].reshape(1, D_MODEL), g_ffn=p['g_ffn'][l].reshape(1, D_MODEL),
        w_in=_permute_w_in(p['w_in'][l]).astype(BF16), w_out=p['w_out'][l].astype(BF16),
        conv_w=p['dn_conv_w'][l], sp=sp, gn=gn, lb_logits=p['hg_lb_logits'].astype(F32),
        lb_sel=lb_sel.astype(F32).reshape(DEPTH, 1), wup=wup, bup=p['gla_b_up'][l].reshape(1, GLA_W).astype(F32),
        w_up=p['w_up'][l].astype(BF16), w_down=p['w_down'][l].astype(BF16))


def _trunk(x, mods, states, lws, g_final, nb, tb):
    new_states = []
    for l in range(DEPTH):
        outs = _mixer_call(x, mods[l], lws[l], states[l], nb, tb)
        x = outs[0]
        new_states.append(outs[1:])
        x = _ffn_call(x, mods[l], lws[l], g_final, *_tiling(x.shape[0], x.shape[1], FFN_ROWS), l == DEPTH - 1)
    return x, new_states


def _pack_states(conv, s_dn, s_hg, s_gla, c_ml, n_ml, m_ml, l):
    b = conv.shape[1]
    m_row = jnp.zeros((b, 1, 128), F32).at[:, 0, S_MLF:S_MLF + N_HEADS].set(m_ml[l].astype(F32))
    return (conv[l].astype(F32), s_dn[l].astype(F32), s_hg[l].astype(F32), s_gla[l].astype(F32),
            c_ml[l].astype(F32), n_ml[l].astype(F32).reshape(b, 1, N_HEADS * ML_DK), m_row)


def _unpack_states(sts):
    conv, s_dn, s_hg, s_gla, c_ml = (jnp.stack([s[i] for s in sts]) for i in range(5))
    n_ml = jnp.stack([s[5].reshape(s[5].shape[0], N_HEADS, ML_DK) for s in sts])
    m_ml = jnp.stack([s[6][:, 0, S_MLF:S_MLF + N_HEADS] for s in sts])
    return conv, s_dn, s_hg, s_gla, c_ml, n_ml, m_ml


def _zero_states(b):
    z = lambda *s: jnp.zeros(s, F32)
    return (z(b, DN_CONV - 1, DN_CONV_CH), z(b, N_HEADS, DN_DK, HEAD_V), z(b, N_HEADS, HG_DK, HEAD_V),
            z(b, N_HEADS, GLA_DK, HEAD_V), z(b, N_HEADS, ML_DK, HEAD_V), z(b, 1, N_HEADS * ML_DK), z(b, 1, 128))


def _tiling(bsz, t, rows):
    tb = min(t, rows)
    nb = max(1, min(bsz, rows // tb))
    while bsz % nb:
        nb -= 1
    return nb, tb


def kernel(x_prompt, x_sample, c_prompt, c_sample, cache_dn_conv, state_dn, state_hgrn, state_gla, state_mlstm_c, state_mlstm_n, state_mlstm_m, w_ada, b_ada, g_mix, g_ffn, w_in, dn_conv_w, dn_a_log, dn_dt_bias, dn_norm_g, hg_lb_logits, hg_norm_g, gla_w_up, gla_b_up, gla_norm_g, ml_i_bias, ml_f_bias, ml_norm_g, w_out, w_up, w_down, g_final):
    p = dict(g_mix=g_mix, g_ffn=g_ffn, w_in=w_in, dn_conv_w=dn_conv_w, dn_a_log=dn_a_log, dn_dt_bias=dn_dt_bias,
             dn_norm_g=dn_norm_g, hg_lb_logits=hg_lb_logits, hg_norm_g=hg_norm_g, gla_w_up=gla_w_up,
             gla_b_up=gla_b_up, gla_norm_g=gla_norm_g, ml_i_bias=ml_i_bias, ml_f_bias=ml_f_bias,
             ml_norm_g=ml_norm_g, w_out=w_out, w_up=w_up, w_down=w_down)
    lws = [_layer_weights(p, l) for l in range(DEPTH)]
    gfin = g_final.reshape(1, D_MODEL).astype(F32)
    bp, bs = x_prompt.shape[0], x_sample.shape[0]
    mod = _ada_call(jnp.concatenate([c_prompt, c_sample], axis=0).astype(F32), w_ada.astype(BF16),
                    b_ada.astype(F32)).reshape(DEPTH, bp + bs, 6, D_MODEL)

    outs = []
    raw = (cache_dn_conv, state_dn, state_hgrn, state_gla, state_mlstm_c, state_mlstm_n, state_mlstm_m)
    for x, lo, hi, states in ((x_prompt, 0, bp, [_zero_states(bp)] * DEPTH),
                              (x_sample, bp, bp + bs, [_pack_states(*raw, l) for l in range(DEPTH)])):
        nb, tb = _tiling(x.shape[0], x.shape[1], MIXER_ROWS)
        y, new = _trunk(x.astype(F32), [mod[l, lo:hi] for l in range(DEPTH)], states, lws, gfin, nb, tb)
        outs.append((y, _unpack_states(new)))
    (y_p, st_p), (y_s, st_s) = outs
    return (y_p, y_s) + tuple(st_p) + tuple(st_s)
```

```python
import functools

import numpy as np
import jax
import jax.numpy as jnp
from jax import lax
from jax.experimental import pallas as pl
from jax.experimental.pallas import tpu as pltpu

F32 = jnp.float32
BF16 = jnp.bfloat16

D_MODEL = 1024
DEPTH = 2
CHUNK = 64
N_HEADS = 4
HEAD_V = 64
GROUP_W = N_HEADS * HEAD_V
MIX_W = 4 * GROUP_W
DN_DK = 64
DN_CONV = 4
DN_CONV_CH = 3 * GROUP_W
HG_DK = 64
GLA_DK = 32
GLA_W = N_HEADS * GLA_DK
GLA_RANK = 16
GLA_TAU = 16.0
ML_DK = 64
D_FF = 2816
EPS = 1e-6

Z_DNQKV = 0
Z_DNG = 768
Z_HGQ = 1024
Z_HGF = 1280
Z_HGI = 1536
Z_HGG = 1792
Z_GLAQ = 2048
Z_GLAK = 2176
Z_GLAV = 2304
Z_GLAG = 2560
Z_MLQ = 2816
Z_MLK = 3072
Z_MLV = 3328
Z_MLO = 3584
Z_SMALL = 3840
N_IN = 3968
S_DNB = 0
S_GLAR = 16
S_DNA = 32
S_MLI = 64
S_MLF = 96

SUB = 8
LANES = 128
TRI_BLOCK = 16
VMEM_LIMIT = 56 * 1024 * 1024

M_INCL, M_STRICT, M_DIAG, M_SAMEBLK = 0, 1, 2, 3
M_LEVEL0 = 4
LOG2E = 1.4426950408889634
MIXER_ROWS = 256
MIXER_TOKENS = 256
FFN_ROWS = 1024
STAGGER = 3
MAX_INTERLEAVE = 4


def _sigmoid(x):
    return 1.0 / (1.0 + jnp.exp(-x))


def _silu(x):
    return x * _sigmoid(x)


def _log_sigmoid(x):
    return jnp.minimum(x, 0.0) - jnp.log1p(jnp.exp(-jnp.abs(x)))


def _softplus(x):
    return jnp.maximum(x, 0.0) + jnp.log1p(jnp.exp(-jnp.abs(x)))


def _dot(a, b):
    return jnp.dot(a.astype(BF16), b.astype(BF16), preferred_element_type=F32)


def _dot_nt(a, b):
    return lax.dot_general(a.astype(BF16), b.astype(BF16), (((1,), (1,)), ((), ())),
                           preferred_element_type=F32)


def _split3(x):
    x1 = x.astype(BF16)
    r = x - x1.astype(F32)
    x2 = r.astype(BF16)
    r = r - x2.astype(F32)
    return x1, x2, r.astype(BF16)


def _move_rows(xs, sel):
    parts = [_split3(x) for x in xs]
    y = jnp.dot(jnp.concatenate([p[i] for i in range(3) for p in parts], axis=0), sel, preferred_element_type=F32)
    n = sum(x.shape[0] for x in xs)
    outs = []
    off = 0
    for x in xs:
        r = x.shape[0]
        outs.append(y[off:off + r] + (y[n + off:n + off + r] + y[2 * n + off:2 * n + off + r]))
        off += r
    return outs


def _move_r(x, sel):
    return _move_rows([x], sel)[0]


def _move_l(sel, x):
    x1, x2, x3 = _split3(x)
    d = lambda a: jnp.dot(sel, a, preferred_element_type=F32)
    return d(x1) + (d(x2) + d(x3))


def _move_tn(x, sel):
    x1, x2, x3 = _split3(x)
    d = lambda a: lax.dot_general(a, sel, (((0,), (0,)), ((), ())), preferred_element_type=F32)
    return d(x1) + (d(x2) + d(x3))


def _move_nt(a, b):
    nt = lambda x, y: lax.dot_general(x, y, (((1,), (1,)), ((), ())), preferred_element_type=F32)
    if a.dtype == BF16:
        d = lambda p: nt(a, p)
        x1, x2, x3 = _split3(b)
    else:
        d = lambda p: nt(p, b)
        x1, x2, x3 = _split3(a)
    return d(x1) + (d(x2) + d(x3))


def _bd(x, mask):
    xb = x.astype(BF16)
    return jnp.concatenate([xb] * N_HEADS, axis=0) * mask


def _scan0(x, op, fill):
    n = x.shape[0]
    row = lax.broadcasted_iota(jnp.int32, x.shape, 0)
    sh = 1
    while sh < n:
        r = pltpu.roll(x, sh, axis=0)
        x = op(x, jnp.where(row >= sh, r, fill))
        sh *= 2
    return x


def _rowform(xe, diag):
    return jnp.sum(xe * diag, axis=0, keepdims=True)


def _rmsnorm_rows(x, g):
    return x * lax.rsqrt(jnp.mean(x * x, axis=-1, keepdims=True) + EPS) * g


def _run_interleaved(gens):
    live = list(gens)
    rnd = 0
    while live:
        alive = []
        for start, g in live:
            if rnd >= start:
                try:
                    next(g)
                except StopIteration:
                    continue
            alive.append((start, g))
        live = alive
        rnd += 1


def _await(boxes, key):
    while key not in boxes:
        yield
    return boxes[key]


def _tri_solve(mm, rhs, masks, bdp, bdr, c):
    assert c // TRI_BLOCK <= 4
    eye = masks[M_DIAG]
    mul = lambda a, b: _dot(a, _bd(b, bdp))
    app = lambda a, r: _dot(a, _bd(r, bdr))
    md = mm * masks[M_SAMEBLK]
    mo = mm - md
    p2 = mul(md, md)
    yield
    d = eye - md
    d = d + mul(d, p2)
    p4 = mul(p2, p2)
    yield
    d = d + mul(d, p4)
    p8 = mul(p4, p4)
    yield
    d = d + mul(d, p8)
    yield
    n = mul(d, mo)
    ys = [app(d, r) for r in rhs]
    yield
    zs = [y - app(n, y) for y in ys]
    if c // TRI_BLOCK <= 2:
        yield
        return zs
    n2 = mul(n, n)
    yield
    ws = [z + app(n2, z) for z in zs]
    yield
    return ws


def _deltanet_chunk(q, k, v, beta_s, gam_s, get_state, kc, c):
    masks = kc['masks']
    gam_e = _move_r(gam_s, kc['e_c'][1])
    beta_e = _move_r(beta_s, kc['e_c'][0])
    if c == HEAD_V:
        gam_d, beta_d = gam_e, beta_e
    else:
        gam_d = _move_r(gam_s, kc['e_d'][1])
        beta_d = _move_r(beta_s, kc['e_d'][0])
    kq = _dot_nt(jnp.concatenate([k, q], axis=0), _bd(k, kc['bd256'][...]))
    kk, qk = kq[0:c], kq[c:2 * c]
    yield
    gam_r = _rowform(gam_e, masks[M_DIAG])
    dec = jnp.exp(jnp.minimum(gam_e - gam_r, 0.0))
    mm = beta_e * kk * dec * masks[M_STRICT]
    eg = jnp.exp(gam_d)
    w, u0 = yield from _tri_solve(mm, [beta_d * eg * k, beta_d * v], masks, kc['bdp'][...], kc['bd256'][...], c)
    gl = gam_d[c - 1:c, :]
    kdec_t = (k * jnp.exp(gl - gam_d)).T
    st = yield from get_state()
    wq = _dot(jnp.concatenate([w, q * eg], axis=0), st)
    u = u0 - wq[0:c]
    qs = wq[c:2 * c]
    yield
    st_new = jnp.exp(gl) * st + kc['st256'][...] * _dot(kdec_t, u)
    o = qs + _dot(qk * dec * masks[M_INCL], _bd(u, kc['bd256'][...]))
    return o, st_new


def _block_ref(b, sz):
    c, w = b.shape
    g3 = b.reshape(c // (2 * sz), 2 * sz, w)
    return jnp.broadcast_to(g3[:, sz - 1:sz, :], g3.shape).reshape(c, w)


def _gla_chunk(q, k, v, g, get_state, kc, c, wide):
    masks = kc['masks']
    bdk = kc['bd256'][...] if wide else kc['bd128'][...]
    ie = kc['ie256'][...] if wide else kc['ie128'][...]
    stm = kc['st256'][...] if wide else kc['st128'][...]
    g2 = g * LOG2E
    v_t = v.T
    bd2 = _move_l(kc['lmat'][...], g2)
    b = bd2[0:c, :]
    yield
    row = lax.broadcasted_iota(jnp.int32, b.shape, 0)
    x1 = jnp.where(jnp.bitwise_and(row, 1) == 1, q * pltpu.roll(k, 1, axis=0) * jnp.exp2(g2), 0.0)
    dd = _dot(jnp.concatenate([q * k, x1], axis=0), ie)
    nlvl = c.bit_length() - 1
    attn = dd[0:c] * masks[M_DIAG] + dd[c:2 * c] * masks[M_LEVEL0 + nlvl - 1]
    lvl = 0
    sz = c // 2
    while sz >= 2:
        d = b - _block_ref(b, sz) if sz >= 4 else bd2[c:2 * c, :]
        e = jnp.exp2(jnp.minimum(d, -d))
        attn = attn + _dot_nt(q * e, _bd(k * e, bdk)) * masks[M_LEVEL0 + lvl]
        yield
        sz //= 2
        lvl += 1
    last = b[c - 1:c, :]
    upd = stm * _dot(v_t, k * jnp.exp2(last - b))
    o = _dot(attn, _bd(v, kc['bd256'][...]))
    yield
    st = yield from get_state()
    st_new = jnp.exp2(last) * st + upd
    o = o + _dot_nt(q * jnp.exp2(b), st)
    return o, st_new


def _mlstm_chunk(q, k, v, ig_s, lf_s, get_m, put_m, get_cn, kc, c):
    masks = kc['masks']
    fcum = _move_l(kc['lmat'][0:c, :], lf_s)
    a = ig_s - fcum
    imax = fcum + _scan0(a, jnp.maximum, -jnp.inf)
    fl = fcum[c - 1:c, :]
    lw = fl - fcum + ig_s
    lw_max = jnp.max(lw, axis=0, keepdims=True)
    kb = _bd(k, kc['bd256'][...])
    qk = _dot_nt(q, kb)
    a_e = _move_r(a, kc['e_c'][2])
    a_r = _rowform(a_e, masks[M_DIAG])
    yield
    m_row = yield from get_m()
    m_new = jnp.maximum(fl + m_row, lw_max)
    put_m(m_new)
    mt = jnp.maximum(fcum + m_row, imax)
    rows = lambda r: jnp.broadcast_to(r, (2 * SUB, LANES))
    consts = [rows(fl + m_row - m_new), rows(m_row), rows(fl - m_new)]
    x1 = fcum - mt
    if c == HEAD_V:
        x1_e, lfl, lwc, m_d, sh_d = _move_rows([x1, -mt] + consts, kc['e_d'][2])
        x1_d, a_d = x1_e, a_e
    else:
        x1_e = _move_r(x1, kc['e_c'][2])
        x1_d, lfl, a_d, lwc, m_d, sh_d = _move_rows([x1, -mt, a] + consts, kc['e_d'][2])
    w_inter = jnp.exp(x1_d + m_d[0:1, :])
    floor = jnp.exp(lfl)
    ws = jnp.exp(a_d + sh_d[0:1, :])
    wc = jnp.exp(lwc)[0:1, :]
    wsv_t = (ws * v).T
    yield
    w_intra = jnp.exp(jnp.minimum(x1_e + a_r, 0.0)) * masks[M_INCL] * qk
    num = _dot(w_intra, _bd(v, kc['bd256'][...]))
    den = _dot(w_intra, kc['iep'][...])
    upd = kc['st256'][...] * _dot(wsv_t, k)
    n_upd = jnp.sum(ws * k, axis=0, keepdims=True)
    yield
    ct, n_row = yield from get_cn()
    ct_new = wc * ct + upd
    n_new = wc * n_row + n_upd
    num = num + w_inter * _dot_nt(q, ct)
    den = den + w_inter * _dot(q * n_row, kc['ones256'][...])
    hh = num / jnp.maximum(jnp.abs(den), floor)
    return hh, ct_new, n_new


def _mixer_kernel(x_ref, mod_ref, gmix_ref, win_ref, wout_ref, convw_ref, sp_ref, gn_ref, lbl_ref, lbs_ref,
                  wup_ref, bup_ref,
                  masks_ref, ec_ref, ed_ref, bd256_ref, bd128_ref, bdp_ref, st256_ref, st128_ref,
                  ie256_ref, ie128_ref, iep_ref, ones256_ref, lmat_ref, sel64_ref, sel32_ref,
                  conv0_ref, sdn0_ref, shg0_ref, sgla0_ref, c0_ref, n0_ref, m0_ref,
                  y_ref, convo_ref, sdno_ref, shgo_ref, sglao_ref, cmlo_ref, nml_ref, mml_ref,
                  z_ref, xp_ref, qkv_ref, mix_ref, sdn_ref, shg_ref, sgla_ref, cml_ref, *, nb, tb, c, ilv):
    ti = pl.program_id(1)
    rows = nb * tb
    nchunk = tb // c
    mats = ((sdn0_ref, sdno_ref, sdn_ref, sel64_ref), (shg0_ref, shgo_ref, shg_ref, sel64_ref),
            (sgla0_ref, sglao_ref, sgla_ref, sel32_ref), (c0_ref, cmlo_ref, cml_ref, sel64_ref))

    @pl.when(ti == 0)
    def _():
        convo_ref[...] = conv0_ref[...]
        nml_ref[...] = n0_ref[...]
        mml_ref[...] = m0_ref[...]
        for raw_ref, _, st_ref, sel_ref in mats:
            place = _move_r if st_ref is sdn_ref else _move_tn
            for b in range(nb):
                st_ref[b] = jnp.concatenate([place(raw_ref[b, hd], sel_ref[hd]) for hd in range(N_HEADS)], axis=0)

    x = x_ref[...]
    mod = mod_ref[...]
    h = _rmsnorm_rows(x, gmix_ref[...]) * (1.0 + mod[:, 1:2, :]) + mod[:, 0:1, :]
    z_ref[...] = jnp.dot(h.reshape(rows, D_MODEL).astype(BF16), win_ref[...], preferred_element_type=F32)

    convw = convw_ref[...]
    for b in range(nb):
        xp_ref[b, SUB - (DN_CONV - 1):SUB, :] = convo_ref[b]
        xp_ref[b, SUB:SUB + tb, :] = z_ref[b * tb:(b + 1) * tb, Z_DNQKV:Z_DNQKV + DN_CONV_CH]
        acc = xp_ref[b, SUB - 3:SUB - 3 + tb, :] * convw[0:1, :]
        for j in range(1, DN_CONV):
            acc = acc + xp_ref[b, SUB - 3 + j:SUB - 3 + j + tb, :] * convw[j:j + 1, :]
        qkv_ref[b * tb:(b + 1) * tb, :] = _silu(acc)
        convo_ref[b] = xp_ref[b, SUB + tb - (DN_CONV - 1):SUB + tb, :]

    kc = dict(masks=masks_ref, e_c=ec_ref, e_d=ed_ref, bd256=bd256_ref, bd128=bd128_ref, bdp=bdp_ref,
              st256=st256_ref, st128=st128_ref, ie256=ie256_ref, ie128=ie128_ref, iep=iep_ref,
              ones256=ones256_ref, lmat=lmat_ref)
    sp = sp_ref[...]
    gn = gn_ref[...]
    lbl = lbl_ref[...]
    lbs = lbs_ref[...]
    lbe = jnp.exp(lbl - jnp.max(lbl, axis=0, keepdims=True))
    lb = jnp.sum(lbs * (lbe / jnp.sum(lbe, axis=0, keepdims=True)), axis=0, keepdims=True)
    log_lb = jnp.log(lb)
    log_1mlb = jnp.log1p(-lb)
    neg_a = -jnp.exp(sp[1:2, :])
    run = min(ilv, nchunk)
    assert ilv % run == 0 and nchunk % run == 0

    def chunk_gens(it, u, boxes):
        i = it * ilv + u
        pos = u % run
        if nb == 1:
            b = 0
        elif ilv % nchunk == 0:
            b = it * (ilv // nchunk) + u // nchunk
        else:
            b = i // nchunk
        rs = pl.ds(pl.multiple_of(i * c, c), c)
        zc = lambda off, w: z_ref[rs, off:off + w]

        def getter(key, read):
            def get():
                if pos > 0:
                    return (yield from _await(boxes, (key, u - 1)))
                return read()
                yield
            return get

        def dn():
            small = zc(Z_SMALL, LANES)
            sb = small + sp[0:1, :]
            qkv = qkv_ref[rs, :]
            cq, ck, cv = qkv[:, 0:GROUP_W], qkv[:, GROUP_W:2 * GROUP_W], qkv[:, 2 * GROUP_W:3 * GROUP_W]
            ss = _dot(jnp.concatenate([cq * cq, ck * ck], axis=0), kc['ones256'][...])
            ssq, ssk = ss[0:c], ss[c:2 * c]
            beta_s = _sigmoid(small)
            gam_s = _move_l(kc['lmat'][0:c, :], neg_a * _softplus(sb))
            yield
            dq = cq * lax.rsqrt(ssq + EPS) * (DN_DK ** -0.5)
            dk = ck * lax.rsqrt(ssk + EPS)
            o, st_new = yield from _deltanet_chunk(dq, dk, cv, beta_s, gam_s,
                                                   getter('dn', lambda: sdn_ref[b]), kc, c)
            boxes[('dn', u)] = st_new
            sdn_ref[b] = st_new
            boxes[('out', u, 0)] = o

        def hg():
            zf = zc(Z_HGF, GROUP_W)
            lsz = _log_sigmoid(zf)
            t2 = log_1mlb + lsz
            mx = jnp.maximum(log_lb, t2)
            log_f = mx + jnp.log(jnp.exp(log_lb - mx) + jnp.exp(t2 - mx))
            key_hg = (1.0 - lb) * _sigmoid(-zf)
            o, st_new = yield from _gla_chunk(_silu(zc(Z_HGQ, GROUP_W)), key_hg, zc(Z_HGI, GROUP_W), log_f,
                                              getter('hg', lambda: shg_ref[b]), kc, c, True)
            boxes[('hg', u)] = st_new
            shg_ref[b] = st_new
            boxes[('out', u, 1)] = o

        def gla():
            small = zc(Z_SMALL, LANES)
            g_gla = _log_sigmoid(_dot(small, wup_ref[...]) + bup_ref[...]) * (1.0 / GLA_TAU)
            yield
            o, st_new = yield from _gla_chunk(zc(Z_GLAQ, GLA_W) * (GLA_DK ** -0.5), zc(Z_GLAK, GLA_W),
                                              zc(Z_GLAV, GROUP_W), g_gla,
                                              getter('gla', lambda: sgla_ref[b]), kc, c, False)
            boxes[('gla', u)] = st_new
            sgla_ref[b] = st_new
            boxes[('out', u, 2)] = o

        def ml():
            sb = zc(Z_SMALL, LANES) + sp[0:1, :]
            ig_s = pltpu.roll(sb, S_MLF - S_MLI, axis=1)
            lf_s = _log_sigmoid(sb)

            def put_m(m_new):
                boxes[('ml_m', u)] = m_new
                mml_ref[b] = m_new

            hh, c_new, n_new = yield from _mlstm_chunk(
                zc(Z_MLQ, GROUP_W) * (ML_DK ** -0.5), zc(Z_MLK, GROUP_W), zc(Z_MLV, GROUP_W), ig_s, lf_s,
                getter('ml_m', lambda: mml_ref[b]), put_m,
                getter('ml_cn', lambda: (cml_ref[b], nml_ref[b])), kc, c)
            boxes[('ml_cn', u)] = (c_new, n_new)
            cml_ref[b] = c_new
            nml_ref[b] = n_new
            boxes[('out', u, 3)] = hh

        late = STAGGER * pos
        return [(0, dn()), (late, hg()), (late, gla()), (late, ml())]

    def chunk_body(it, carry):
        boxes = {}
        gens = []
        for u in range(ilv):
            gens += chunk_gens(it, u, boxes)
        _run_interleaved(gens)
        outs = jnp.concatenate([boxes[('out', u, m)] for u in range(ilv) for m in range(4)], axis=0)
        ms = _dot(outs * outs, kc['ones256'][...]) * (1.0 / HEAD_V)
        normed = outs * lax.rsqrt(ms + EPS)
        gate_cols = (Z_DNG, Z_HGG, Z_GLAG, Z_MLO)
        for u in range(ilv):
            rs = pl.ds(pl.multiple_of((it * ilv + u) * c, c), c)
            for m in range(4):
                zg = z_ref[rs, gate_cols[m]:gate_cols[m] + GROUP_W]
                gate = _sigmoid(zg) if m == 3 else _silu(zg)
                r0 = (u * 4 + m) * c
                mix_ref[rs, m * GROUP_W:(m + 1) * GROUP_W] = (normed[r0:r0 + c] * gn[m:m + 1, :] * gate).astype(BF16)
        return carry

    lax.fori_loop(0, nb * nchunk // ilv, chunk_body, 0)

    @pl.when(ti == pl.num_programs(1) - 1)
    def _():
        for _, out_ref, st_ref, sel_ref in mats:
            for b in range(nb):
                for hd in range(N_HEADS):
                    blk = st_ref[b, hd * HEAD_V:(hd + 1) * HEAD_V, :]
                    out_ref[b, hd] = _move_nt(blk, sel_ref[hd]) if st_ref is sdn_ref else _move_nt(sel_ref[hd], blk)

    out = jnp.dot(mix_ref[...], wout_ref[...], preferred_element_type=F32).reshape(nb, tb, D_MODEL)
    y_ref[...] = x + mod[:, 2:3, :] * out


def _const_tables(c):
    pc = N_HEADS * c
    t = np.arange(c)[:, None]
    lane = np.arange(pc)[None, :]
    hs, s = lane // c, lane % c
    masks = [s <= t, s < t, s == t, (s // TRI_BLOCK) == (t // TRI_BLOCK)]
    sz = c // 2
    while sz >= 1:
        masks.append(((s // (2 * sz)) == (t // (2 * sz))) & ((t // sz) % 2 == 1) & ((s // sz) % 2 == 0))
        sz //= 2
    masks = np.stack([np.broadcast_to(m, (c, pc)) for m in masks]).astype(np.float32)
    r = np.arange(c)[None, :]
    tri = (r <= t).astype(np.float32)
    lmat = np.concatenate([tri, tri - tri[(np.arange(c) // 4) * 4 + 1]], axis=0)

    def expand(col0, w):
        j = np.arange(LANES)[:, None]
        l = np.arange(N_HEADS * w)[None, :]
        return (j == col0 + l // w).astype(np.float32)

    e_c = np.stack([expand(S_DNB, c), expand(S_DNA, c), expand(S_MLF, c)])
    e_d = np.stack([expand(S_DNB, HEAD_V), expand(S_DNA, HEAD_V), expand(S_MLF, HEAD_V)])

    def headsel(dk):
        d = np.arange(dk)[None, :, None]
        l = np.arange(N_HEADS * dk)[None, None, :]
        return (l == np.arange(N_HEADS)[:, None, None] * dk + d).astype(np.float32)

    def blk(nr, rg, nl, lg):
        return ((np.arange(nr)[:, None] // rg) == (np.arange(nl)[None, :] // lg)).astype(np.float32)

    tabs = dict(
        masks=jnp.asarray(masks), e_c=jnp.asarray(e_c, BF16), e_d=jnp.asarray(e_d, BF16),
        bd256=jnp.asarray(blk(pc, c, GROUP_W, HEAD_V), BF16), bd128=jnp.asarray(blk(pc, c, GLA_W, GLA_DK), BF16),
        bdp=jnp.asarray(blk(pc, c, pc, c), BF16),
        st256=jnp.asarray(blk(GROUP_W, HEAD_V, GROUP_W, HEAD_V)), st128=jnp.asarray(blk(GROUP_W, HEAD_V, GLA_W, GLA_DK)),
        ie256=jnp.asarray(blk(GROUP_W, HEAD_V, pc, c), BF16), ie128=jnp.asarray(blk(GLA_W, GLA_DK, pc, c), BF16),
        iep=jnp.asarray(blk(pc, c, GROUP_W, HEAD_V), BF16),
        ones256=jnp.asarray(blk(GROUP_W, HEAD_V, GROUP_W, HEAD_V), BF16), lmat=jnp.asarray(lmat, BF16),
        sel64=jnp.asarray(headsel(HEAD_V), BF16), sel32=jnp.asarray(headsel(GLA_DK), BF16))
    order = ['masks', 'e_c', 'e_d', 'bd256', 'bd128', 'bdp', 'st256', 'st128', 'ie256', 'ie128', 'iep', 'ones256',
             'lmat', 'sel64', 'sel32']
    return [tabs[k] for k in order]


def _full_spec(a):
    nd = a.ndim
    return pl.BlockSpec(a.shape, lambda bi, ti, _n=nd: (0,) * _n, pipeline_mode=pl.Buffered(1))


def _mixer_call(x, mod, lw, states, nb, tb):
    bsz, t, _ = x.shape
    c = min(CHUNK, t)
    assert t % tb == 0 and tb % c == 0 and bsz % nb == 0 and c % TRI_BLOCK == 0
    rows = nb * tb
    consts = _const_tables(c)
    params = [lw['g_mix'], lw['w_in'], lw['w_out'], lw['conv_w'], lw['sp'], lw['gn'], lw['lb_logits'], lw['lb_sel'],
              lw['wup'], lw['bup']]
    xspec = pl.BlockSpec((nb, tb, D_MODEL), lambda bi, ti: (bi, ti, 0))

    def bspec(a):
        nd = a.ndim
        return pl.BlockSpec((nb,) + a.shape[1:], lambda bi, ti, _n=nd: (bi,) + (0,) * (_n - 1))

    in_specs = ([xspec, bspec(mod)] + [_full_spec(a) for a in params] + [_full_spec(a) for a in consts]
                + [bspec(s) for s in states])
    out_shape = [jax.ShapeDtypeStruct(x.shape, F32)] + [jax.ShapeDtypeStruct(s.shape, F32) for s in states]
    out_specs = [xspec] + [bspec(s) for s in states]
    scratch = [pltpu.VMEM((rows, N_IN), F32),
               pltpu.VMEM((nb, SUB + tb, DN_CONV_CH), F32),
               pltpu.VMEM((rows, DN_CONV_CH), F32),
               pltpu.VMEM((rows, MIX_W), BF16),
               pltpu.VMEM((nb, GROUP_W, N_HEADS * DN_DK), F32),
               pltpu.VMEM((nb, GROUP_W, N_HEADS * HG_DK), F32),
               pltpu.VMEM((nb, GROUP_W, GLA_W), F32),
               pltpu.VMEM((nb, GROUP_W, N_HEADS * ML_DK), F32)]
    ilv = MAX_INTERLEAVE
    while (nb * (tb // c)) % ilv:
        ilv //= 2
    kern = functools.partial(_mixer_kernel, nb=nb, tb=tb, c=c, ilv=ilv)
    return pl.pallas_call(
        kern, grid=(bsz // nb, t // tb), in_specs=in_specs, out_specs=out_specs, out_shape=out_shape,
        scratch_shapes=scratch, name='mixer',
        compiler_params=pltpu.CompilerParams(dimension_semantics=('arbitrary', 'arbitrary'),
                                             vmem_limit_bytes=VMEM_LIMIT),
    )(x, mod, *params, *consts, *states)


FF_TILE = 256


def _ffn_kernel(x_ref, mod_ref, gffn_ref, wup_ref, wdown_ref, gfin_ref, y_ref, *, nb, tb, final):
    rows = nb * tb
    x = x_ref[...]
    mod = mod_ref[...]
    h = _rmsnorm_rows(x, gffn_ref[...]) * (1.0 + mod[:, 4:5, :]) + mod[:, 3:4, :]
    hb = h.reshape(rows, D_MODEL).astype(BF16)
    acc = jnp.zeros((rows, D_MODEL), F32)
    for j in range(D_FF // FF_TILE):
        gate = jnp.dot(hb, wup_ref[:, j * FF_TILE:(j + 1) * FF_TILE], preferred_element_type=F32)
        up = jnp.dot(hb, wup_ref[:, D_FF + j * FF_TILE:D_FF + (j + 1) * FF_TILE], preferred_element_type=F32)
        act = (_silu(gate) * up).astype(BF16)
        acc = acc + jnp.dot(act, wdown_ref[j * FF_TILE:(j + 1) * FF_TILE, :], preferred_element_type=F32)
    y = x + mod[:, 5:6, :] * acc.reshape(nb, tb, D_MODEL)
    if final:
        y = _rmsnorm_rows(y, gfin_ref[...])
    y_ref[...] = y


def _ffn_call(x, mod, lw, g_final, nb, tb, final):
    bsz, t, _ = x.shape
    xspec = pl.BlockSpec((nb, tb, D_MODEL), lambda bi, ti: (bi, ti, 0))
    mspec = pl.BlockSpec((nb, 6, D_MODEL), lambda bi, ti: (bi, 0, 0))
    params = [lw['g_ffn'], lw['w_up'], lw['w_down'], g_final]
    kern = functools.partial(_ffn_kernel, nb=nb, tb=tb, final=final)
    return pl.pallas_call(
        kern, grid=(bsz // nb, t // tb), in_specs=[xspec, mspec] + [_full_spec(a) for a in params],
        out_specs=xspec, out_shape=jax.ShapeDtypeStruct(x.shape, F32), name='ffn',
        compiler_params=pltpu.CompilerParams(dimension_semantics=('arbitrary', 'arbitrary'),
                                             vmem_limit_bytes=VMEM_LIMIT),
    )(x, mod, *params)


ADA_TILE = 1536


def _ada_kernel(c_ref, w_ref, b_ref, o_ref):
    o_ref[0] = jnp.dot(_silu(c_ref[...]).astype(BF16), w_ref[0], preferred_element_type=F32) + b_ref[0]


def _ada_call(c_all, w_ada, b_ada):
    n = c_all.shape[0]
    nt = 6 * D_MODEL // ADA_TILE
    return pl.pallas_call(
        _ada_kernel, grid=(DEPTH, nt),
        in_specs=[pl.BlockSpec((n, D_MODEL), lambda l, j: (0, 0)),
                  pl.BlockSpec((1, D_MODEL, ADA_TILE), lambda l, j: (l, 0, j)),
                  pl.BlockSpec((1, 1, ADA_TILE), lambda l, j: (l, 0, j))],
        out_specs=pl.BlockSpec((1, n, ADA_TILE), lambda l, j: (l, 0, j)),
        out_shape=jax.ShapeDtypeStruct((DEPTH, n, 6 * D_MODEL), F32), name='ada',
        compiler_params=pltpu.CompilerParams(dimension_semantics=('arbitrary', 'arbitrary')),
    )(c_all, w_ada, b_ada.reshape(DEPTH, 1, 6 * D_MODEL))


_REF_SPLITS = (('dn_qkv', DN_CONV_CH), ('dn_b', N_HEADS), ('dn_a', N_HEADS), ('dn_g', GROUP_W),
               ('hg_q', GROUP_W), ('hg_f', GROUP_W), ('hg_i', GROUP_W), ('hg_g', GROUP_W),
               ('gla_q', GLA_W), ('gla_k', GLA_W), ('gla_v', GROUP_W), ('gla_r', GLA_RANK), ('gla_g', GROUP_W),
               ('ml_q', GROUP_W), ('ml_k', GROUP_W), ('ml_v', GROUP_W), ('ml_i', N_HEADS), ('ml_f', N_HEADS),
               ('ml_o', GROUP_W))


def _permute_w_in(w):
    d = w.shape[0]
    cols, off = {}, 0
    for name, n in _REF_SPLITS:
        cols[name] = w[:, off:off + n]
        off += n
    assert off == w.shape[1]
    pieces, lane = [], 0
    for name, start in (('dn_b', S_DNB), ('gla_r', S_GLAR), ('dn_a', S_DNA), ('ml_i', S_MLI), ('ml_f', S_MLF)):
        pieces += [jnp.zeros((d, start - lane), w.dtype), cols[name]]
        lane = start + cols[name].shape[1]
    small = jnp.concatenate(pieces + [jnp.zeros((d, LANES - lane), w.dtype)], axis=1)
    order = ('dn_qkv', 'dn_g', 'hg_q', 'hg_f', 'hg_i', 'hg_g', 'gla_q', 'gla_k', 'gla_v', 'gla_g',
             'ml_q', 'ml_k', 'ml_v', 'ml_o')
    out = jnp.concatenate([cols[k] for k in order] + [small], axis=1)
    assert out.shape[1] == N_IN
    return out


def _lane_row(pairs, width=LANES):
    row = jnp.zeros((width,), F32)
    for off, val in pairs:
        row = row.at[off:off + val.shape[0]].set(val.astype(F32))
    return row


def _layer_weights(p, l):
    sp = jnp.zeros((SUB, LANES), F32)
    sp = sp.at[0].set(_lane_row([(S_DNA, p['dn_dt_bias'][l]), (S_MLI, p['ml_i_bias'][l]), (S_MLF, p['ml_f_bias'][l])]))
    sp = sp.at[1].set(_lane_row([(S_DNA, p['dn_a_log'][l])]))
    gn = jnp.stack([jnp.tile(p[k][l].astype(F32), N_HEADS) for k in ('dn_norm_g', 'hg_norm_g', 'gla_norm_g', 'ml_norm_g')])
    wup = jnp.zeros((LANES, GLA_W), F32).at[S_GLAR:S_GLAR + GLA_RANK].set(p['gla_w_up'][l]).astype(BF16)
    lb_sel = (jnp.arange(DEPTH) >= 1) & (jnp.arange(DEPTH) <= l)
    return dict(
        g_mix=p['g_mix'][l].reshape(1, D_MODEL), g_ffn=p['g_ffn'][l].reshape(1, D_MODEL),
        w_in=_permute_w_in(p['w_in'][l]).astype(BF16), w_out=p['w_out'][l].astype(BF16),
        conv_w=p['dn_conv_w'][l], sp=sp, gn=gn, lb_logits=p['hg_lb_logits'].astype(F32),
        lb_sel=lb_sel.astype(F32).reshape(DEPTH, 1), wup=wup, bup=p['gla_b_up'][l].reshape(1, GLA_W).astype(F32),
        w_up=p['w_up'][l].astype(BF16), w_down=p['w_down'][l].astype(BF16))


def _trunk(x, mods, states, lws, g_final, nb, tb):
    new_states = []
    for l in range(DEPTH):
        outs = _mixer_call(x, mods[l], lws[l], states[l], nb, tb)
        x = outs[0]
        new_states.append(outs[1:])
        x = _ffn_call(x, mods[l], lws[l], g_final, *_tiling(x.shape[0], x.shape[1], FFN_ROWS, FFN_ROWS), l == DEPTH - 1)
    return x, new_states


def _pack_states(conv, s_dn, s_hg, s_gla, c_ml, n_ml, m_ml, l):
    b = conv.shape[1]
    m_row = jnp.zeros((b, 1, LANES), F32).at[:, 0, S_MLF:S_MLF + N_HEADS].set(m_ml[l].astype(F32))
    return (conv[l].astype(F32), s_dn[l].astype(F32), s_hg[l].astype(F32), s_gla[l].astype(F32),
            c_ml[l].astype(F32), n_ml[l].astype(F32).reshape(b, 1, N_HEADS * ML_DK), m_row)


def _unpack_states(sts):
    conv, s_dn, s_hg, s_gla, c_ml = (jnp.stack([s[i] for s in sts]) for i in range(5))
    n_ml = jnp.stack([s[5].reshape(s[5].shape[0], N_HEADS, ML_DK) for s in sts])
    m_ml = jnp.stack([s[6][:, 0, S_MLF:S_MLF + N_HEADS] for s in sts])
    return conv, s_dn, s_hg, s_gla, c_ml, n_ml, m_ml


def _zero_states(b):
    z = lambda *s: jnp.zeros(s, F32)
    return (z(b, DN_CONV - 1, DN_CONV_CH), z(b, N_HEADS, DN_DK, HEAD_V), z(b, N_HEADS, HG_DK, HEAD_V),
            z(b, N_HEADS, GLA_DK, HEAD_V), z(b, N_HEADS, ML_DK, HEAD_V), z(b, 1, N_HEADS * ML_DK), z(b, 1, LANES))


def _tiling(bsz, t, rows, tokens):
    tb = min(t, tokens)
    nb = max(1, min(bsz, rows // tb))
    while bsz % nb:
        nb -= 1
    return nb, tb


def kernel(x_prompt, x_sample, c_prompt, c_sample, cache_dn_conv, state_dn, state_hgrn, state_gla, state_mlstm_c, state_mlstm_n, state_mlstm_m, w_ada, b_ada, g_mix, g_ffn, w_in, dn_conv_w, dn_a_log, dn_dt_bias, dn_norm_g, hg_lb_logits, hg_norm_g, gla_w_up, gla_b_up, gla_norm_g, ml_i_bias, ml_f_bias, ml_norm_g, w_out, w_up, w_down, g_final):
    p = dict(g_mix=g_mix, g_ffn=g_ffn, w_in=w_in, dn_conv_w=dn_conv_w, dn_a_log=dn_a_log, dn_dt_bias=dn_dt_bias,
             dn_norm_g=dn_norm_g, hg_lb_logits=hg_lb_logits, hg_norm_g=hg_norm_g, gla_w_up=gla_w_up,
             gla_b_up=gla_b_up, gla_norm_g=gla_norm_g, ml_i_bias=ml_i_bias, ml_f_bias=ml_f_bias,
             ml_norm_g=ml_norm_g, w_out=w_out, w_up=w_up, w_down=w_down)
    lws = [_layer_weights(p, l) for l in range(DEPTH)]
    gfin = g_final.reshape(1, D_MODEL).astype(F32)
    bp, bs = x_prompt.shape[0], x_sample.shape[0]
    mod = _ada_call(jnp.concatenate([c_prompt, c_sample], axis=0).astype(F32), w_ada.astype(BF16),
                    b_ada.astype(F32)).reshape(DEPTH, bp + bs, 6, D_MODEL)

    outs = []
    raw = (cache_dn_conv, state_dn, state_hgrn, state_gla, state_mlstm_c, state_mlstm_n, state_mlstm_m)
    for x, lo, hi, states in ((x_prompt, 0, bp, [_zero_states(bp)] * DEPTH),
                              (x_sample, bp, bp + bs, [_pack_states(*raw, l) for l in range(DEPTH)])):
        nb, tb = _tiling(x.shape[0], x.shape[1], MIXER_ROWS, MIXER_TOKENS)
        y, new = _trunk(x.astype(F32), [mod[l, lo:hi] for l in range(DEPTH)], states, lws, gfin, nb, tb)
        outs.append((y, _unpack_states(new)))
    (y_p, st_p), (y_s, st_s) = outs
    return (y_p, y_s) + tuple(st_p) + tuple(st_s)
```

```python
import functools

import numpy as np
import jax
import jax.numpy as jnp
from jax import lax
from jax.experimental import pallas as pl
from jax.experimental.pallas import tpu as pltpu

F32 = jnp.float32
BF16 = jnp.bfloat16

D_MODEL = 1024
DEPTH = 2
CHUNK = 64
N_HEADS = 4
HEAD_V = 64
GROUP_W = N_HEADS * HEAD_V
MIX_W = 4 * GROUP_W
DN_DK = 64
DN_CONV = 4
DN_CONV_CH = 3 * GROUP_W
HG_DK = 64
GLA_DK = 32
GLA_W = N_HEADS * GLA_DK
GLA_RANK = 16
GLA_TAU = 16.0
ML_DK = 64
D_FF = 2816
EPS = 1e-6

Z_DNQKV = 0
Z_SMALL = 768
Z_HGF = 896
Z_HGQ = 1152
Z_DNG = 1408
Z_HGG = 1664
Z_GLAG = 1920
Z_MLO = 2176
N_EARLY = 2432
Z_HGI = 2432
Z_GLAQ = 2688
Z_GLAK = 2816
Z_GLAV = 2944
Z_MLQ = 3200
Z_MLK = 3456
Z_MLV = 3712
N_IN = 3968
S_DNB = 0
S_GLAR = 16
S_DNA = 32
S_MLI = 64
S_MLF = 96

SUB = 8
LANES = 128
TRI_BLOCK = 16
VMEM_LIMIT = 56 * 1024 * 1024

M_INCL, M_STRICT, M_DIAG, M_SAMEBLK = 0, 1, 2, 3
M_LEVEL0 = 4
LOG2E = 1.4426950408889634
MIXER_ROWS = 256
MIXER_TOKENS = 256
FFN_ROWS = 1024
STAGGER = 3
MAX_INTERLEAVE = 4


def _sigmoid(x):
    return 1.0 / (1.0 + jnp.exp(-x))


def _silu(x):
    return x * _sigmoid(x)


def _log_sigmoid(x):
    return jnp.minimum(x, 0.0) - jnp.log1p(jnp.exp(-jnp.abs(x)))


def _softplus(x):
    return jnp.maximum(x, 0.0) + jnp.log1p(jnp.exp(-jnp.abs(x)))


def _dot(a, b):
    return jnp.dot(a.astype(BF16), b.astype(BF16), preferred_element_type=F32)


def _dot_nt(a, b):
    return lax.dot_general(a.astype(BF16), b.astype(BF16), (((1,), (1,)), ((), ())),
                           preferred_element_type=F32)


def _split3(x):
    x1 = x.astype(BF16)
    r = x - x1.astype(F32)
    x2 = r.astype(BF16)
    r = r - x2.astype(F32)
    return x1, x2, r.astype(BF16)


def _move_rows(xs, sel):
    parts = [_split3(x) for x in xs]
    y = jnp.dot(jnp.concatenate([p[i] for i in range(3) for p in parts], axis=0), sel, preferred_element_type=F32)
    n = sum(x.shape[0] for x in xs)
    outs = []
    off = 0
    for x in xs:
        r = x.shape[0]
        outs.append(y[off:off + r] + (y[n + off:n + off + r] + y[2 * n + off:2 * n + off + r]))
        off += r
    return outs


def _move_r(x, sel):
    return _move_rows([x], sel)[0]


def _move_l(sel, x):
    x1, x2, x3 = _split3(x)
    d = lambda a: jnp.dot(sel, a, preferred_element_type=F32)
    return d(x1) + (d(x2) + d(x3))


def _move_tn(x, sel):
    x1, x2, x3 = _split3(x)
    d = lambda a: lax.dot_general(a, sel, (((0,), (0,)), ((), ())), preferred_element_type=F32)
    return d(x1) + (d(x2) + d(x3))


def _move_nt(a, b):
    nt = lambda x, y: lax.dot_general(x, y, (((1,), (1,)), ((), ())), preferred_element_type=F32)
    if a.dtype == BF16:
        d = lambda p: nt(a, p)
        x1, x2, x3 = _split3(b)
    else:
        d = lambda p: nt(p, b)
        x1, x2, x3 = _split3(a)
    return d(x1) + (d(x2) + d(x3))


def _bd(x, mask):
    xb = x.astype(BF16)
    return jnp.concatenate([xb] * N_HEADS, axis=0) * mask


def _scan0(x, op, fill):
    n = x.shape[0]
    row = lax.broadcasted_iota(jnp.int32, x.shape, 0)
    sh = 1
    while sh < n:
        r = pltpu.roll(x, sh, axis=0)
        x = op(x, jnp.where(row >= sh, r, fill))
        sh *= 2
    return x


def _rowform(xe, diag):
    return jnp.sum(xe * diag, axis=0, keepdims=True)


def _rmsnorm_rows(x, g):
    return x * lax.rsqrt(jnp.mean(x * x, axis=-1, keepdims=True) + EPS) * g


def _run_interleaved(gens):
    live = list(gens)
    rnd = 0
    while live:
        alive = []
        for start, g in live:
            if rnd >= start:
                try:
                    next(g)
                except StopIteration:
                    continue
            alive.append((start, g))
        live = alive
        rnd += 1


def _await(boxes, key):
    while key not in boxes:
        yield
    return boxes[key]


def _tri_solve(mm, rhs, masks, bdp, bdr, c):
    assert c // TRI_BLOCK <= 4
    eye = masks[M_DIAG]
    mul = lambda a, b: _dot(a, _bd(b, bdp))
    app = lambda a, r: _dot(a, _bd(r, bdr))
    md = mm * masks[M_SAMEBLK]
    mo = mm - md
    p2 = mul(md, md)
    yield
    d = eye - md
    d = d + mul(d, p2)
    p4 = mul(p2, p2)
    yield
    d = d + mul(d, p4)
    p8 = mul(p4, p4)
    yield
    d = d + mul(d, p8)
    yield
    n = mul(d, mo)
    ys = [app(d, r) for r in rhs]
    yield
    zs = [y - app(n, y) for y in ys]
    if c // TRI_BLOCK <= 2:
        yield
        return zs
    n2 = mul(n, n)
    yield
    ws = [z + app(n2, z) for z in zs]
    yield
    return ws


def _deltanet_chunk(q, k, v, beta_s, gam_s, get_state, kc, c):
    masks = kc['masks']
    gam_e = _move_r(gam_s, kc['e_c'][1])
    beta_e = _move_r(beta_s, kc['e_c'][0])
    if c == HEAD_V:
        gam_d, beta_d = gam_e, beta_e
    else:
        gam_d = _move_r(gam_s, kc['e_d'][1])
        beta_d = _move_r(beta_s, kc['e_d'][0])
    kq = _dot_nt(jnp.concatenate([k, q], axis=0), _bd(k, kc['bd256'][...]))
    kk, qk = kq[0:c], kq[c:2 * c]
    yield
    gam_r = _rowform(gam_e, masks[M_DIAG])
    dec = jnp.exp(jnp.minimum(gam_e - gam_r, 0.0))
    mm = beta_e * kk * dec * masks[M_STRICT]
    eg = jnp.exp(gam_d)
    w, u0 = yield from _tri_solve(mm, [beta_d * eg * k, beta_d * v], masks, kc['bdp'][...], kc['bd256'][...], c)
    gl = gam_d[c - 1:c, :]
    kdec_t = (k * jnp.exp(gl - gam_d)).T
    st = yield from get_state()
    wq = _dot(jnp.concatenate([w, q * eg], axis=0), st)
    u = u0 - wq[0:c]
    qs = wq[c:2 * c]
    yield
    st_new = jnp.exp(gl) * st + kc['st256'][...] * _dot(kdec_t, u)
    o = qs + _dot(qk * dec * masks[M_INCL], _bd(u, kc['bd256'][...]))
    return o, st_new


def _block_ref(b, sz):
    c, w = b.shape
    g3 = b.reshape(c // (2 * sz), 2 * sz, w)
    return jnp.broadcast_to(g3[:, sz - 1:sz, :], g3.shape).reshape(c, w)


def _gla_chunk(q, k, v, g2, get_state, kc, c, wide):
    masks = kc['masks']
    bdk = kc['bd256'][...] if wide else kc['bd128'][...]
    ie = kc['ie256'][...] if wide else kc['ie128'][...]
    stm = kc['st256'][...] if wide else kc['st128'][...]
    v_t = v.T
    bd2 = _move_l(kc['lmat'][...], g2)
    b = bd2[0:c, :]
    yield
    row = lax.broadcasted_iota(jnp.int32, b.shape, 0)
    x1 = jnp.where(jnp.bitwise_and(row, 1) == 1, q * pltpu.roll(k, 1, axis=0) * jnp.exp2(g2), 0.0)
    dd = _dot(jnp.concatenate([q * k, x1], axis=0), ie)
    nlvl = c.bit_length() - 1
    attn = dd[0:c] * masks[M_DIAG] + dd[c:2 * c] * masks[M_LEVEL0 + nlvl - 1]
    lvl = 0
    sz = c // 2
    while sz >= 2:
        d = b - _block_ref(b, sz) if sz >= 4 else bd2[c:2 * c, :]
        e = jnp.exp2(jnp.minimum(d, -d))
        attn = attn + _dot_nt(q * e, _bd(k * e, bdk)) * masks[M_LEVEL0 + lvl]
        yield
        sz //= 2
        lvl += 1
    last = b[c - 1:c, :]
    upd = stm * _dot(v_t, k * jnp.exp2(last - b))
    o = _dot(attn, _bd(v, kc['bd256'][...]))
    yield
    st = yield from get_state()
    st_new = jnp.exp2(last) * st + upd
    o = o + _dot_nt(q * jnp.exp2(b), st)
    return o, st_new


def _mlstm_chunk(q, k, v, ig_s, lf_s, get_m, put_m, get_cn, kc, c):
    masks = kc['masks']
    fcum = _move_l(kc['lmat'][0:c, :], lf_s)
    a = ig_s - fcum
    imax = fcum + _scan0(a, jnp.maximum, -jnp.inf)
    fl = fcum[c - 1:c, :]
    lw = fl - fcum + ig_s
    lw_max = jnp.max(lw, axis=0, keepdims=True)
    kb = _bd(k, kc['bd256'][...])
    qk = _dot_nt(q, kb)
    a_e = _move_r(a, kc['e_c'][2])
    a_r = _rowform(a_e, masks[M_DIAG])
    yield
    m_row = yield from get_m()
    m_new = jnp.maximum(fl + m_row, lw_max)
    put_m(m_new)
    mt = jnp.maximum(fcum + m_row, imax)
    rows = lambda r: jnp.broadcast_to(r, (2 * SUB, LANES))
    consts = [rows(fl + m_row - m_new), rows(m_row), rows(fl - m_new)]
    x1 = fcum - mt
    if c == HEAD_V:
        x1_e, lfl, lwc, m_d, sh_d = _move_rows([x1, -mt] + consts, kc['e_d'][2])
        x1_d, a_d = x1_e, a_e
    else:
        x1_e = _move_r(x1, kc['e_c'][2])
        x1_d, lfl, a_d, lwc, m_d, sh_d = _move_rows([x1, -mt, a] + consts, kc['e_d'][2])
    w_inter = jnp.exp(x1_d + m_d[0:1, :])
    floor = jnp.exp(lfl)
    ws = jnp.exp(a_d + sh_d[0:1, :])
    wc = jnp.exp(lwc)[0:1, :]
    wsv_t = (ws * v).T
    yield
    w_intra = jnp.exp(jnp.minimum(x1_e + a_r, 0.0)) * masks[M_INCL] * qk
    num = _dot(w_intra, _bd(v, kc['bd256'][...]))
    den = _dot(w_intra, kc['iep'][...])
    upd = kc['st256'][...] * _dot(wsv_t, k)
    n_upd = jnp.sum(ws * k, axis=0, keepdims=True)
    yield
    ct, n_row = yield from get_cn()
    ct_new = wc * ct + upd
    n_new = wc * n_row + n_upd
    num = num + w_inter * _dot_nt(q, ct)
    den = den + w_inter * _dot(q * n_row, kc['ones256'][...])
    hh = num / jnp.maximum(jnp.abs(den), floor)
    return hh, ct_new, n_new


def _mixer_kernel(x_ref, mod_ref, gmix_ref, win_ref, wout_ref, convw_ref, sp_ref, gn_ref, lbl_ref, lbs_ref,
                  wup_ref, bup_ref,
                  masks_ref, ec_ref, ed_ref, bd256_ref, bd128_ref, bdp_ref, st256_ref, st128_ref,
                  ie256_ref, ie128_ref, iep_ref, ones256_ref, lmat_ref, sel64_ref, sel32_ref,
                  conv0_ref, sdn0_ref, shg0_ref, sgla0_ref, c0_ref, n0_ref, m0_ref,
                  y_ref, convo_ref, sdno_ref, shgo_ref, sglao_ref, cmlo_ref, nml_ref, mml_ref,
                  z_ref, xp_ref, qkv_ref, mix_ref, hgk_ref, sdn_ref, shg_ref, sgla_ref, cml_ref, *, nb, tb, c, ilv):
    ti = pl.program_id(1)
    rows = nb * tb
    nchunk = tb // c
    mats = ((sdn0_ref, sdno_ref, sdn_ref, sel64_ref), (shg0_ref, shgo_ref, shg_ref, sel64_ref),
            (sgla0_ref, sglao_ref, sgla_ref, sel32_ref), (c0_ref, cmlo_ref, cml_ref, sel64_ref))

    @pl.when(ti == 0)
    def _():
        convo_ref[...] = conv0_ref[...]
        nml_ref[...] = n0_ref[...]
        mml_ref[...] = m0_ref[...]
        for raw_ref, _, st_ref, sel_ref in mats:
            place = _move_r if st_ref is sdn_ref else _move_tn
            for b in range(nb):
                st_ref[b] = jnp.concatenate([place(raw_ref[b, hd], sel_ref[hd]) for hd in range(N_HEADS)], axis=0)

    x = x_ref[...]
    mod = mod_ref[...]
    h = _rmsnorm_rows(x, gmix_ref[...]) * (1.0 + mod[:, 1:2, :]) + mod[:, 0:1, :]
    hb = h.reshape(rows, D_MODEL).astype(BF16)
    z_ref[:, 0:N_EARLY] = jnp.dot(hb, win_ref[:, 0:N_EARLY], preferred_element_type=F32)
    z_ref[:, N_EARLY:N_IN] = jnp.dot(hb, win_ref[:, N_EARLY:N_IN], preferred_element_type=F32)

    convw = convw_ref[...]
    for b in range(nb):
        xp_ref[b, SUB - (DN_CONV - 1):SUB, :] = convo_ref[b]
        xp_ref[b, SUB:SUB + tb, :] = z_ref[b * tb:(b + 1) * tb, Z_DNQKV:Z_DNQKV + DN_CONV_CH]
        acc = xp_ref[b, SUB - 3:SUB - 3 + tb, :] * convw[0:1, :]
        for j in range(1, DN_CONV):
            acc = acc + xp_ref[b, SUB - 3 + j:SUB - 3 + j + tb, :] * convw[j:j + 1, :]
        qkv_ref[b * tb:(b + 1) * tb, :] = _silu(acc)
        convo_ref[b] = xp_ref[b, SUB + tb - (DN_CONV - 1):SUB + tb, :]

    sp = sp_ref[...]
    gn = gn_ref[...]
    lbl = lbl_ref[...]
    lbs = lbs_ref[...]
    lbe = jnp.exp(lbl - jnp.max(lbl, axis=0, keepdims=True))
    lb = jnp.sum(lbs * (lbe / jnp.sum(lbe, axis=0, keepdims=True)), axis=0, keepdims=True)
    log_lb = jnp.log(lb)
    log_1mlb = jnp.log1p(-lb)
    neg_a = -jnp.exp(sp[1:2, :])

    zf = z_ref[:, Z_HGF:Z_HGF + GROUP_W]
    t2 = log_1mlb + _log_sigmoid(zf)
    mx = jnp.maximum(log_lb, t2)
    z_ref[:, Z_HGF:Z_HGF + GROUP_W] = (mx + jnp.log(jnp.exp(log_lb - mx) + jnp.exp(t2 - mx))) * LOG2E
    hgk_ref[...] = (1.0 - lb) * _sigmoid(-zf)
    z_ref[:, Z_HGQ:Z_HGQ + GROUP_W] = _silu(z_ref[:, Z_HGQ:Z_HGQ + GROUP_W])
    for col in (Z_DNG, Z_HGG, Z_GLAG):
        z_ref[:, col:col + GROUP_W] = _silu(z_ref[:, col:col + GROUP_W])
    z_ref[:, Z_MLO:Z_MLO + GROUP_W] = _sigmoid(z_ref[:, Z_MLO:Z_MLO + GROUP_W])

    kc = dict(masks=masks_ref, e_c=ec_ref, e_d=ed_ref, bd256=bd256_ref, bd128=bd128_ref, bdp=bdp_ref,
              st256=st256_ref, st128=st128_ref, ie256=ie256_ref, ie128=ie128_ref, iep=iep_ref,
              ones256=ones256_ref, lmat=lmat_ref)
    run = min(ilv, nchunk)
    assert ilv % run == 0 and nchunk % run == 0

    def chunk_gens(it, u, boxes):
        i = it * ilv + u
        pos = u % run
        if nb == 1:
            b = 0
        elif ilv % nchunk == 0:
            b = it * (ilv // nchunk) + u // nchunk
        else:
            b = i // nchunk
        rs = pl.ds(pl.multiple_of(i * c, c), c)
        zc = lambda off, w: z_ref[rs, off:off + w]

        def getter(key, read):
            def get():
                if pos > 0:
                    return (yield from _await(boxes, (key, u - 1)))
                return read()
                yield
            return get

        def dn():
            small = zc(Z_SMALL, LANES)
            sb = small + sp[0:1, :]
            qkv = qkv_ref[rs, :]
            cq, ck, cv = qkv[:, 0:GROUP_W], qkv[:, GROUP_W:2 * GROUP_W], qkv[:, 2 * GROUP_W:3 * GROUP_W]
            ss = _dot(jnp.concatenate([cq * cq, ck * ck], axis=0), kc['ones256'][...])
            ssq, ssk = ss[0:c], ss[c:2 * c]
            beta_s = _sigmoid(small)
            gam_s = _move_l(kc['lmat'][0:c, :], neg_a * _softplus(sb))
            yield
            dq = cq * lax.rsqrt(ssq + EPS) * (DN_DK ** -0.5)
            dk = ck * lax.rsqrt(ssk + EPS)
            o, st_new = yield from _deltanet_chunk(dq, dk, cv, beta_s, gam_s,
                                                   getter('dn', lambda: sdn_ref[b]), kc, c)
            boxes[('dn', u)] = st_new
            sdn_ref[b] = st_new
            boxes[('out', u, 0)] = o

        def hg():
            o, st_new = yield from _gla_chunk(zc(Z_HGQ, GROUP_W), hgk_ref[rs, :], zc(Z_HGI, GROUP_W),
                                              zc(Z_HGF, GROUP_W), getter('hg', lambda: shg_ref[b]), kc, c, True)
            boxes[('hg', u)] = st_new
            shg_ref[b] = st_new
            boxes[('out', u, 1)] = o

        def gla():
            small = zc(Z_SMALL, LANES)
            g_gla = _log_sigmoid(_dot(small, wup_ref[...]) + bup_ref[...]) * (LOG2E / GLA_TAU)
            yield
            o, st_new = yield from _gla_chunk(zc(Z_GLAQ, GLA_W) * (GLA_DK ** -0.5), zc(Z_GLAK, GLA_W),
                                              zc(Z_GLAV, GROUP_W), g_gla,
                                              getter('gla', lambda: sgla_ref[b]), kc, c, False)
            boxes[('gla', u)] = st_new
            sgla_ref[b] = st_new
            boxes[('out', u, 2)] = o

        def ml():
            sb = zc(Z_SMALL, LANES) + sp[0:1, :]
            ig_s = pltpu.roll(sb, S_MLF - S_MLI, axis=1)
            lf_s = _log_sigmoid(sb)

            def put_m(m_new):
                boxes[('ml_m', u)] = m_new
                mml_ref[b] = m_new

            hh, c_new, n_new = yield from _mlstm_chunk(
                zc(Z_MLQ, GROUP_W) * (ML_DK ** -0.5), zc(Z_MLK, GROUP_W), zc(Z_MLV, GROUP_W), ig_s, lf_s,
                getter('ml_m', lambda: mml_ref[b]), put_m,
                getter('ml_cn', lambda: (cml_ref[b], nml_ref[b])), kc, c)
            boxes[('ml_cn', u)] = (c_new, n_new)
            cml_ref[b] = c_new
            nml_ref[b] = n_new
            boxes[('out', u, 3)] = hh

        late = STAGGER * pos
        return [(0, dn()), (late, hg()), (late, gla()), (late, ml())]

    def chunk_body(it, carry):
        boxes = {}
        gens = []
        for u in range(ilv):
            gens += chunk_gens(it, u, boxes)
        _run_interleaved(gens)
        outs = jnp.concatenate([boxes[('out', u, m)] for u in range(ilv) for m in range(4)], axis=0)
        ms = _dot(outs * outs, kc['ones256'][...]) * (1.0 / HEAD_V)
        normed = outs * lax.rsqrt(ms + EPS)
        gate_cols = (Z_DNG, Z_HGG, Z_GLAG, Z_MLO)
        for u in range(ilv):
            rs = pl.ds(pl.multiple_of((it * ilv + u) * c, c), c)
            for m in range(4):
                gate = z_ref[rs, gate_cols[m]:gate_cols[m] + GROUP_W]
                r0 = (u * 4 + m) * c
                mix_ref[rs, m * GROUP_W:(m + 1) * GROUP_W] = (normed[r0:r0 + c] * gn[m:m + 1, :] * gate).astype(BF16)
        return carry

    lax.fori_loop(0, nb * nchunk // ilv, chunk_body, 0)

    @pl.when(ti == pl.num_programs(1) - 1)
    def _():
        for _, out_ref, st_ref, sel_ref in mats:
            for b in range(nb):
                for hd in range(N_HEADS):
                    blk = st_ref[b, hd * HEAD_V:(hd + 1) * HEAD_V, :]
                    out_ref[b, hd] = _move_nt(blk, sel_ref[hd]) if st_ref is sdn_ref else _move_nt(sel_ref[hd], blk)

    out = jnp.dot(mix_ref[...], wout_ref[...], preferred_element_type=F32).reshape(nb, tb, D_MODEL)
    y_ref[...] = x + mod[:, 2:3, :] * out


def _const_tables(c):
    pc = N_HEADS * c
    t = np.arange(c)[:, None]
    lane = np.arange(pc)[None, :]
    hs, s = lane // c, lane % c
    masks = [s <= t, s < t, s == t, (s // TRI_BLOCK) == (t // TRI_BLOCK)]
    sz = c // 2
    while sz >= 1:
        masks.append(((s // (2 * sz)) == (t // (2 * sz))) & ((t // sz) % 2 == 1) & ((s // sz) % 2 == 0))
        sz //= 2
    masks = np.stack([np.broadcast_to(m, (c, pc)) for m in masks]).astype(np.float32)
    r = np.arange(c)[None, :]
    tri = (r <= t).astype(np.float32)
    lmat = np.concatenate([tri, tri - tri[(np.arange(c) // 4) * 4 + 1]], axis=0)

    def expand(col0, w):
        j = np.arange(LANES)[:, None]
        l = np.arange(N_HEADS * w)[None, :]
        return (j == col0 + l // w).astype(np.float32)

    e_c = np.stack([expand(S_DNB, c), expand(S_DNA, c), expand(S_MLF, c)])
    e_d = np.stack([expand(S_DNB, HEAD_V), expand(S_DNA, HEAD_V), expand(S_MLF, HEAD_V)])

    def headsel(dk):
        d = np.arange(dk)[None, :, None]
        l = np.arange(N_HEADS * dk)[None, None, :]
        return (l == np.arange(N_HEADS)[:, None, None] * dk + d).astype(np.float32)

    def blk(nr, rg, nl, lg):
        return ((np.arange(nr)[:, None] // rg) == (np.arange(nl)[None, :] // lg)).astype(np.float32)

    tabs = dict(
        masks=jnp.asarray(masks), e_c=jnp.asarray(e_c, BF16), e_d=jnp.asarray(e_d, BF16),
        bd256=jnp.asarray(blk(pc, c, GROUP_W, HEAD_V), BF16), bd128=jnp.asarray(blk(pc, c, GLA_W, GLA_DK), BF16),
        bdp=jnp.asarray(blk(pc, c, pc, c), BF16),
        st256=jnp.asarray(blk(GROUP_W, HEAD_V, GROUP_W, HEAD_V)), st128=jnp.asarray(blk(GROUP_W, HEAD_V, GLA_W, GLA_DK)),
        ie256=jnp.asarray(blk(GROUP_W, HEAD_V, pc, c), BF16), ie128=jnp.asarray(blk(GLA_W, GLA_DK, pc, c), BF16),
        iep=jnp.asarray(blk(pc, c, GROUP_W, HEAD_V), BF16),
        ones256=jnp.asarray(blk(GROUP_W, HEAD_V, GROUP_W, HEAD_V), BF16), lmat=jnp.asarray(lmat, BF16),
        sel64=jnp.asarray(headsel(HEAD_V), BF16), sel32=jnp.asarray(headsel(GLA_DK), BF16))
    order = ['masks', 'e_c', 'e_d', 'bd256', 'bd128', 'bdp', 'st256', 'st128', 'ie256', 'ie128', 'iep', 'ones256',
             'lmat', 'sel64', 'sel32']
    return [tabs[k] for k in order]


def _full_spec(a):
    nd = a.ndim
    return pl.BlockSpec(a.shape, lambda bi, ti, _n=nd: (0,) * _n, pipeline_mode=pl.Buffered(1))


def _mixer_call(x, mod, lw, states, nb, tb):
    bsz, t, _ = x.shape
    c = min(CHUNK, t)
    assert t % tb == 0 and tb % c == 0 and bsz % nb == 0 and c % TRI_BLOCK == 0
    rows = nb * tb
    consts = _const_tables(c)
    params = [lw['g_mix'], lw['w_in'], lw['w_out'], lw['conv_w'], lw['sp'], lw['gn'], lw['lb_logits'], lw['lb_sel'],
              lw['wup'], lw['bup']]
    xspec = pl.BlockSpec((nb, tb, D_MODEL), lambda bi, ti: (bi, ti, 0))

    def bspec(a):
        nd = a.ndim
        return pl.BlockSpec((nb,) + a.shape[1:], lambda bi, ti, _n=nd: (bi,) + (0,) * (_n - 1))

    in_specs = ([xspec, bspec(mod)] + [_full_spec(a) for a in params] + [_full_spec(a) for a in consts]
                + [bspec(s) for s in states])
    out_shape = [jax.ShapeDtypeStruct(x.shape, F32)] + [jax.ShapeDtypeStruct(s.shape, F32) for s in states]
    out_specs = [xspec] + [bspec(s) for s in states]
    scratch = [pltpu.VMEM((rows, N_IN), F32),
               pltpu.VMEM((nb, SUB + tb, DN_CONV_CH), F32),
               pltpu.VMEM((rows, DN_CONV_CH), F32),
               pltpu.VMEM((rows, MIX_W), BF16),
               pltpu.VMEM((rows, N_HEADS * HG_DK), F32),
               pltpu.VMEM((nb, GROUP_W, N_HEADS * DN_DK), F32),
               pltpu.VMEM((nb, GROUP_W, N_HEADS * HG_DK), F32),
               pltpu.VMEM((nb, GROUP_W, GLA_W), F32),
               pltpu.VMEM((nb, GROUP_W, N_HEADS * ML_DK), F32)]
    ilv = MAX_INTERLEAVE
    while (nb * (tb // c)) % ilv:
        ilv //= 2
    kern = functools.partial(_mixer_kernel, nb=nb, tb=tb, c=c, ilv=ilv)
    return pl.pallas_call(
        kern, grid=(bsz // nb, t // tb), in_specs=in_specs, out_specs=out_specs, out_shape=out_shape,
        scratch_shapes=scratch, name='mixer',
        compiler_params=pltpu.CompilerParams(dimension_semantics=('arbitrary', 'arbitrary'),
                                             vmem_limit_bytes=VMEM_LIMIT),
    )(x, mod, *params, *consts, *states)


FF_TILE = 256


def _ffn_kernel(x_ref, mod_ref, gffn_ref, wup_ref, wdown_ref, gfin_ref, y_ref, *, nb, tb, final):
    rows = nb * tb
    x = x_ref[...]
    mod = mod_ref[...]
    h = _rmsnorm_rows(x, gffn_ref[...]) * (1.0 + mod[:, 4:5, :]) + mod[:, 3:4, :]
    hb = h.reshape(rows, D_MODEL).astype(BF16)
    acc = jnp.zeros((rows, D_MODEL), F32)
    for j in range(D_FF // FF_TILE):
        gate = jnp.dot(hb, wup_ref[:, j * FF_TILE:(j + 1) * FF_TILE], preferred_element_type=F32)
        up = jnp.dot(hb, wup_ref[:, D_FF + j * FF_TILE:D_FF + (j + 1) * FF_TILE], preferred_element_type=F32)
        act = (_silu(gate) * up).astype(BF16)
        acc = acc + jnp.dot(act, wdown_ref[j * FF_TILE:(j + 1) * FF_TILE, :], preferred_element_type=F32)
    y = x + mod[:, 5:6, :] * acc.reshape(nb, tb, D_MODEL)
    if final:
        y = _rmsnorm_rows(y, gfin_ref[...])
    y_ref[...] = y


def _ffn_call(x, mod, lw, g_final, nb, tb, final):
    bsz, t, _ = x.shape
    xspec = pl.BlockSpec((nb, tb, D_MODEL), lambda bi, ti: (bi, ti, 0))
    mspec = pl.BlockSpec((nb, 6, D_MODEL), lambda bi, ti: (bi, 0, 0))
    params = [lw['g_ffn'], lw['w_up'], lw['w_down'], g_final]
    kern = functools.partial(_ffn_kernel, nb=nb, tb=tb, final=final)
    return pl.pallas_call(
        kern, grid=(bsz // nb, t // tb), in_specs=[xspec, mspec] + [_full_spec(a) for a in params],
        out_specs=xspec, out_shape=jax.ShapeDtypeStruct(x.shape, F32), name='ffn',
        compiler_params=pltpu.CompilerParams(dimension_semantics=('arbitrary', 'arbitrary'),
                                             vmem_limit_bytes=VMEM_LIMIT),
    )(x, mod, *params)


ADA_TILE = 1536


def _ada_kernel(c_ref, w_ref, b_ref, o_ref):
    o_ref[0] = jnp.dot(_silu(c_ref[...]).astype(BF16), w_ref[0], preferred_element_type=F32) + b_ref[0]


def _ada_call(c_all, w_ada, b_ada):
    n = c_all.shape[0]
    nt = 6 * D_MODEL // ADA_TILE
    return pl.pallas_call(
        _ada_kernel, grid=(DEPTH, nt),
        in_specs=[pl.BlockSpec((n, D_MODEL), lambda l, j: (0, 0)),
                  pl.BlockSpec((1, D_MODEL, ADA_TILE), lambda l, j: (l, 0, j)),
                  pl.BlockSpec((1, 1, ADA_TILE), lambda l, j: (l, 0, j))],
        out_specs=pl.BlockSpec((1, n, ADA_TILE), lambda l, j: (l, 0, j)),
        out_shape=jax.ShapeDtypeStruct((DEPTH, n, 6 * D_MODEL), F32), name='ada',
        compiler_params=pltpu.CompilerParams(dimension_semantics=('arbitrary', 'arbitrary')),
    )(c_all, w_ada, b_ada.reshape(DEPTH, 1, 6 * D_MODEL))


_REF_SPLITS = (('dn_qkv', DN_CONV_CH), ('dn_b', N_HEADS), ('dn_a', N_HEADS), ('dn_g', GROUP_W),
               ('hg_q', GROUP_W), ('hg_f', GROUP_W), ('hg_i', GROUP_W), ('hg_g', GROUP_W),
               ('gla_q', GLA_W), ('gla_k', GLA_W), ('gla_v', GROUP_W), ('gla_r', GLA_RANK), ('gla_g', GROUP_W),
               ('ml_q', GROUP_W), ('ml_k', GROUP_W), ('ml_v', GROUP_W), ('ml_i', N_HEADS), ('ml_f', N_HEADS),
               ('ml_o', GROUP_W))


def _permute_w_in(w):
    d = w.shape[0]
    cols, off = {}, 0
    for name, n in _REF_SPLITS:
        cols[name] = w[:, off:off + n]
        off += n
    assert off == w.shape[1]
    pieces, lane = [], 0
    for name, start in (('dn_b', S_DNB), ('gla_r', S_GLAR), ('dn_a', S_DNA), ('ml_i', S_MLI), ('ml_f', S_MLF)):
        pieces += [jnp.zeros((d, start - lane), w.dtype), cols[name]]
        lane = start + cols[name].shape[1]
    small = jnp.concatenate(pieces + [jnp.zeros((d, LANES - lane), w.dtype)], axis=1)
    cols['small'] = small
    order = ('dn_qkv', 'small', 'hg_f', 'hg_q', 'dn_g', 'hg_g', 'gla_g', 'ml_o',
             'hg_i', 'gla_q', 'gla_k', 'gla_v', 'ml_q', 'ml_k', 'ml_v')
    out = jnp.concatenate([cols[k] for k in order], axis=1)
    assert out.shape[1] == N_IN
    return out


def _lane_row(pairs, width=LANES):
    row = jnp.zeros((width,), F32)
    for off, val in pairs:
        row = row.at[off:off + val.shape[0]].set(val.astype(F32))
    return row


def _layer_weights(p, l):
    sp = jnp.zeros((SUB, LANES), F32)
    sp = sp.at[0].set(_lane_row([(S_DNA, p['dn_dt_bias'][l]), (S_MLI, p['ml_i_bias'][l]), (S_MLF, p['ml_f_bias'][l])]))
    sp = sp.at[1].set(_lane_row([(S_DNA, p['dn_a_log'][l])]))
    gn = jnp.stack([jnp.tile(p[k][l].astype(F32), N_HEADS) for k in ('dn_norm_g', 'hg_norm_g', 'gla_norm_g', 'ml_norm_g')])
    wup = jnp.zeros((LANES, GLA_W), F32).at[S_GLAR:S_GLAR + GLA_RANK].set(p['gla_w_up'][l]).astype(BF16)
    lb_sel = (jnp.arange(DEPTH) >= 1) & (jnp.arange(DEPTH) <= l)
    return dict(
        g_mix=p['g_mix'][l].reshape(1, D_MODEL), g_ffn=p['g_ffn'][l].reshape(1, D_MODEL),
        w_in=_permute_w_in(p['w_in'][l]).astype(BF16), w_out=p['w_out'][l].astype(BF16),
        conv_w=p['dn_conv_w'][l], sp=sp, gn=gn, lb_logits=p['hg_lb_logits'].astype(F32),
        lb_sel=lb_sel.astype(F32).reshape(DEPTH, 1), wup=wup, bup=p['gla_b_up'][l].reshape(1, GLA_W).astype(F32),
        w_up=p['w_up'][l].astype(BF16), w_down=p['w_down'][l].astype(BF16))


def _trunk(x, mods, states, lws, g_final, nb, tb):
    new_states = []
    for l in range(DEPTH):
        outs = _mixer_call(x, mods[l], lws[l], states[l], nb, tb)
        x = outs[0]
        new_states.append(outs[1:])
        x = _ffn_call(x, mods[l], lws[l], g_final, *_tiling(x.shape[0], x.shape[1], FFN_ROWS, FFN_ROWS), l == DEPTH - 1)
    return x, new_states


def _pack_states(conv, s_dn, s_hg, s_gla, c_ml, n_ml, m_ml, l):
    b = conv.shape[1]
    m_row = jnp.zeros((b, 1, LANES), F32).at[:, 0, S_MLF:S_MLF + N_HEADS].set(m_ml[l].astype(F32))
    return (conv[l].astype(F32), s_dn[l].astype(F32), s_hg[l].astype(F32), s_gla[l].astype(F32),
            c_ml[l].astype(F32), n_ml[l].astype(F32).reshape(b, 1, N_HEADS * ML_DK), m_row)


def _unpack_states(sts):
    conv, s_dn, s_hg, s_gla, c_ml = (jnp.stack([s[i] for s in sts]) for i in range(5))
    n_ml = jnp.stack([s[5].reshape(s[5].shape[0], N_HEADS, ML_DK) for s in sts])
    m_ml = jnp.stack([s[6][:, 0, S_MLF:S_MLF + N_HEADS] for s in sts])
    return conv, s_dn, s_hg, s_gla, c_ml, n_ml, m_ml


def _zero_states(b):
    z = lambda *s: jnp.zeros(s, F32)
    return (z(b, DN_CONV - 1, DN_CONV_CH), z(b, N_HEADS, DN_DK, HEAD_V), z(b, N_HEADS, HG_DK, HEAD_V),
            z(b, N_HEADS, GLA_DK, HEAD_V), z(b, N_HEADS, ML_DK, HEAD_V), z(b, 1, N_HEADS * ML_DK), z(b, 1, LANES))


def _tiling(bsz, t, rows, tokens):
    tb = min(t, tokens)
    nb = max(1, min(bsz, rows // tb))
    while bsz % nb:
        nb -= 1
    return nb, tb


def kernel(x_prompt, x_sample, c_prompt, c_sample, cache_dn_conv, state_dn, state_hgrn, state_gla, state_mlstm_c, state_mlstm_n, state_mlstm_m, w_ada, b_ada, g_mix, g_ffn, w_in, dn_conv_w, dn_a_log, dn_dt_bias, dn_norm_g, hg_lb_logits, hg_norm_g, gla_w_up, gla_b_up, gla_norm_g, ml_i_bias, ml_f_bias, ml_norm_g, w_out, w_up, w_down, g_final):
    p = dict(g_mix=g_mix, g_ffn=g_ffn, w_in=w_in, dn_conv_w=dn_conv_w, dn_a_log=dn_a_log, dn_dt_bias=dn_dt_bias,
             dn_norm_g=dn_norm_g, hg_lb_logits=hg_lb_logits, hg_norm_g=hg_norm_g, gla_w_up=gla_w_up,
             gla_b_up=gla_b_up, gla_norm_g=gla_norm_g, ml_i_bias=ml_i_bias, ml_f_bias=ml_f_bias,
             ml_norm_g=ml_norm_g, w_out=w_out, w_up=w_up, w_down=w_down)
    lws = [_layer_weights(p, l) for l in range(DEPTH)]
    gfin = g_final.reshape(1, D_MODEL).astype(F32)
    bp, bs = x_prompt.shape[0], x_sample.shape[0]
    mod = _ada_call(jnp.concatenate([c_prompt, c_sample], axis=0).astype(F32), w_ada.astype(BF16),
                    b_ada.astype(F32)).reshape(DEPTH, bp + bs, 6, D_MODEL)

    outs = []
    raw = (cache_dn_conv, state_dn, state_hgrn, state_gla, state_mlstm_c, state_mlstm_n, state_mlstm_m)
    for x, lo, hi, states in ((x_prompt, 0, bp, [_zero_states(bp)] * DEPTH),
                              (x_sample, bp, bp + bs, [_pack_states(*raw, l) for l in range(DEPTH)])):
        nb, tb = _tiling(x.shape[0], x.shape[1], MIXER_ROWS, MIXER_TOKENS)
        y, new = _trunk(x.astype(F32), [mod[l, lo:hi] for l in range(DEPTH)], states, lws, gfin, nb, tb)
        outs.append((y, _unpack_states(new)))
    (y_p, st_p), (y_s, st_s) = outs
    return (y_p, y_s) + tuple(st_p) + tuple(st_s)
```

```python
import functools

import numpy as np
import jax
import jax.numpy as jnp
from jax import lax
from jax.experimental import pallas as pl
from jax.experimental.pallas import tpu as pltpu

F32 = jnp.float32
BF16 = jnp.bfloat16

D_MODEL = 1024
DEPTH = 2
CHUNK = 64
N_HEADS = 4
HEAD_V = 64
GROUP_W = N_HEADS * HEAD_V
MIX_W = 4 * GROUP_W
DN_DK = 64
DN_CONV = 4
DN_CONV_CH = 3 * GROUP_W
HG_DK = 64
GLA_DK = 32
GLA_W = N_HEADS * GLA_DK
GLA_RANK = 16
GLA_TAU = 16.0
ML_DK = 64
D_FF = 2816
EPS = 1e-6

Z_DNQKV = 0
Z_SMALL = 768
Z_HGF = 896
Z_HGQ = 1152
Z_DNG = 1408
Z_HGG = 1664
Z_GLAG = 1920
Z_MLO = 2176
N_EARLY = 2432
Z_HGI = 2432
Z_GLAQ = 2688
Z_GLAK = 2816
Z_GLAV = 2944
Z_MLQ = 3200
Z_MLK = 3456
Z_MLV = 3712
N_IN = 3968
S_DNB = 0
S_GLAR = 16
S_DNA = 32
S_MLI = 64
S_MLF = 96

SUB = 8
LANES = 128
TRI_BLOCK = 16
VMEM_LIMIT = 56 * 1024 * 1024

M_INCL, M_STRICT, M_DIAG, M_SAMEBLK = 0, 1, 2, 3
M_LEVEL0 = 4
LOG2E = 1.4426950408889634
MIXER_ROWS = 256
MIXER_TOKENS = 256
FFN_ROWS = 1024
STAGGER = 4
MAX_INTERLEAVE = 4


def _sigmoid(x):
    return 1.0 / (1.0 + jnp.exp(-x))


def _silu(x):
    return x * _sigmoid(x)


def _log_sigmoid(x):
    return jnp.minimum(x, 0.0) - jnp.log1p(jnp.exp(-jnp.abs(x)))


def _softplus(x):
    return jnp.maximum(x, 0.0) + jnp.log1p(jnp.exp(-jnp.abs(x)))


def _dot(a, b):
    return jnp.dot(a.astype(BF16), b.astype(BF16), preferred_element_type=F32)


def _dot_nt(a, b):
    return lax.dot_general(a.astype(BF16), b.astype(BF16), (((1,), (1,)), ((), ())),
                           preferred_element_type=F32)


def _split3(x):
    x1 = x.astype(BF16)
    r = x - x1.astype(F32)
    x2 = r.astype(BF16)
    r = r - x2.astype(F32)
    return x1, x2, r.astype(BF16)


def _move_rows(xs, sel):
    parts = [_split3(x) for x in xs]
    y = jnp.dot(jnp.concatenate([p[i] for i in range(3) for p in parts], axis=0), sel, preferred_element_type=F32)
    n = sum(x.shape[0] for x in xs)
    outs = []
    off = 0
    for x in xs:
        r = x.shape[0]
        outs.append(y[off:off + r] + (y[n + off:n + off + r] + y[2 * n + off:2 * n + off + r]))
        off += r
    return outs


def _move_r(x, sel):
    return _move_rows([x], sel)[0]


def _move_l(sel, x):
    x1, x2, x3 = _split3(x)
    d = lambda a: jnp.dot(sel, a, preferred_element_type=F32)
    return d(x1) + (d(x2) + d(x3))


def _move_tn(x, sel):
    x1, x2, x3 = _split3(x)
    d = lambda a: lax.dot_general(a, sel, (((0,), (0,)), ((), ())), preferred_element_type=F32)
    return d(x1) + (d(x2) + d(x3))


def _move_nt(a, b):
    nt = lambda x, y: lax.dot_general(x, y, (((1,), (1,)), ((), ())), preferred_element_type=F32)
    if a.dtype == BF16:
        d = lambda p: nt(a, p)
        x1, x2, x3 = _split3(b)
    else:
        d = lambda p: nt(p, b)
        x1, x2, x3 = _split3(a)
    return d(x1) + (d(x2) + d(x3))


def _bd(x, mask):
    xb = x.astype(BF16)
    return jnp.concatenate([xb] * N_HEADS, axis=0) * mask


def _scan0(x, op, fill):
    n = x.shape[0]
    row = lax.broadcasted_iota(jnp.int32, x.shape, 0)
    sh = 1
    while sh < n:
        r = pltpu.roll(x, sh, axis=0)
        x = op(x, jnp.where(row >= sh, r, fill))
        sh *= 2
    return x


def _rowform(xe, diag):
    return jnp.sum(xe * diag, axis=0, keepdims=True)


def _rmsnorm_rows(x, g):
    return x * lax.rsqrt(jnp.mean(x * x, axis=-1, keepdims=True) + EPS) * g


def _run_interleaved(gens):
    live = list(gens)
    rnd = 0
    while live:
        alive = []
        for start, g in live:
            if rnd >= start:
                try:
                    next(g)
                except StopIteration:
                    continue
            alive.append((start, g))
        live = alive
        rnd += 1


def _await(boxes, key):
    while key not in boxes:
        yield
    return boxes[key]


def _tri_solve(mm, rhs, masks, bdp, bdr, c):
    assert c // TRI_BLOCK <= 4
    eye = masks[M_DIAG]
    mul = lambda a, b: _dot(a, _bd(b, bdp))
    app = lambda a, r: _dot(a, _bd(r, bdr))
    md = mm * masks[M_SAMEBLK]
    mo = mm - md
    p2 = mul(md, md)
    yield
    d = eye - md
    d = d + mul(d, p2)
    p4 = mul(p2, p2)
    yield
    d = d + mul(d, p4)
    p8 = mul(p4, p4)
    yield
    d = d + mul(d, p8)
    yield
    n = mul(d, mo)
    ys = [app(d, r) for r in rhs]
    yield
    zs = [y - app(n, y) for y in ys]
    if c // TRI_BLOCK <= 2:
        yield
        return zs
    n2 = mul(n, n)
    yield
    ws = [z + app(n2, z) for z in zs]
    yield
    return ws


def _deltanet_chunk(q, k, v, beta_s, gam_s, get_state, kc, c):
    masks = kc['masks']
    gam_e = _move_r(gam_s, kc['e_c'][1])
    beta_e = _move_r(beta_s, kc['e_c'][0])
    if c == HEAD_V:
        gam_d, beta_d = gam_e, beta_e
    else:
        gam_d = _move_r(gam_s, kc['e_d'][1])
        beta_d = _move_r(beta_s, kc['e_d'][0])
    kq = _dot_nt(jnp.concatenate([k, q], axis=0), _bd(k, kc['bd256'][...]))
    kk, qk = kq[0:c], kq[c:2 * c]
    yield
    gam_r = _rowform(gam_e, masks[M_DIAG])
    dec = jnp.exp(jnp.minimum(gam_e - gam_r, 0.0))
    mm = beta_e * kk * dec * masks[M_STRICT]
    eg = jnp.exp(gam_d)
    w, u0 = yield from _tri_solve(mm, [beta_d * eg * k, beta_d * v], masks, kc['bdp'][...], kc['bd256'][...], c)
    gl = gam_d[c - 1:c, :]
    kdec_t = (k * jnp.exp(gl - gam_d)).T
    st = yield from get_state()
    wq = _dot(jnp.concatenate([w, q * eg], axis=0), st)
    u = u0 - wq[0:c]
    qs = wq[c:2 * c]
    yield
    st_new = jnp.exp(gl) * st + kc['st256'][...] * _dot(kdec_t, u)
    o = qs + _dot(qk * dec * masks[M_INCL], _bd(u, kc['bd256'][...]))
    return o, st_new


def _block_ref(b, sz):
    c, w = b.shape
    g3 = b.reshape(c // (2 * sz), 2 * sz, w)
    return jnp.broadcast_to(g3[:, sz - 1:sz, :], g3.shape).reshape(c, w)


def _gla_chunk(q, k, v, g2, get_state, kc, c, wide):
    masks = kc['masks']
    bdk = kc['bd256'][...] if wide else kc['bd128'][...]
    ie = kc['ie256'][...] if wide else kc['ie128'][...]
    stm = kc['st256'][...] if wide else kc['st128'][...]
    v_t = v.T
    bd2 = _move_l(kc['lmat'][...], g2)
    b = bd2[0:c, :]
    yield
    row = lax.broadcasted_iota(jnp.int32, b.shape, 0)
    x1 = jnp.where(jnp.bitwise_and(row, 1) == 1, q * pltpu.roll(k, 1, axis=0) * jnp.exp2(g2), 0.0)
    dd = _dot(jnp.concatenate([q * k, x1], axis=0), ie)
    nlvl = c.bit_length() - 1
    attn = dd[0:c] * masks[M_DIAG] + dd[c:2 * c] * masks[M_LEVEL0 + nlvl - 1]
    lvl = 0
    sz = c // 2
    while sz >= 2:
        d = b - _block_ref(b, sz) if sz >= 4 else bd2[c:2 * c, :]
        e = jnp.exp2(jnp.minimum(d, -d))
        attn = attn + _dot_nt(q * e, _bd(k * e, bdk)) * masks[M_LEVEL0 + lvl]
        yield
        sz //= 2
        lvl += 1
    last = b[c - 1:c, :]
    upd = stm * _dot(v_t, k * jnp.exp2(last - b))
    o = _dot(attn, _bd(v, kc['bd256'][...]))
    yield
    st = yield from get_state()
    st_new = jnp.exp2(last) * st + upd
    o = o + _dot_nt(q * jnp.exp2(b), st)
    return o, st_new


def _mlstm_chunk(q, k, v, ig_s, lf_s, get_m, put_m, get_cn, kc, c):
    masks = kc['masks']
    fcum = _move_l(kc['lmat'][0:c, :], lf_s)
    a = ig_s - fcum
    imax = fcum + _scan0(a, jnp.maximum, -jnp.inf)
    fl = fcum[c - 1:c, :]
    lw = fl - fcum + ig_s
    lw_max = jnp.max(lw, axis=0, keepdims=True)
    kb = _bd(k, kc['bd256'][...])
    qk = _dot_nt(q, kb)
    a_e = _move_r(a, kc['e_c'][2])
    a_r = _rowform(a_e, masks[M_DIAG])
    yield
    m_row = yield from get_m()
    m_new = jnp.maximum(fl + m_row, lw_max)
    put_m(m_new)
    mt = jnp.maximum(fcum + m_row, imax)
    rows = lambda r: jnp.broadcast_to(r, (2 * SUB, LANES))
    consts = [rows(fl + m_row - m_new), rows(m_row), rows(fl - m_new)]
    x1 = fcum - mt
    if c == HEAD_V:
        x1_e, lfl, lwc, m_d, sh_d = _move_rows([x1, -mt] + consts, kc['e_d'][2])
        x1_d, a_d = x1_e, a_e
    else:
        x1_e = _move_r(x1, kc['e_c'][2])
        x1_d, lfl, a_d, lwc, m_d, sh_d = _move_rows([x1, -mt, a] + consts, kc['e_d'][2])
    w_inter = jnp.exp(x1_d + m_d[0:1, :])
    floor = jnp.exp(lfl)
    ws = jnp.exp(a_d + sh_d[0:1, :])
    wc = jnp.exp(lwc)[0:1, :]
    wsv_t = (ws * v).T
    yield
    w_intra = jnp.exp(jnp.minimum(x1_e + a_r, 0.0)) * masks[M_INCL] * qk
    num = _dot(w_intra, _bd(v, kc['bd256'][...]))
    den = _dot(w_intra, kc['iep'][...])
    upd = kc['st256'][...] * _dot(wsv_t, k)
    n_upd = jnp.sum(ws * k, axis=0, keepdims=True)
    yield
    ct, n_row = yield from get_cn()
    ct_new = wc * ct + upd
    n_new = wc * n_row + n_upd
    num = num + w_inter * _dot_nt(q, ct)
    den = den + w_inter * _dot(q * n_row, kc['ones256'][...])
    hh = num / jnp.maximum(jnp.abs(den), floor)
    return hh, ct_new, n_new


def _mixer_kernel(x_ref, mod_ref, gmix_ref, win_ref, wout_ref, convw_ref, sp_ref, gn_ref, lbl_ref, lbs_ref,
                  wup_ref, bup_ref,
                  masks_ref, ec_ref, ed_ref, bd256_ref, bd128_ref, bdp_ref, st256_ref, st128_ref,
                  ie256_ref, ie128_ref, iep_ref, ones256_ref, mean256_ref, lmat_ref, sel64_ref, sel32_ref,
                  conv0_ref, sdn0_ref, shg0_ref, sgla0_ref, c0_ref, n0_ref, m0_ref,
                  y_ref, convo_ref, sdno_ref, shgo_ref, sglao_ref, cmlo_ref, nml_ref, mml_ref,
                  z_ref, xp_ref, qkv_ref, mix_ref, hgk_ref, sdn_ref, shg_ref, sgla_ref, cml_ref, *, nb, tb, c, ilv):
    ti = pl.program_id(1)
    rows = nb * tb
    nchunk = tb // c
    mats = ((sdn0_ref, sdno_ref, sdn_ref, sel64_ref), (shg0_ref, shgo_ref, shg_ref, sel64_ref),
            (sgla0_ref, sglao_ref, sgla_ref, sel32_ref), (c0_ref, cmlo_ref, cml_ref, sel64_ref))

    @pl.when(ti == 0)
    def _():
        convo_ref[...] = conv0_ref[...]
        nml_ref[...] = n0_ref[...]
        mml_ref[...] = m0_ref[...]
        for raw_ref, _, st_ref, sel_ref in mats:
            place = _move_r if st_ref is sdn_ref else _move_tn
            for b in range(nb):
                st_ref[b] = jnp.concatenate([place(raw_ref[b, hd], sel_ref[hd]) for hd in range(N_HEADS)], axis=0)

    x = x_ref[...]
    mod = mod_ref[...]
    h = _rmsnorm_rows(x, gmix_ref[...]) * (1.0 + mod[:, 1:2, :]) + mod[:, 0:1, :]
    hb = h.reshape(rows, D_MODEL).astype(BF16)
    z_ref[:, 0:N_EARLY] = jnp.dot(hb, win_ref[:, 0:N_EARLY], preferred_element_type=F32)
    z_ref[:, N_EARLY:N_IN] = jnp.dot(hb, win_ref[:, N_EARLY:N_IN], preferred_element_type=F32)

    convw = convw_ref[...]
    for b in range(nb):
        xp_ref[b, SUB - (DN_CONV - 1):SUB, :] = convo_ref[b]
        xp_ref[b, SUB:SUB + tb, :] = z_ref[b * tb:(b + 1) * tb, Z_DNQKV:Z_DNQKV + DN_CONV_CH]
        acc = xp_ref[b, SUB - 3:SUB - 3 + tb, :] * convw[0:1, :]
        for j in range(1, DN_CONV):
            acc = acc + xp_ref[b, SUB - 3 + j:SUB - 3 + j + tb, :] * convw[j:j + 1, :]
        qkv_ref[b * tb:(b + 1) * tb, :] = _silu(acc)
        convo_ref[b] = xp_ref[b, SUB + tb - (DN_CONV - 1):SUB + tb, :]

    sp = sp_ref[...]
    gn = gn_ref[...]
    lbl = lbl_ref[...]
    lbs = lbs_ref[...]
    lbe = jnp.exp(lbl - jnp.max(lbl, axis=0, keepdims=True))
    lb = jnp.sum(lbs * (lbe / jnp.sum(lbe, axis=0, keepdims=True)), axis=0, keepdims=True)
    log_lb = jnp.log(lb)
    log_1mlb = jnp.log1p(-lb)
    neg_a = -jnp.exp(sp[1:2, :])

    zf = z_ref[:, Z_HGF:Z_HGF + GROUP_W]
    t2 = log_1mlb + _log_sigmoid(zf)
    mx = jnp.maximum(log_lb, t2)
    z_ref[:, Z_HGF:Z_HGF + GROUP_W] = (mx + jnp.log(jnp.exp(log_lb - mx) + jnp.exp(t2 - mx))) * LOG2E
    hgk_ref[...] = (1.0 - lb) * _sigmoid(-zf)
    z_ref[:, Z_HGQ:Z_HGQ + GROUP_W] = _silu(z_ref[:, Z_HGQ:Z_HGQ + GROUP_W])
    for m, col in enumerate((Z_DNG, Z_HGG, Z_GLAG)):
        z_ref[:, col:col + GROUP_W] = _silu(z_ref[:, col:col + GROUP_W]) * gn[m:m + 1, :]
    z_ref[:, Z_MLO:Z_MLO + GROUP_W] = _sigmoid(z_ref[:, Z_MLO:Z_MLO + GROUP_W]) * gn[3:4, :]

    kc = dict(masks=masks_ref, e_c=ec_ref, e_d=ed_ref, bd256=bd256_ref, bd128=bd128_ref, bdp=bdp_ref,
              st256=st256_ref, st128=st128_ref, ie256=ie256_ref, ie128=ie128_ref, iep=iep_ref,
              ones256=ones256_ref, mean256=mean256_ref, lmat=lmat_ref)
    run = min(ilv, nchunk)
    assert ilv % run == 0 and nchunk % run == 0

    def chunk_gens(it, u, boxes):
        i = it * ilv + u
        pos = u % run
        if nb == 1:
            b = 0
        elif ilv % nchunk == 0:
            b = it * (ilv // nchunk) + u // nchunk
        else:
            b = i // nchunk
        rs = pl.ds(pl.multiple_of(i * c, c), c)
        zc = lambda off, w: z_ref[rs, off:off + w]

        def getter(key, read):
            def get():
                if pos > 0:
                    return (yield from _await(boxes, (key, u - 1)))
                return read()
                yield
            return get

        def dn():
            small = zc(Z_SMALL, LANES)
            sb = small + sp[0:1, :]
            qkv = qkv_ref[rs, :]
            cq, ck, cv = qkv[:, 0:GROUP_W], qkv[:, GROUP_W:2 * GROUP_W], qkv[:, 2 * GROUP_W:3 * GROUP_W]
            ss = _dot(jnp.concatenate([cq * cq, ck * ck], axis=0), kc['ones256'][...])
            ssq, ssk = ss[0:c], ss[c:2 * c]
            beta_s = _sigmoid(small)
            gam_s = _move_l(kc['lmat'][0:c, :], neg_a * _softplus(sb))
            yield
            dq = cq * lax.rsqrt(ssq + EPS) * (DN_DK ** -0.5)
            dk = ck * lax.rsqrt(ssk + EPS)
            o, st_new = yield from _deltanet_chunk(dq, dk, cv, beta_s, gam_s,
                                                   getter('dn', lambda: sdn_ref[b]), kc, c)
            boxes[('dn', u)] = st_new
            sdn_ref[b] = st_new
            boxes[('out', u, 0)] = o

        def hg():
            o, st_new = yield from _gla_chunk(zc(Z_HGQ, GROUP_W), hgk_ref[rs, :], zc(Z_HGI, GROUP_W),
                                              zc(Z_HGF, GROUP_W), getter('hg', lambda: shg_ref[b]), kc, c, True)
            boxes[('hg', u)] = st_new
            shg_ref[b] = st_new
            boxes[('out', u, 1)] = o

        def gla():
            small = zc(Z_SMALL, LANES)
            g_gla = _log_sigmoid(_dot(small, wup_ref[...]) + bup_ref[...]) * (LOG2E / GLA_TAU)
            yield
            o, st_new = yield from _gla_chunk(zc(Z_GLAQ, GLA_W) * (GLA_DK ** -0.5), zc(Z_GLAK, GLA_W),
                                              zc(Z_GLAV, GROUP_W), g_gla,
                                              getter('gla', lambda: sgla_ref[b]), kc, c, False)
            boxes[('gla', u)] = st_new
            sgla_ref[b] = st_new
            boxes[('out', u, 2)] = o

        def ml():
            sb = zc(Z_SMALL, LANES) + sp[0:1, :]
            ig_s = pltpu.roll(sb, S_MLF - S_MLI, axis=1)
            lf_s = _log_sigmoid(sb)

            def put_m(m_new):
                boxes[('ml_m', u)] = m_new
                mml_ref[b] = m_new

            hh, c_new, n_new = yield from _mlstm_chunk(
                zc(Z_MLQ, GROUP_W) * (ML_DK ** -0.5), zc(Z_MLK, GROUP_W), zc(Z_MLV, GROUP_W), ig_s, lf_s,
                getter('ml_m', lambda: mml_ref[b]), put_m,
                getter('ml_cn', lambda: (cml_ref[b], nml_ref[b])), kc, c)
            boxes[('ml_cn', u)] = (c_new, n_new)
            cml_ref[b] = c_new
            nml_ref[b] = n_new
            boxes[('out', u, 3)] = hh

        late = STAGGER * pos
        return [(0, dn()), (late, hg()), (late, gla()), (late, ml())]

    def chunk_body(it, carry):
        boxes = {}
        gens = []
        for u in range(ilv):
            gens += chunk_gens(it, u, boxes)
        _run_interleaved(gens)
        outs = jnp.concatenate([boxes[('out', u, m)] for u in range(ilv) for m in range(4)], axis=0)
        ms = _dot(outs * outs, kc['mean256'][...])
        normed = outs * lax.rsqrt(ms + EPS)
        gate_cols = (Z_DNG, Z_HGG, Z_GLAG, Z_MLO)
        for u in range(ilv):
            rs = pl.ds(pl.multiple_of((it * ilv + u) * c, c), c)
            for m in range(4):
                gate = z_ref[rs, gate_cols[m]:gate_cols[m] + GROUP_W]
                r0 = (u * 4 + m) * c
                mix_ref[rs, m * GROUP_W:(m + 1) * GROUP_W] = (normed[r0:r0 + c] * gate).astype(BF16)
        return carry

    lax.fori_loop(0, nb * nchunk // ilv, chunk_body, 0)

    @pl.when(ti == pl.num_programs(1) - 1)
    def _():
        for _, out_ref, st_ref, sel_ref in mats:
            for b in range(nb):
                for hd in range(N_HEADS):
                    blk = st_ref[b, hd * HEAD_V:(hd + 1) * HEAD_V, :]
                    out_ref[b, hd] = _move_nt(blk, sel_ref[hd]) if st_ref is sdn_ref else _move_nt(sel_ref[hd], blk)

    out = jnp.dot(mix_ref[...], wout_ref[...], preferred_element_type=F32).reshape(nb, tb, D_MODEL)
    y_ref[...] = x + mod[:, 2:3, :] * out


def _const_tables(c):
    pc = N_HEADS * c
    t = np.arange(c)[:, None]
    lane = np.arange(pc)[None, :]
    hs, s = lane // c, lane % c
    masks = [s <= t, s < t, s == t, (s // TRI_BLOCK) == (t // TRI_BLOCK)]
    sz = c // 2
    while sz >= 1:
        masks.append(((s // (2 * sz)) == (t // (2 * sz))) & ((t // sz) % 2 == 1) & ((s // sz) % 2 == 0))
        sz //= 2
    masks = np.stack([np.broadcast_to(m, (c, pc)) for m in masks]).astype(np.float32)
    r = np.arange(c)[None, :]
    tri = (r <= t).astype(np.float32)
    lmat = np.concatenate([tri, tri - tri[(np.arange(c) // 4) * 4 + 1]], axis=0)

    def expand(col0, w):
        j = np.arange(LANES)[:, None]
        l = np.arange(N_HEADS * w)[None, :]
        return (j == col0 + l // w).astype(np.float32)

    e_c = np.stack([expand(S_DNB, c), expand(S_DNA, c), expand(S_MLF, c)])
    e_d = np.stack([expand(S_DNB, HEAD_V), expand(S_DNA, HEAD_V), expand(S_MLF, HEAD_V)])

    def headsel(dk):
        d = np.arange(dk)[None, :, None]
        l = np.arange(N_HEADS * dk)[None, None, :]
        return (l == np.arange(N_HEADS)[:, None, None] * dk + d).astype(np.float32)

    def blk(nr, rg, nl, lg):
        return ((np.arange(nr)[:, None] // rg) == (np.arange(nl)[None, :] // lg)).astype(np.float32)

    tabs = dict(
        masks=jnp.asarray(masks), e_c=jnp.asarray(e_c, BF16), e_d=jnp.asarray(e_d, BF16),
        bd256=jnp.asarray(blk(pc, c, GROUP_W, HEAD_V), BF16), bd128=jnp.asarray(blk(pc, c, GLA_W, GLA_DK), BF16),
        bdp=jnp.asarray(blk(pc, c, pc, c), BF16),
        st256=jnp.asarray(blk(GROUP_W, HEAD_V, GROUP_W, HEAD_V)), st128=jnp.asarray(blk(GROUP_W, HEAD_V, GLA_W, GLA_DK)),
        ie256=jnp.asarray(blk(GROUP_W, HEAD_V, pc, c), BF16), ie128=jnp.asarray(blk(GLA_W, GLA_DK, pc, c), BF16),
        iep=jnp.asarray(blk(pc, c, GROUP_W, HEAD_V), BF16),
        ones256=jnp.asarray(blk(GROUP_W, HEAD_V, GROUP_W, HEAD_V), BF16),
        mean256=jnp.asarray(blk(GROUP_W, HEAD_V, GROUP_W, HEAD_V) / HEAD_V, BF16), lmat=jnp.asarray(lmat, BF16),
        sel64=jnp.asarray(headsel(HEAD_V), BF16), sel32=jnp.asarray(headsel(GLA_DK), BF16))
    order = ['masks', 'e_c', 'e_d', 'bd256', 'bd128', 'bdp', 'st256', 'st128', 'ie256', 'ie128', 'iep', 'ones256',
             'mean256', 'lmat', 'sel64', 'sel32']
    return [tabs[k] for k in order]


def _full_spec(a):
    nd = a.ndim
    return pl.BlockSpec(a.shape, lambda bi, ti, _n=nd: (0,) * _n, pipeline_mode=pl.Buffered(1))


def _mixer_call(x, mod, lw, states, nb, tb):
    bsz, t, _ = x.shape
    c = min(CHUNK, t)
    assert t % tb == 0 and tb % c == 0 and bsz % nb == 0 and c % TRI_BLOCK == 0
    rows = nb * tb
    consts = _const_tables(c)
    params = [lw['g_mix'], lw['w_in'], lw['w_out'], lw['conv_w'], lw['sp'], lw['gn'], lw['lb_logits'], lw['lb_sel'],
              lw['wup'], lw['bup']]
    xspec = pl.BlockSpec((nb, tb, D_MODEL), lambda bi, ti: (bi, ti, 0))

    def bspec(a):
        nd = a.ndim
        return pl.BlockSpec((nb,) + a.shape[1:], lambda bi, ti, _n=nd: (bi,) + (0,) * (_n - 1))

    in_specs = ([xspec, bspec(mod)] + [_full_spec(a) for a in params] + [_full_spec(a) for a in consts]
                + [bspec(s) for s in states])
    out_shape = [jax.ShapeDtypeStruct(x.shape, F32)] + [jax.ShapeDtypeStruct(s.shape, F32) for s in states]
    out_specs = [xspec] + [bspec(s) for s in states]
    scratch = [pltpu.VMEM((rows, N_IN), F32),
               pltpu.VMEM((nb, SUB + tb, DN_CONV_CH), F32),
               pltpu.VMEM((rows, DN_CONV_CH), F32),
               pltpu.VMEM((rows, MIX_W), BF16),
               pltpu.VMEM((rows, N_HEADS * HG_DK), F32),
               pltpu.VMEM((nb, GROUP_W, N_HEADS * DN_DK), F32),
               pltpu.VMEM((nb, GROUP_W, N_HEADS * HG_DK), F32),
               pltpu.VMEM((nb, GROUP_W, GLA_W), F32),
               pltpu.VMEM((nb, GROUP_W, N_HEADS * ML_DK), F32)]
    ilv = MAX_INTERLEAVE
    while (nb * (tb // c)) % ilv:
        ilv //= 2
    kern = functools.partial(_mixer_kernel, nb=nb, tb=tb, c=c, ilv=ilv)
    return pl.pallas_call(
        kern, grid=(bsz // nb, t // tb), in_specs=in_specs, out_specs=out_specs, out_shape=out_shape,
        scratch_shapes=scratch, name='mixer',
        compiler_params=pltpu.CompilerParams(dimension_semantics=('arbitrary', 'arbitrary'),
                                             vmem_limit_bytes=VMEM_LIMIT),
    )(x, mod, *params, *consts, *states)


FF_TILE = 256


def _ffn_kernel(x_ref, mod_ref, gffn_ref, wup_ref, wdown_ref, gfin_ref, y_ref, *, nb, tb, final):
    rows = nb * tb
    x = x_ref[...]
    mod = mod_ref[...]
    h = _rmsnorm_rows(x, gffn_ref[...]) * (1.0 + mod[:, 4:5, :]) + mod[:, 3:4, :]
    hb = h.reshape(rows, D_MODEL).astype(BF16)
    acc = jnp.zeros((rows, D_MODEL), F32)
    for j in range(D_FF // FF_TILE):
        gate = jnp.dot(hb, wup_ref[:, j * FF_TILE:(j + 1) * FF_TILE], preferred_element_type=F32)
        up = jnp.dot(hb, wup_ref[:, D_FF + j * FF_TILE:D_FF + (j + 1) * FF_TILE], preferred_element_type=F32)
        act = (_silu(gate) * up).astype(BF16)
        acc = acc + jnp.dot(act, wdown_ref[j * FF_TILE:(j + 1) * FF_TILE, :], preferred_element_type=F32)
    y = x + mod[:, 5:6, :] * acc.reshape(nb, tb, D_MODEL)
    if final:
        y = _rmsnorm_rows(y, gfin_ref[...])
    y_ref[...] = y


def _ffn_call(x, mod, lw, g_final, nb, tb, final):
    bsz, t, _ = x.shape
    xspec = pl.BlockSpec((nb, tb, D_MODEL), lambda bi, ti: (bi, ti, 0))
    mspec = pl.BlockSpec((nb, 6, D_MODEL), lambda bi, ti: (bi, 0, 0))
    params = [lw['g_ffn'], lw['w_up'], lw['w_down'], g_final]
    kern = functools.partial(_ffn_kernel, nb=nb, tb=tb, final=final)
    return pl.pallas_call(
        kern, grid=(bsz // nb, t // tb), in_specs=[xspec, mspec] + [_full_spec(a) for a in params],
        out_specs=xspec, out_shape=jax.ShapeDtypeStruct(x.shape, F32), name='ffn',
        compiler_params=pltpu.CompilerParams(dimension_semantics=('arbitrary', 'arbitrary'),
                                             vmem_limit_bytes=VMEM_LIMIT),
    )(x, mod, *params)


ADA_TILE = 1536


def _ada_kernel(c_ref, w_ref, b_ref, o_ref):
    o_ref[0] = jnp.dot(_silu(c_ref[...]).astype(BF16), w_ref[0], preferred_element_type=F32) + b_ref[0]


def _ada_call(c_all, w_ada, b_ada):
    n = c_all.shape[0]
    nt = 6 * D_MODEL // ADA_TILE
    return pl.pallas_call(
        _ada_kernel, grid=(DEPTH, nt),
        in_specs=[pl.BlockSpec((n, D_MODEL), lambda l, j: (0, 0)),
                  pl.BlockSpec((1, D_MODEL, ADA_TILE), lambda l, j: (l, 0, j)),
                  pl.BlockSpec((1, 1, ADA_TILE), lambda l, j: (l, 0, j))],
        out_specs=pl.BlockSpec((1, n, ADA_TILE), lambda l, j: (l, 0, j)),
        out_shape=jax.ShapeDtypeStruct((DEPTH, n, 6 * D_MODEL), F32), name='ada',
        compiler_params=pltpu.CompilerParams(dimension_semantics=('arbitrary', 'arbitrary')),
    )(c_all, w_ada, b_ada.reshape(DEPTH, 1, 6 * D_MODEL))


_REF_SPLITS = (('dn_qkv', DN_CONV_CH), ('dn_b', N_HEADS), ('dn_a', N_HEADS), ('dn_g', GROUP_W),
               ('hg_q', GROUP_W), ('hg_f', GROUP_W), ('hg_i', GROUP_W), ('hg_g', GROUP_W),
               ('gla_q', GLA_W), ('gla_k', GLA_W), ('gla_v', GROUP_W), ('gla_r', GLA_RANK), ('gla_g', GROUP_W),
               ('ml_q', GROUP_W), ('ml_k', GROUP_W), ('ml_v', GROUP_W), ('ml_i', N_HEADS), ('ml_f', N_HEADS),
               ('ml_o', GROUP_W))


def _permute_w_in(w):
    d = w.shape[0]
    cols, off = {}, 0
    for name, n in _REF_SPLITS:
        cols[name] = w[:, off:off + n]
        off += n
    assert off == w.shape[1]
    pieces, lane = [], 0
    for name, start in (('dn_b', S_DNB), ('gla_r', S_GLAR), ('dn_a', S_DNA), ('ml_i', S_MLI), ('ml_f', S_MLF)):
        pieces += [jnp.zeros((d, start - lane), w.dtype), cols[name]]
        lane = start + cols[name].shape[1]
    small = jnp.concatenate(pieces + [jnp.zeros((d, LANES - lane), w.dtype)], axis=1)
    cols['small'] = small
    order = ('dn_qkv', 'small', 'hg_f', 'hg_q', 'dn_g', 'hg_g', 'gla_g', 'ml_o',
             'hg_i', 'gla_q', 'gla_k', 'gla_v', 'ml_q', 'ml_k', 'ml_v')
    out = jnp.concatenate([cols[k] for k in order], axis=1)
    assert out.shape[1] == N_IN
    return out


def _lane_row(pairs, width=LANES):
    row = jnp.zeros((width,), F32)
    for off, val in pairs:
        row = row.at[off:off + val.shape[0]].set(val.astype(F32))
    return row


def _layer_weights(p, l):
    sp = jnp.zeros((SUB, LANES), F32)
    sp = sp.at[0].set(_lane_row([(S_DNA, p['dn_dt_bias'][l]), (S_MLI, p['ml_i_bias'][l]), (S_MLF, p['ml_f_bias'][l])]))
    sp = sp.at[1].set(_lane_row([(S_DNA, p['dn_a_log'][l])]))
    gn = jnp.stack([jnp.tile(p[k][l].astype(F32), N_HEADS) for k in ('dn_norm_g', 'hg_norm_g', 'gla_norm_g', 'ml_norm_g')])
    wup = jnp.zeros((LANES, GLA_W), F32).at[S_GLAR:S_GLAR + GLA_RANK].set(p['gla_w_up'][l]).astype(BF16)
    lb_sel = (jnp.arange(DEPTH) >= 1) & (jnp.arange(DEPTH) <= l)
    return dict(
        g_mix=p['g_mix'][l].reshape(1, D_MODEL), g_ffn=p['g_ffn'][l].reshape(1, D_MODEL),
        w_in=_permute_w_in(p['w_in'][l]).astype(BF16), w_out=p['w_out'][l].astype(BF16),
        conv_w=p['dn_conv_w'][l], sp=sp, gn=gn, lb_logits=p['hg_lb_logits'].astype(F32),
        lb_sel=lb_sel.astype(F32).reshape(DEPTH, 1), wup=wup, bup=p['gla_b_up'][l].reshape(1, GLA_W).astype(F32),
        w_up=p['w_up'][l].astype(BF16), w_down=p['w_down'][l].astype(BF16))


def _trunk(x, mods, states, lws, g_final, nb, tb):
    new_states = []
    for l in range(DEPTH):
        outs = _mixer_call(x, mods[l], lws[l], states[l], nb, tb)
        x = outs[0]
        new_states.append(outs[1:])
        x = _ffn_call(x, mods[l], lws[l], g_final, *_tiling(x.shape[0], x.shape[1], FFN_ROWS, FFN_ROWS), l == DEPTH - 1)
    return x, new_states


def _pack_states(conv, s_dn, s_hg, s_gla, c_ml, n_ml, m_ml, l):
    b = conv.shape[1]
    m_row = jnp.zeros((b, 1, LANES), F32).at[:, 0, S_MLF:S_MLF + N_HEADS].set(m_ml[l].astype(F32))
    return (conv[l].astype(F32), s_dn[l].astype(F32), s_hg[l].astype(F32), s_gla[l].astype(F32),
            c_ml[l].astype(F32), n_ml[l].astype(F32).reshape(b, 1, N_HEADS * ML_DK), m_row)


def _unpack_states(sts):
    conv, s_dn, s_hg, s_gla, c_ml = (jnp.stack([s[i] for s in sts]) for i in range(5))
    n_ml = jnp.stack([s[5].reshape(s[5].shape[0], N_HEADS, ML_DK) for s in sts])
    m_ml = jnp.stack([s[6][:, 0, S_MLF:S_MLF + N_HEADS] for s in sts])
    return conv, s_dn, s_hg, s_gla, c_ml, n_ml, m_ml


def _zero_states(b):
    z = lambda *s: jnp.zeros(s, F32)
    return (z(b, DN_CONV - 1, DN_CONV_CH), z(b, N_HEADS, DN_DK, HEAD_V), z(b, N_HEADS, HG_DK, HEAD_V),
            z(b, N_HEADS, GLA_DK, HEAD_V), z(b, N_HEADS, ML_DK, HEAD_V), z(b, 1, N_HEADS * ML_DK), z(b, 1, LANES))


def _tiling(bsz, t, rows, tokens):
    tb = min(t, tokens)
    nb = max(1, min(bsz, rows // tb))
    while bsz % nb:
        nb -= 1
    return nb, tb


def kernel(x_prompt, x_sample, c_prompt, c_sample, cache_dn_conv, state_dn, state_hgrn, state_gla, state_mlstm_c, state_mlstm_n, state_mlstm_m, w_ada, b_ada, g_mix, g_ffn, w_in, dn_conv_w, dn_a_log, dn_dt_bias, dn_norm_g, hg_lb_logits, hg_norm_g, gla_w_up, gla_b_up, gla_norm_g, ml_i_bias, ml_f_bias, ml_norm_g, w_out, w_up, w_down, g_final):
    p = dict(g_mix=g_mix, g_ffn=g_ffn, w_in=w_in, dn_conv_w=dn_conv_w, dn_a_log=dn_a_log, dn_dt_bias=dn_dt_bias,
             dn_norm_g=dn_norm_g, hg_lb_logits=hg_lb_logits, hg_norm_g=hg_norm_g, gla_w_up=gla_w_up,
             gla_b_up=gla_b_up, gla_norm_g=gla_norm_g, ml_i_bias=ml_i_bias, ml_f_bias=ml_f_bias,
             ml_norm_g=ml_norm_g, w_out=w_out, w_up=w_up, w_down=w_down)
    lws = [_layer_weights(p, l) for l in range(DEPTH)]
    gfin = g_final.reshape(1, D_MODEL).astype(F32)
    bp, bs = x_prompt.shape[0], x_sample.shape[0]
    mod = _ada_call(jnp.concatenate([c_prompt, c_sample], axis=0).astype(F32), w_ada.astype(BF16),
                    b_ada.astype(F32)).reshape(DEPTH, bp + bs, 6, D_MODEL)

    outs = []
    raw = (cache_dn_conv, state_dn, state_hgrn, state_gla, state_mlstm_c, state_mlstm_n, state_mlstm_m)
    for x, lo, hi, states in ((x_prompt, 0, bp, [_zero_states(bp)] * DEPTH),
                              (x_sample, bp, bp + bs, [_pack_states(*raw, l) for l in range(DEPTH)])):
        nb, tb = _tiling(x.shape[0], x.shape[1], MIXER_ROWS, MIXER_TOKENS)
        y, new = _trunk(x.astype(F32), [mod[l, lo:hi] for l in range(DEPTH)], states, lws, gfin, nb, tb)
        outs.append((y, _unpack_states(new)))
    (y_p, st_p), (y_s, st_s) = outs
    return (y_p, y_s) + tuple(st_p) + tuple(st_s)
```

```python
import functools

import numpy as np
import jax
import jax.numpy as jnp
from jax import lax
from jax.experimental import pallas as pl
from jax.experimental.pallas import tpu as pltpu

F32 = jnp.float32
BF16 = jnp.bfloat16

D_MODEL = 1024
DEPTH = 2
CHUNK = 64
N_HEADS = 4
HEAD_V = 64
GROUP_W = N_HEADS * HEAD_V
MIX_W = 4 * GROUP_W
DN_DK = 64
DN_CONV = 4
DN_CONV_CH = 3 * GROUP_W
HG_DK = 64
GLA_DK = 32
GLA_W = N_HEADS * GLA_DK
GLA_RANK = 16
GLA_TAU = 16.0
ML_DK = 64
D_FF = 2816
EPS = 1e-6

Z_DNQKV = 0
Z_SMALL = 768
Z_HGF = 896
Z_HGQ = 1152
Z_DNG = 1408
Z_HGG = 1664
Z_GLAG = 1920
Z_MLO = 2176
N_EARLY = 2432
Z_HGI = 2432
Z_GLAQ = 2688
Z_GLAK = 2816
Z_GLAV = 2944
Z_MLQ = 3200
Z_MLK = 3456
Z_MLV = 3712
N_IN = 3968
S_DNB = 0
S_GLAR = 16
S_DNA = 32
S_MLI = 64
S_MLF = 96

SUB = 8
LANES = 128
TRI_BLOCK = 16
VMEM_LIMIT = 56 * 1024 * 1024

M_INCL, M_STRICT, M_DIAG, M_SAMEBLK = 0, 1, 2, 3
M_LEVEL0 = 4
LOG2E = 1.4426950408889634
MIXER_ROWS = 256
MIXER_TOKENS = 256
FFN_ROWS = 1024
STAGGER = 4
MAX_INTERLEAVE = 4


def _sigmoid(x):
    return 1.0 / (1.0 + jnp.exp(-x))


def _silu(x):
    return x * _sigmoid(x)


def _log_sigmoid(x):
    return jnp.minimum(x, 0.0) - jnp.log1p(jnp.exp(-jnp.abs(x)))


def _softplus(x):
    return jnp.maximum(x, 0.0) + jnp.log1p(jnp.exp(-jnp.abs(x)))


def _dot(a, b):
    return jnp.dot(a.astype(BF16), b.astype(BF16), preferred_element_type=F32)


def _dot_nt(a, b):
    return lax.dot_general(a.astype(BF16), b.astype(BF16), (((1,), (1,)), ((), ())),
                           preferred_element_type=F32)


def _split3(x):
    x1 = x.astype(BF16)
    r = x - x1.astype(F32)
    x2 = r.astype(BF16)
    r = r - x2.astype(F32)
    return x1, x2, r.astype(BF16)


def _move_rows(xs, sel):
    parts = [_split3(x) for x in xs]
    y = jnp.dot(jnp.concatenate([p[i] for i in range(3) for p in parts], axis=0), sel, preferred_element_type=F32)
    n = sum(x.shape[0] for x in xs)
    outs = []
    off = 0
    for x in xs:
        r = x.shape[0]
        outs.append(y[off:off + r] + (y[n + off:n + off + r] + y[2 * n + off:2 * n + off + r]))
        off += r
    return outs


def _move_r(x, sel):
    return _move_rows([x], sel)[0]


def _move_l(sel, x):
    x1, x2, x3 = _split3(x)
    d = lambda a: jnp.dot(sel, a, preferred_element_type=F32)
    return d(x1) + (d(x2) + d(x3))


def _move_tn(x, sel):
    x1, x2, x3 = _split3(x)
    d = lambda a: lax.dot_general(a, sel, (((0,), (0,)), ((), ())), preferred_element_type=F32)
    return d(x1) + (d(x2) + d(x3))


def _move_nt(a, b):
    nt = lambda x, y: lax.dot_general(x, y, (((1,), (1,)), ((), ())), preferred_element_type=F32)
    if a.dtype == BF16:
        d = lambda p: nt(a, p)
        x1, x2, x3 = _split3(b)
    else:
        d = lambda p: nt(p, b)
        x1, x2, x3 = _split3(a)
    return d(x1) + (d(x2) + d(x3))


def _bd(x, mask):
    xb = x.astype(BF16)
    return jnp.concatenate([xb] * N_HEADS, axis=0) * mask


def _scan0(x, op, fill):
    n = x.shape[0]
    row = lax.broadcasted_iota(jnp.int32, x.shape, 0)
    sh = 1
    while sh < n:
        r = pltpu.roll(x, sh, axis=0)
        x = op(x, jnp.where(row >= sh, r, fill))
        sh *= 2
    return x


def _rowform(xe, diag):
    return jnp.sum(xe * diag, axis=0, keepdims=True)


def _rmsnorm_rows(x, g):
    return x * lax.rsqrt(jnp.mean(x * x, axis=-1, keepdims=True) + EPS) * g


def _run_interleaved(gens):
    live = list(gens)
    rnd = 0
    while live:
        alive = []
        for start, g in live:
            if rnd >= start:
                try:
                    next(g)
                except StopIteration:
                    continue
            alive.append((start, g))
        live = alive
        rnd += 1


def _await(boxes, key):
    while key not in boxes:
        yield
    return boxes[key]


def _tri_solve(mm, rhs, masks, bdp, bdr, c):
    assert c // TRI_BLOCK <= 4
    eye = masks[M_DIAG]
    mul = lambda a, b: _dot(a, _bd(b, bdp))
    app = lambda a, r: _dot(a, _bd(r, bdr))
    md = mm * masks[M_SAMEBLK]
    mo = mm - md
    p2 = mul(md, md)
    yield
    d = eye - md
    d = d + mul(d, p2)
    p4 = mul(p2, p2)
    yield
    d = d + mul(d, p4)
    p8 = mul(p4, p4)
    yield
    d = d + mul(d, p8)
    yield
    n = mul(d, mo)
    ys = [app(d, r) for r in rhs]
    yield
    zs = [y - app(n, y) for y in ys]
    if c // TRI_BLOCK <= 2:
        yield
        return zs
    n2 = mul(n, n)
    yield
    ws = [z + app(n2, z) for z in zs]
    yield
    return ws


def _deltanet_chunk(q, k, v, beta_s, gam_s, get_state, kc, c):
    masks = kc['masks']
    gam_e = _move_r(gam_s, kc['e_c'][1])
    beta_e = _move_r(beta_s, kc['e_c'][0])
    if c == HEAD_V:
        gam_d, beta_d = gam_e, beta_e
    else:
        gam_d = _move_r(gam_s, kc['e_d'][1])
        beta_d = _move_r(beta_s, kc['e_d'][0])
    kq = _dot_nt(jnp.concatenate([k, q], axis=0), _bd(k, kc['bd256'][...]))
    kk, qk = kq[0:c], kq[c:2 * c]
    yield
    gam_r = _rowform(gam_e, masks[M_DIAG])
    dec = jnp.exp(jnp.minimum(gam_e - gam_r, 0.0))
    mm = beta_e * kk * dec * masks[M_STRICT]
    eg = jnp.exp(gam_d)
    w, u0 = yield from _tri_solve(mm, [beta_d * eg * k, beta_d * v], masks, kc['bdp'][...], kc['bd256'][...], c)
    gl = gam_d[c - 1:c, :]
    kdec_t = (k * jnp.exp(gl - gam_d)).T
    st = yield from get_state()
    wq = _dot(jnp.concatenate([w, q * eg], axis=0), st)
    u = u0 - wq[0:c]
    qs = wq[c:2 * c]
    yield
    st_new = jnp.exp(gl) * st + kc['st256'][...] * _dot(kdec_t, u)
    o = qs + _dot(qk * dec * masks[M_INCL], _bd(u, kc['bd256'][...]))
    return o, st_new


def _block_ref(b, sz):
    c, w = b.shape
    g3 = b.reshape(c // (2 * sz), 2 * sz, w)
    return jnp.broadcast_to(g3[:, sz - 1:sz, :], g3.shape).reshape(c, w)


def _gla_chunk(q, k, v, g2, get_state, kc, c, wide):
    masks = kc['masks']
    bdk = kc['bd256'][...] if wide else kc['bd128'][...]
    ie = kc['ie256'][...] if wide else kc['ie128'][...]
    stm = kc['st256'][...] if wide else kc['st128'][...]
    v_t = v.T
    bd2 = _move_l(kc['lmat'][...], g2)
    b = bd2[0:c, :]
    yield
    row = lax.broadcasted_iota(jnp.int32, b.shape, 0)
    x1 = jnp.where(jnp.bitwise_and(row, 1) == 1, q * pltpu.roll(k, 1, axis=0) * jnp.exp2(g2), 0.0)
    dd = _dot(jnp.concatenate([q * k, x1], axis=0), ie)
    nlvl = c.bit_length() - 1
    attn = dd[0:c] * masks[M_DIAG] + dd[c:2 * c] * masks[M_LEVEL0 + nlvl - 1]
    lvl = 0
    sz = c // 2
    while sz >= 2:
        d = b - _block_ref(b, sz) if sz >= 4 else bd2[c:2 * c, :]
        e = jnp.exp2(jnp.minimum(d, -d))
        attn = attn + _dot_nt(q * e, _bd(k * e, bdk)) * masks[M_LEVEL0 + lvl]
        yield
        sz //= 2
        lvl += 1
    last = b[c - 1:c, :]
    upd = stm * _dot(v_t, k * jnp.exp2(last - b))
    o = _dot(attn, _bd(v, kc['bd256'][...]))
    yield
    st = yield from get_state()
    st_new = jnp.exp2(last) * st + upd
    o = o + _dot_nt(q * jnp.exp2(b), st)
    return o, st_new


def _mlstm_chunk(q, k, v, ig_s, lf_s, get_m, put_m, get_cn, kc, c):
    masks = kc['masks']
    fcum = _move_l(kc['lmat'][0:c, :], lf_s)
    a = ig_s - fcum
    imax = fcum + _scan0(a, jnp.maximum, -jnp.inf)
    fl = fcum[c - 1:c, :]
    lw = fl - fcum + ig_s
    lw_max = jnp.max(lw, axis=0, keepdims=True)
    kb = _bd(k, kc['bd256'][...])
    qk = _dot_nt(q, kb)
    a_e = _move_r(a, kc['e_c'][2])
    a_r = _rowform(a_e, masks[M_DIAG])
    yield
    m_row = yield from get_m()
    m_new = jnp.maximum(fl + m_row, lw_max)
    put_m(m_new)
    mt = jnp.maximum(fcum + m_row, imax)
    rows = lambda r: jnp.broadcast_to(r, (2 * SUB, LANES))
    consts = [rows(fl + m_row - m_new), rows(m_row), rows(fl - m_new)]
    x1 = fcum - mt
    if c == HEAD_V:
        x1_e, lfl, lwc, m_d, sh_d = _move_rows([x1, -mt] + consts, kc['e_d'][2])
        x1_d, a_d = x1_e, a_e
    else:
        x1_e = _move_r(x1, kc['e_c'][2])
        x1_d, lfl, a_d, lwc, m_d, sh_d = _move_rows([x1, -mt, a] + consts, kc['e_d'][2])
    w_inter = jnp.exp(x1_d + m_d[0:1, :])
    floor = jnp.exp(lfl)
    ws = jnp.exp(a_d + sh_d[0:1, :])
    wc = jnp.exp(lwc)[0:1, :]
    wsv_t = (ws * v).T
    yield
    w_intra = jnp.exp(jnp.minimum(x1_e + a_r, 0.0)) * masks[M_INCL] * qk
    num = _dot(w_intra, _bd(v, kc['bd256'][...]))
    den = _dot(w_intra, kc['iep'][...])
    upd = kc['st256'][...] * _dot(wsv_t, k)
    n_upd = jnp.sum(ws * k, axis=0, keepdims=True)
    yield
    ct, n_row = yield from get_cn()
    ct_new = wc * ct + upd
    n_new = wc * n_row + n_upd
    num = num + w_inter * _dot_nt(q, ct)
    den = den + w_inter * _dot(q * n_row, kc['ones256'][...])
    hh = num / jnp.maximum(jnp.abs(den), floor)
    return hh, ct_new, n_new


def _mixer_kernel(x_ref, mod_ref, gmix_ref, win_ref, wout_ref, convw_ref, sp_ref, gn_ref, lbl_ref, lbs_ref,
                  wup_ref, bup_ref,
                  masks_ref, ec_ref, ed_ref, bd256_ref, bd128_ref, bdp_ref, st256_ref, st128_ref,
                  ie256_ref, ie128_ref, iep_ref, ones256_ref, mean256_ref, lmat_ref, sel64_ref, sel32_ref,
                  conv0_ref, sdn0_ref, shg0_ref, sgla0_ref, c0_ref, n0_ref, m0_ref,
                  y_ref, convo_ref, sdno_ref, shgo_ref, sglao_ref, cmlo_ref, nml_ref, mml_ref,
                  z_ref, xp_ref, qkv_ref, mix_ref, hgk_ref, sdn_ref, shg_ref, sgla_ref, cml_ref, *, nb, tb, c, ilv):
    ti = pl.program_id(1)
    rows = nb * tb
    nchunk = tb // c
    mats = ((sdn0_ref, sdno_ref, sdn_ref, sel64_ref), (shg0_ref, shgo_ref, shg_ref, sel64_ref),
            (sgla0_ref, sglao_ref, sgla_ref, sel32_ref), (c0_ref, cmlo_ref, cml_ref, sel64_ref))

    @pl.when(ti == 0)
    def _():
        convo_ref[...] = conv0_ref[...]
        nml_ref[...] = n0_ref[...]
        mml_ref[...] = m0_ref[...]
        for raw_ref, _, st_ref, sel_ref in mats:
            place = _move_r if st_ref is sdn_ref else _move_tn
            for b in range(nb):
                st_ref[b] = jnp.concatenate([place(raw_ref[b, hd], sel_ref[hd]) for hd in range(N_HEADS)], axis=0)

    x = x_ref[...]
    mod = mod_ref[...]
    h = _rmsnorm_rows(x, gmix_ref[...]) * (1.0 + mod[:, 1:2, :]) + mod[:, 0:1, :]
    hb = h.reshape(rows, D_MODEL).astype(BF16)
    z_ref[:, 0:N_EARLY] = jnp.dot(hb, win_ref[:, 0:N_EARLY], preferred_element_type=F32)
    z_ref[:, N_EARLY:N_IN] = jnp.dot(hb, win_ref[:, N_EARLY:N_IN], preferred_element_type=F32)

    convw = convw_ref[...]
    for b in range(nb):
        xp_ref[b, SUB - (DN_CONV - 1):SUB, :] = convo_ref[b]
        xp_ref[b, SUB:SUB + tb, :] = z_ref[b * tb:(b + 1) * tb, Z_DNQKV:Z_DNQKV + DN_CONV_CH]
        acc = xp_ref[b, SUB - 3:SUB - 3 + tb, :] * convw[0:1, :]
        for j in range(1, DN_CONV):
            acc = acc + xp_ref[b, SUB - 3 + j:SUB - 3 + j + tb, :] * convw[j:j + 1, :]
        qkv_ref[b * tb:(b + 1) * tb, :] = _silu(acc)
        convo_ref[b] = xp_ref[b, SUB + tb - (DN_CONV - 1):SUB + tb, :]

    sp = sp_ref[...]
    gn = gn_ref[...]
    lbl = lbl_ref[...]
    lbs = lbs_ref[...]
    lbe = jnp.exp(lbl - jnp.max(lbl, axis=0, keepdims=True))
    lb = jnp.sum(lbs * (lbe / jnp.sum(lbe, axis=0, keepdims=True)), axis=0, keepdims=True)
    log_lb = jnp.log(lb)
    log_1mlb = jnp.log1p(-lb)
    neg_a = -jnp.exp(sp[1:2, :])

    zf = z_ref[:, Z_HGF:Z_HGF + GROUP_W]
    t2 = log_1mlb + _log_sigmoid(zf)
    mx = jnp.maximum(log_lb, t2)
    z_ref[:, Z_HGF:Z_HGF + GROUP_W] = (mx + jnp.log(jnp.exp(log_lb - mx) + jnp.exp(t2 - mx))) * LOG2E
    hgk_ref[...] = (1.0 - lb) * _sigmoid(-zf)
    z_ref[:, Z_HGQ:Z_HGQ + GROUP_W] = _silu(z_ref[:, Z_HGQ:Z_HGQ + GROUP_W])
    for m, col in enumerate((Z_DNG, Z_HGG, Z_GLAG)):
        z_ref[:, col:col + GROUP_W] = _silu(z_ref[:, col:col + GROUP_W]) * gn[m:m + 1, :]
    z_ref[:, Z_MLO:Z_MLO + GROUP_W] = _sigmoid(z_ref[:, Z_MLO:Z_MLO + GROUP_W]) * gn[3:4, :]

    kc = dict(masks=masks_ref, e_c=ec_ref, e_d=ed_ref, bd256=bd256_ref, bd128=bd128_ref, bdp=bdp_ref,
              st256=st256_ref, st128=st128_ref, ie256=ie256_ref, ie128=ie128_ref, iep=iep_ref,
              ones256=ones256_ref, mean256=mean256_ref, lmat=lmat_ref)
    run = min(ilv, nchunk)
    assert ilv % run == 0 and nchunk % run == 0

    def chunk_gens(it, u, boxes):
        i = it * ilv + u
        pos = u % run
        if nb == 1:
            b = 0
        elif ilv % nchunk == 0:
            b = it * (ilv // nchunk) + u // nchunk
        else:
            b = i // nchunk
        rs = pl.ds(pl.multiple_of(i * c, c), c)
        zc = lambda off, w: z_ref[rs, off:off + w]

        def getter(key, read):
            def get():
                if pos > 0:
                    return (yield from _await(boxes, (key, u - 1)))
                return read()
                yield
            return get

        def dn():
            small = zc(Z_SMALL, LANES)
            sb = small + sp[0:1, :]
            qkv = qkv_ref[rs, :]
            cq, ck, cv = qkv[:, 0:GROUP_W], qkv[:, GROUP_W:2 * GROUP_W], qkv[:, 2 * GROUP_W:3 * GROUP_W]
            ss = _dot(jnp.concatenate([cq * cq, ck * ck], axis=0), kc['ones256'][...])
            ssq, ssk = ss[0:c], ss[c:2 * c]
            beta_s = _sigmoid(small)
            gam_s = _move_l(kc['lmat'][0:c, :], neg_a * _softplus(sb))
            yield
            dq = cq * lax.rsqrt(ssq + EPS) * (DN_DK ** -0.5)
            dk = ck * lax.rsqrt(ssk + EPS)
            o, st_new = yield from _deltanet_chunk(dq, dk, cv, beta_s, gam_s,
                                                   getter('dn', lambda: sdn_ref[b]), kc, c)
            boxes[('dn', u)] = st_new
            sdn_ref[b] = st_new
            boxes[('out', u, 0)] = o

        def hg():
            o, st_new = yield from _gla_chunk(zc(Z_HGQ, GROUP_W), hgk_ref[rs, :], zc(Z_HGI, GROUP_W),
                                              zc(Z_HGF, GROUP_W), getter('hg', lambda: shg_ref[b]), kc, c, True)
            boxes[('hg', u)] = st_new
            shg_ref[b] = st_new
            boxes[('out', u, 1)] = o

        def gla():
            small = zc(Z_SMALL, LANES)
            g_gla = _log_sigmoid(_dot(small, wup_ref[...]) + bup_ref[...]) * (LOG2E / GLA_TAU)
            yield
            o, st_new = yield from _gla_chunk(zc(Z_GLAQ, GLA_W) * (GLA_DK ** -0.5), zc(Z_GLAK, GLA_W),
                                              zc(Z_GLAV, GROUP_W), g_gla,
                                              getter('gla', lambda: sgla_ref[b]), kc, c, False)
            boxes[('gla', u)] = st_new
            sgla_ref[b] = st_new
            boxes[('out', u, 2)] = o

        def ml():
            sb = zc(Z_SMALL, LANES) + sp[0:1, :]
            ig_s = pltpu.roll(sb, S_MLF - S_MLI, axis=1)
            lf_s = _log_sigmoid(sb)

            def put_m(m_new):
                boxes[('ml_m', u)] = m_new
                mml_ref[b] = m_new

            hh, c_new, n_new = yield from _mlstm_chunk(
                zc(Z_MLQ, GROUP_W) * (ML_DK ** -0.5), zc(Z_MLK, GROUP_W), zc(Z_MLV, GROUP_W), ig_s, lf_s,
                getter('ml_m', lambda: mml_ref[b]), put_m,
                getter('ml_cn', lambda: (cml_ref[b], nml_ref[b])), kc, c)
            boxes[('ml_cn', u)] = (c_new, n_new)
            cml_ref[b] = c_new
            nml_ref[b] = n_new
            boxes[('out', u, 3)] = hh

        late = STAGGER * pos
        return [(0, dn()), (late, hg()), (late, gla()), (late, ml())]

    def chunk_body(it, carry):
        boxes = {}
        gens = []
        for u in range(ilv):
            gens += chunk_gens(it, u, boxes)
        _run_interleaved(gens)
        outs = jnp.concatenate([boxes[('out', u, m)] for u in range(ilv) for m in range(4)], axis=0)
        ms = _dot(outs * outs, kc['mean256'][...])
        normed = outs * lax.rsqrt(ms + EPS)
        gate_cols = (Z_DNG, Z_HGG, Z_GLAG, Z_MLO)
        for u in range(ilv):
            rs = pl.ds(pl.multiple_of((it * ilv + u) * c, c), c)
            for m in range(4):
                gate = z_ref[rs, gate_cols[m]:gate_cols[m] + GROUP_W]
                r0 = (u * 4 + m) * c
                mix_ref[rs, m * GROUP_W:(m + 1) * GROUP_W] = (normed[r0:r0 + c] * gate).astype(BF16)
        return carry

    lax.fori_loop(0, nb * nchunk // ilv, chunk_body, 0)

    @pl.when(ti == pl.num_programs(1) - 1)
    def _():
        for _, out_ref, st_ref, sel_ref in mats:
            for b in range(nb):
                for hd in range(N_HEADS):
                    blk = st_ref[b, hd * HEAD_V:(hd + 1) * HEAD_V, :]
                    out_ref[b, hd] = _move_nt(blk, sel_ref[hd]) if st_ref is sdn_ref else _move_nt(sel_ref[hd], blk)

    out = jnp.dot(mix_ref[...], wout_ref[...], preferred_element_type=F32).reshape(nb, tb, D_MODEL)
    y_ref[...] = x + mod[:, 2:3, :] * out


def _const_tables(c):
    pc = N_HEADS * c
    t = np.arange(c)[:, None]
    lane = np.arange(pc)[None, :]
    hs, s = lane // c, lane % c
    masks = [s <= t, s < t, s == t, (s // TRI_BLOCK) == (t // TRI_BLOCK)]
    sz = c // 2
    while sz >= 1:
        masks.append(((s // (2 * sz)) == (t // (2 * sz))) & ((t // sz) % 2 == 1) & ((s // sz) % 2 == 0))
        sz //= 2
    masks = np.stack([np.broadcast_to(m, (c, pc)) for m in masks]).astype(np.float32)
    r = np.arange(c)[None, :]
    tri = (r <= t).astype(np.float32)
    lmat = np.concatenate([tri, tri - tri[(np.arange(c) // 4) * 4 + 1]], axis=0)

    def expand(col0, w):
        j = np.arange(LANES)[:, None]
        l = np.arange(N_HEADS * w)[None, :]
        return (j == col0 + l // w).astype(np.float32)

    e_c = np.stack([expand(S_DNB, c), expand(S_DNA, c), expand(S_MLF, c)])
    e_d = np.stack([expand(S_DNB, HEAD_V), expand(S_DNA, HEAD_V), expand(S_MLF, HEAD_V)])

    def headsel(dk):
        d = np.arange(dk)[None, :, None]
        l = np.arange(N_HEADS * dk)[None, None, :]
        return (l == np.arange(N_HEADS)[:, None, None] * dk + d).astype(np.float32)

    def blk(nr, rg, nl, lg):
        return ((np.arange(nr)[:, None] // rg) == (np.arange(nl)[None, :] // lg)).astype(np.float32)

    tabs = dict(
        masks=jnp.asarray(masks), e_c=jnp.asarray(e_c, BF16), e_d=jnp.asarray(e_d, BF16),
        bd256=jnp.asarray(blk(pc, c, GROUP_W, HEAD_V), BF16), bd128=jnp.asarray(blk(pc, c, GLA_W, GLA_DK), BF16),
        bdp=jnp.asarray(blk(pc, c, pc, c), BF16),
        st256=jnp.asarray(blk(GROUP_W, HEAD_V, GROUP_W, HEAD_V)), st128=jnp.asarray(blk(GROUP_W, HEAD_V, GLA_W, GLA_DK)),
        ie256=jnp.asarray(blk(GROUP_W, HEAD_V, pc, c), BF16), ie128=jnp.asarray(blk(GLA_W, GLA_DK, pc, c), BF16),
        iep=jnp.asarray(blk(pc, c, GROUP_W, HEAD_V), BF16),
        ones256=jnp.asarray(blk(GROUP_W, HEAD_V, GROUP_W, HEAD_V), BF16),
        mean256=jnp.asarray(blk(GROUP_W, HEAD_V, GROUP_W, HEAD_V) / HEAD_V, BF16), lmat=jnp.asarray(lmat, BF16),
        sel64=jnp.asarray(headsel(HEAD_V), BF16), sel32=jnp.asarray(headsel(GLA_DK), BF16))
    order = ['masks', 'e_c', 'e_d', 'bd256', 'bd128', 'bdp', 'st256', 'st128', 'ie256', 'ie128', 'iep', 'ones256',
             'mean256', 'lmat', 'sel64', 'sel32']
    return [tabs[k] for k in order]


def _full_spec(a):
    nd = a.ndim
    return pl.BlockSpec(a.shape, lambda bi, ti, _n=nd: (0,) * _n, pipeline_mode=pl.Buffered(1))


def _mixer_call(x, mod, lw, states, nb, tb):
    bsz, t, _ = x.shape
    c = min(CHUNK, t)
    assert t % tb == 0 and tb % c == 0 and bsz % nb == 0 and c % TRI_BLOCK == 0
    rows = nb * tb
    consts = _const_tables(c)
    params = [lw['g_mix'], lw['w_in'], lw['w_out'], lw['conv_w'], lw['sp'], lw['gn'], lw['lb_logits'], lw['lb_sel'],
              lw['wup'], lw['bup']]
    xspec = pl.BlockSpec((nb, tb, D_MODEL), lambda bi, ti: (bi, ti, 0))

    def bspec(a):
        nd = a.ndim
        return pl.BlockSpec((nb,) + a.shape[1:], lambda bi, ti, _n=nd: (bi,) + (0,) * (_n - 1))

    in_specs = ([xspec, bspec(mod)] + [_full_spec(a) for a in params] + [_full_spec(a) for a in consts]
                + [bspec(s) for s in states])
    out_shape = [jax.ShapeDtypeStruct(x.shape, F32)] + [jax.ShapeDtypeStruct(s.shape, F32) for s in states]
    out_specs = [xspec] + [bspec(s) for s in states]
    scratch = [pltpu.VMEM((rows, N_IN), F32),
               pltpu.VMEM((nb, SUB + tb + SUB, DN_CONV_CH), F32),
               pltpu.VMEM((rows, DN_CONV_CH), F32),
               pltpu.VMEM((rows, MIX_W), BF16),
               pltpu.VMEM((rows, N_HEADS * HG_DK), F32),
               pltpu.VMEM((nb, GROUP_W, N_HEADS * DN_DK), F32),
               pltpu.VMEM((nb, GROUP_W, N_HEADS * HG_DK), F32),
               pltpu.VMEM((nb, GROUP_W, GLA_W), F32),
               pltpu.VMEM((nb, GROUP_W, N_HEADS * ML_DK), F32)]
    ilv = MAX_INTERLEAVE
    while (nb * (tb // c)) % ilv:
        ilv //= 2
    kern = functools.partial(_mixer_kernel, nb=nb, tb=tb, c=c, ilv=ilv)
    return pl.pallas_call(
        kern, grid=(bsz // nb, t // tb), in_specs=in_specs, out_specs=out_specs, out_shape=out_shape,
        scratch_shapes=scratch, name='mixer',
        compiler_params=pltpu.CompilerParams(dimension_semantics=('arbitrary', 'arbitrary'),
                                             vmem_limit_bytes=VMEM_LIMIT),
    )(x, mod, *params, *consts, *states)


FF_TILE = 256


def _ffn_kernel(x_ref, mod_ref, gffn_ref, wup_ref, wdown_ref, gfin_ref, y_ref, *, nb, tb, final):
    rows = nb * tb
    x = x_ref[...]
    mod = mod_ref[...]
    h = _rmsnorm_rows(x, gffn_ref[...]) * (1.0 + mod[:, 4:5, :]) + mod[:, 3:4, :]
    hb = h.reshape(rows, D_MODEL).astype(BF16)
    acc = jnp.zeros((rows, D_MODEL), F32)
    for j in range(D_FF // FF_TILE):
        gate = jnp.dot(hb, wup_ref[:, j * FF_TILE:(j + 1) * FF_TILE], preferred_element_type=F32)
        up = jnp.dot(hb, wup_ref[:, D_FF + j * FF_TILE:D_FF + (j + 1) * FF_TILE], preferred_element_type=F32)
        act = (_silu(gate) * up).astype(BF16)
        acc = acc + jnp.dot(act, wdown_ref[j * FF_TILE:(j + 1) * FF_TILE, :], preferred_element_type=F32)
    y = x + mod[:, 5:6, :] * acc.reshape(nb, tb, D_MODEL)
    if final:
        y = _rmsnorm_rows(y, gfin_ref[...])
    y_ref[...] = y


def _ffn_call(x, mod, lw, g_final, nb, tb, final):
    bsz, t, _ = x.shape
    xspec = pl.BlockSpec((nb, tb, D_MODEL), lambda bi, ti: (bi, ti, 0))
    mspec = pl.BlockSpec((nb, 6, D_MODEL), lambda bi, ti: (bi, 0, 0))
    params = [lw['g_ffn'], lw['w_up'], lw['w_down'], g_final]
    kern = functools.partial(_ffn_kernel, nb=nb, tb=tb, final=final)
    return pl.pallas_call(
        kern, grid=(bsz // nb, t // tb), in_specs=[xspec, mspec] + [_full_spec(a) for a in params],
        out_specs=xspec, out_shape=jax.ShapeDtypeStruct(x.shape, F32), name='ffn',
        compiler_params=pltpu.CompilerParams(dimension_semantics=('arbitrary', 'arbitrary'),
                                             vmem_limit_bytes=VMEM_LIMIT),
    )(x, mod, *params)


ADA_TILE = 1536


def _ada_kernel(c_ref, w_ref, b_ref, o_ref):
    o_ref[0] = jnp.dot(_silu(c_ref[...]).astype(BF16), w_ref[0], preferred_element_type=F32) + b_ref[0]


def _ada_call(c_all, w_ada, b_ada):
    n = c_all.shape[0]
    nt = 6 * D_MODEL // ADA_TILE
    return pl.pallas_call(
        _ada_kernel, grid=(DEPTH, nt),
        in_specs=[pl.BlockSpec((n, D_MODEL), lambda l, j: (0, 0)),
                  pl.BlockSpec((1, D_MODEL, ADA_TILE), lambda l, j: (l, 0, j)),
                  pl.BlockSpec((1, 1, ADA_TILE), lambda l, j: (l, 0, j))],
        out_specs=pl.BlockSpec((1, n, ADA_TILE), lambda l, j: (l, 0, j)),
        out_shape=jax.ShapeDtypeStruct((DEPTH, n, 6 * D_MODEL), F32), name='ada',
        compiler_params=pltpu.CompilerParams(dimension_semantics=('arbitrary', 'arbitrary')),
    )(c_all, w_ada, b_ada.reshape(DEPTH, 1, 6 * D_MODEL))


_REF_SPLITS = (('dn_qkv', DN_CONV_CH), ('dn_b', N_HEADS), ('dn_a', N_HEADS), ('dn_g', GROUP_W),
               ('hg_q', GROUP_W), ('hg_f', GROUP_W), ('hg_i', GROUP_W), ('hg_g', GROUP_W),
               ('gla_q', GLA_W), ('gla_k', GLA_W), ('gla_v', GROUP_W), ('gla_r', GLA_RANK), ('gla_g', GROUP_W),
               ('ml_q', GROUP_W), ('ml_k', GROUP_W), ('ml_v', GROUP_W), ('ml_i', N_HEADS), ('ml_f', N_HEADS),
               ('ml_o', GROUP_W))


def _permute_w_in(w):
    d = w.shape[0]
    cols, off = {}, 0
    for name, n in _REF_SPLITS:
        cols[name] = w[:, off:off + n]
        off += n
    assert off == w.shape[1]
    pieces, lane = [], 0
    for name, start in (('dn_b', S_DNB), ('gla_r', S_GLAR), ('dn_a', S_DNA), ('ml_i', S_MLI), ('ml_f', S_MLF)):
        pieces += [jnp.zeros((d, start - lane), w.dtype), cols[name]]
        lane = start + cols[name].shape[1]
    small = jnp.concatenate(pieces + [jnp.zeros((d, LANES - lane), w.dtype)], axis=1)
    cols['small'] = small
    order = ('dn_qkv', 'small', 'hg_f', 'hg_q', 'dn_g', 'hg_g', 'gla_g', 'ml_o',
             'hg_i', 'gla_q', 'gla_k', 'gla_v', 'ml_q', 'ml_k', 'ml_v')
    out = jnp.concatenate([cols[k] for k in order], axis=1)
    assert out.shape[1] == N_IN
    return out


def _lane_row(pairs, width=LANES):
    row = jnp.zeros((width,), F32)
    for off, val in pairs:
        row = row.at[off:off + val.shape[0]].set(val.astype(F32))
    return row


def _layer_weights(p, l):
    sp = jnp.zeros((SUB, LANES), F32)
    sp = sp.at[0].set(_lane_row([(S_DNA, p['dn_dt_bias'][l]), (S_MLI, p['ml_i_bias'][l]), (S_MLF, p['ml_f_bias'][l])]))
    sp = sp.at[1].set(_lane_row([(S_DNA, p['dn_a_log'][l])]))
    gn = jnp.stack([jnp.tile(p[k][l].astype(F32), N_HEADS) for k in ('dn_norm_g', 'hg_norm_g', 'gla_norm_g', 'ml_norm_g')])
    wup = jnp.zeros((LANES, GLA_W), F32).at[S_GLAR:S_GLAR + GLA_RANK].set(p['gla_w_up'][l]).astype(BF16)
    lb_sel = (jnp.arange(DEPTH) >= 1) & (jnp.arange(DEPTH) <= l)
    return dict(
        g_mix=p['g_mix'][l].reshape(1, D_MODEL), g_ffn=p['g_ffn'][l].reshape(1, D_MODEL),
        w_in=_permute_w_in(p['w_in'][l]).astype(BF16), w_out=p['w_out'][l].astype(BF16),
        conv_w=p['dn_conv_w'][l], sp=sp, gn=gn, lb_logits=p['hg_lb_logits'].astype(F32),
        lb_sel=lb_sel.astype(F32).reshape(DEPTH, 1), wup=wup, bup=p['gla_b_up'][l].reshape(1, GLA_W).astype(F32),
        w_up=p['w_up'][l].astype(BF16), w_down=p['w_down'][l].astype(BF16))


def _trunk(x, mods, states, lws, g_final, nb, tb):
    new_states = []
    for l in range(DEPTH):
        outs = _mixer_call(x, mods[l], lws[l], states[l], nb, tb)
        x = outs[0]
        new_states.append(outs[1:])
        x = _ffn_call(x, mods[l], lws[l], g_final, *_tiling(x.shape[0], x.shape[1], FFN_ROWS, FFN_ROWS), l == DEPTH - 1)
    return x, new_states


def _pack_states(conv, s_dn, s_hg, s_gla, c_ml, n_ml, m_ml, l):
    b = conv.shape[1]
    m_row = jnp.zeros((b, 1, LANES), F32).at[:, 0, S_MLF:S_MLF + N_HEADS].set(m_ml[l].astype(F32))
    return (conv[l].astype(F32), s_dn[l].astype(F32), s_hg[l].astype(F32), s_gla[l].astype(F32),
            c_ml[l].astype(F32), n_ml[l].astype(F32).reshape(b, 1, N_HEADS * ML_DK), m_row)


def _unpack_states(sts):
    conv, s_dn, s_hg, s_gla, c_ml = (jnp.stack([s[i] for s in sts]) for i in range(5))
    n_ml = jnp.stack([s[5].reshape(s[5].shape[0], N_HEADS, ML_DK) for s in sts])
    m_ml = jnp.stack([s[6][:, 0, S_MLF:S_MLF + N_HEADS] for s in sts])
    return conv, s_dn, s_hg, s_gla, c_ml, n_ml, m_ml


def _zero_states(b):
    z = lambda *s: jnp.zeros(s, F32)
    return (z(b, DN_CONV - 1, DN_CONV_CH), z(b, N_HEADS, DN_DK, HEAD_V), z(b, N_HEADS, HG_DK, HEAD_V),
            z(b, N_HEADS, GLA_DK, HEAD_V), z(b, N_HEADS, ML_DK, HEAD_V), z(b, 1, N_HEADS * ML_DK), z(b, 1, LANES))


def _tiling(bsz, t, rows, tokens):
    tb = min(t, tokens)
    nb = max(1, min(bsz, rows // tb))
    while bsz % nb:
        nb -= 1
    return nb, tb


def kernel(x_prompt, x_sample, c_prompt, c_sample, cache_dn_conv, state_dn, state_hgrn, state_gla, state_mlstm_c, state_mlstm_n, state_mlstm_m, w_ada, b_ada, g_mix, g_ffn, w_in, dn_conv_w, dn_a_log, dn_dt_bias, dn_norm_g, hg_lb_logits, hg_norm_g, gla_w_up, gla_b_up, gla_norm_g, ml_i_bias, ml_f_bias, ml_norm_g, w_out, w_up, w_down, g_final):
    p = dict(g_mix=g_mix, g_ffn=g_ffn, w_in=w_in, dn_conv_w=dn_conv_w, dn_a_log=dn_a_log, dn_dt_bias=dn_dt_bias,
             dn_norm_g=dn_norm_g, hg_lb_logits=hg_lb_logits, hg_norm_g=hg_norm_g, gla_w_up=gla_w_up,
             gla_b_up=gla_b_up, gla_norm_g=gla_norm_g, ml_i_bias=ml_i_bias, ml_f_bias=ml_f_bias,
             ml_norm_g=ml_norm_g, w_out=w_out, w_up=w_up, w_down=w_down)
    lws = [_layer_weights(p, l) for l in range(DEPTH)]
    gfin = g_final.reshape(1, D_MODEL).astype(F32)
    bp, bs = x_prompt.shape[0], x_sample.shape[0]
    mod = _ada_call(jnp.concatenate([c_prompt, c_sample], axis=0).astype(F32), w_ada.astype(BF16),
                    b_ada.astype(F32)).reshape(DEPTH, bp + bs, 6, D_MODEL)

    outs = []
    raw = (cache_dn_conv, state_dn, state_hgrn, state_gla, state_mlstm_c, state_mlstm_n, state_mlstm_m)
    for x, lo, hi, states in ((x_prompt, 0, bp, [_zero_states(bp)] * DEPTH),
                              (x_sample, bp, bp + bs, [_pack_states(*raw, l) for l in range(DEPTH)])):
        nb, tb = _tiling(x.shape[0], x.shape[1], MIXER_ROWS, MIXER_TOKENS)
        y, new = _trunk(x.astype(F32), [mod[l, lo:hi] for l in range(DEPTH)], states, lws, gfin, nb, tb)
        outs.append((y, _unpack_states(new)))
    (y_p, st_p), (y_s, st_s) = outs
    return (y_p, y_s) + tuple(st_p) + tuple(st_s)
```

```python
import functools

import numpy as np
import jax
import jax.numpy as jnp
from jax import lax
from jax.experimental import pallas as pl
from jax.experimental.pallas import tpu as pltpu

F32 = jnp.float32
BF16 = jnp.bfloat16

D_MODEL = 1024
DEPTH = 2
CHUNK = 64
N_HEADS = 4
HEAD_V = 64
GROUP_W = N_HEADS * HEAD_V
MIX_W = 4 * GROUP_W
DN_DK = 64
DN_CONV = 4
DN_CONV_CH = 3 * GROUP_W
HG_DK = 64
GLA_DK = 32
GLA_W = N_HEADS * GLA_DK
GLA_RANK = 16
GLA_TAU = 16.0
ML_DK = 64
D_FF = 2816
EPS = 1e-6

Z_DNQKV = 0
Z_SMALL = 768
Z_HGF = 896
Z_HGQ = 1152
Z_DNG = 1408
Z_HGG = 1664
Z_GLAG = 1920
Z_MLO = 2176
N_EARLY = 2432
Z_HGI = 2432
Z_GLAQ = 2688
Z_GLAK = 2816
Z_GLAV = 2944
Z_MLQ = 3200
Z_MLK = 3456
Z_MLV = 3712
N_IN = 3968
S_DNB = 0
S_GLAR = 16
S_DNA = 32
S_MLI = 64
S_MLF = 96

SUB = 8
LANES = 128
TRI_BLOCK = 16
VMEM_LIMIT = 56 * 1024 * 1024

M_INCL, M_STRICT, M_DIAG, M_SAMEBLK = 0, 1, 2, 3
M_LEVEL0 = 4
LOG2E = 1.4426950408889634
MIXER_ROWS = 256
MIXER_TOKENS = 256
FFN_ROWS = 1024
STAGGER = 4
MAX_INTERLEAVE = 4


def _sigmoid(x):
    return 1.0 / (1.0 + jnp.exp(-x))


def _silu(x):
    return x * _sigmoid(x)


def _log_sigmoid(x):
    return jnp.minimum(x, 0.0) - jnp.log1p(jnp.exp(-jnp.abs(x)))


def _softplus(x):
    return jnp.maximum(x, 0.0) + jnp.log1p(jnp.exp(-jnp.abs(x)))


def _dot(a, b):
    return jnp.dot(a.astype(BF16), b.astype(BF16), preferred_element_type=F32)


def _dot_nt(a, b):
    return lax.dot_general(a.astype(BF16), b.astype(BF16), (((1,), (1,)), ((), ())),
                           preferred_element_type=F32)


def _split3(x):
    x1 = x.astype(BF16)
    r = x - x1.astype(F32)
    x2 = r.astype(BF16)
    r = r - x2.astype(F32)
    return x1, x2, r.astype(BF16)


def _move_rows(xs, sel):
    parts = [_split3(x) for x in xs]
    y = jnp.dot(jnp.concatenate([p[i] for i in range(3) for p in parts], axis=0), sel, preferred_element_type=F32)
    n = sum(x.shape[0] for x in xs)
    outs = []
    off = 0
    for x in xs:
        r = x.shape[0]
        outs.append(y[off:off + r] + (y[n + off:n + off + r] + y[2 * n + off:2 * n + off + r]))
        off += r
    return outs


def _move_r(x, sel):
    return _move_rows([x], sel)[0]


def _move_l(sel, x):
    x1, x2, x3 = _split3(x)
    d = lambda a: jnp.dot(sel, a, preferred_element_type=F32)
    return d(x1) + (d(x2) + d(x3))


def _bd(x, mask):
    xb = x.astype(BF16)
    return jnp.concatenate([xb] * N_HEADS, axis=0) * mask


def _scan0(x, op, fill):
    n = x.shape[0]
    row = lax.broadcasted_iota(jnp.int32, x.shape, 0)
    sh = 1
    while sh < n:
        r = pltpu.roll(x, sh, axis=0)
        x = op(x, jnp.where(row >= sh, r, fill))
        sh *= 2
    return x


def _rowform(xe, diag):
    return jnp.sum(xe * diag, axis=0, keepdims=True)


def _rmsnorm_rows(x, g):
    return x * lax.rsqrt(jnp.mean(x * x, axis=-1, keepdims=True) + EPS) * g


def _run_interleaved(gens):
    live = list(gens)
    rnd = 0
    while live:
        alive = []
        for start, g in live:
            if rnd >= start:
                try:
                    next(g)
                except StopIteration:
                    continue
            alive.append((start, g))
        live = alive
        rnd += 1


def _await(boxes, key):
    while key not in boxes:
        yield
    return boxes[key]


def _tri_solve(mm, rhs, masks, bdp, bdr, c):
    assert c // TRI_BLOCK <= 4
    eye = masks[M_DIAG]
    mul = lambda a, b: _dot(a, _bd(b, bdp))
    app = lambda a, r: _dot(a, _bd(r, bdr))
    md = mm * masks[M_SAMEBLK]
    mo = mm - md
    p2 = mul(md, md)
    yield
    d = eye - md
    d = d + mul(d, p2)
    p4 = mul(p2, p2)
    yield
    d = d + mul(d, p4)
    p8 = mul(p4, p4)
    yield
    d = d + mul(d, p8)
    yield
    n = mul(d, mo)
    ys = [app(d, r) for r in rhs]
    yield
    zs = [y - app(n, y) for y in ys]
    if c // TRI_BLOCK <= 2:
        yield
        return zs
    n2 = mul(n, n)
    yield
    ws = [z + app(n2, z) for z in zs]
    yield
    return ws


def _deltanet_chunk(q, k, v, beta_s, gam_s, get_state, kc, c):
    masks = kc['masks']
    gam_e = _move_r(gam_s, kc['e_c'][1])
    beta_e = _move_r(beta_s, kc['e_c'][0])
    if c == HEAD_V:
        gam_d, beta_d = gam_e, beta_e
    else:
        gam_d = _move_r(gam_s, kc['e_d'][1])
        beta_d = _move_r(beta_s, kc['e_d'][0])
    kq = _dot_nt(jnp.concatenate([k, q], axis=0), _bd(k, kc['bd256'][...]))
    kk, qk = kq[0:c], kq[c:2 * c]
    yield
    gam_r = _rowform(gam_e, masks[M_DIAG])
    dec = jnp.exp(jnp.minimum(gam_e - gam_r, 0.0))
    mm = beta_e * kk * dec * masks[M_STRICT]
    eg = jnp.exp(gam_d)
    w, u0 = yield from _tri_solve(mm, [beta_d * eg * k, beta_d * v], masks, kc['bdp'][...], kc['bd256'][...], c)
    gl = gam_d[c - 1:c, :]
    kdec_t = (k * jnp.exp(gl - gam_d)).T
    st = yield from get_state()
    wq = _dot(jnp.concatenate([w, q * eg], axis=0), st)
    u = u0 - wq[0:c]
    qs = wq[c:2 * c]
    yield
    st_new = jnp.exp(gl) * st + kc['st256'][...] * _dot(kdec_t, u)
    o = qs + _dot(qk * dec * masks[M_INCL], _bd(u, kc['bd256'][...]))
    return o, st_new


def _block_ref(b, sz):
    c, w = b.shape
    g3 = b.reshape(c // (2 * sz), 2 * sz, w)
    return jnp.broadcast_to(g3[:, sz - 1:sz, :], g3.shape).reshape(c, w)


def _gla_chunk(q, k, v, g2, get_state, kc, c, wide):
    masks = kc['masks']
    bdk = kc['bd256'][...] if wide else kc['bd128'][...]
    ie = kc['ie256'][...] if wide else kc['ie128'][...]
    stm = kc['st256'][...] if wide else kc['st128'][...]
    v_t = v.T
    bd2 = _move_l(kc['lmat'][...], g2)
    b = bd2[0:c, :]
    yield
    row = lax.broadcasted_iota(jnp.int32, b.shape, 0)
    x1 = jnp.where(jnp.bitwise_and(row, 1) == 1, q * pltpu.roll(k, 1, axis=0) * jnp.exp2(g2), 0.0)
    dd = _dot(jnp.concatenate([q * k, x1], axis=0), ie)
    nlvl = c.bit_length() - 1
    attn = dd[0:c] * masks[M_DIAG] + dd[c:2 * c] * masks[M_LEVEL0 + nlvl - 1]
    lvl = 0
    sz = c // 2
    while sz >= 2:
        d = b - _block_ref(b, sz) if sz >= 4 else bd2[c:2 * c, :]
        e = jnp.exp2(jnp.minimum(d, -d))
        attn = attn + _dot_nt(q * e, _bd(k * e, bdk)) * masks[M_LEVEL0 + lvl]
        yield
        sz //= 2
        lvl += 1
    last = b[c - 1:c, :]
    upd = stm * _dot(v_t, k * jnp.exp2(last - b))
    o = _dot(attn, _bd(v, kc['bd256'][...]))
    yield
    st = yield from get_state()
    st_new = jnp.exp2(last) * st + upd
    o = o + _dot_nt(q * jnp.exp2(b), st)
    return o, st_new


def _mlstm_chunk(q, k, v, ig_s, lf_s, get_m, put_m, get_cn, kc, c):
    masks = kc['masks']
    fcum = _move_l(kc['lmat'][0:c, :], lf_s)
    a = ig_s - fcum
    imax = fcum + _scan0(a, jnp.maximum, -jnp.inf)
    fl = fcum[c - 1:c, :]
    lw = fl - fcum + ig_s
    lw_max = jnp.max(lw, axis=0, keepdims=True)
    kb = _bd(k, kc['bd256'][...])
    qk = _dot_nt(q, kb)
    a_e = _move_r(a, kc['e_c'][2])
    a_r = _rowform(a_e, masks[M_DIAG])
    yield
    m_row = yield from get_m()
    m_new = jnp.maximum(fl + m_row, lw_max)
    put_m(m_new)
    mt = jnp.maximum(fcum + m_row, imax)
    rows = lambda r: jnp.broadcast_to(r, (2 * SUB, LANES))
    consts = [rows(fl + m_row - m_new), rows(m_row), rows(fl - m_new)]
    x1 = fcum - mt
    if c == HEAD_V:
        x1_e, lfl, lwc, m_d, sh_d = _move_rows([x1, -mt] + consts, kc['e_d'][2])
        x1_d, a_d = x1_e, a_e
    else:
        x1_e = _move_r(x1, kc['e_c'][2])
        x1_d, lfl, a_d, lwc, m_d, sh_d = _move_rows([x1, -mt, a] + consts, kc['e_d'][2])
    w_inter = jnp.exp(x1_d + m_d[0:1, :])
    floor = jnp.exp(lfl)
    ws = jnp.exp(a_d + sh_d[0:1, :])
    wc = jnp.exp(lwc)[0:1, :]
    wsv_t = (ws * v).T
    yield
    w_intra = jnp.exp(jnp.minimum(x1_e + a_r, 0.0)) * masks[M_INCL] * qk
    num = _dot(w_intra, _bd(v, kc['bd256'][...]))
    den = _dot(w_intra, kc['iep'][...])
    upd = kc['st256'][...] * _dot(wsv_t, k)
    n_upd = jnp.sum(ws * k, axis=0, keepdims=True)
    yield
    ct, n_row = yield from get_cn()
    ct_new = wc * ct + upd
    n_new = wc * n_row + n_upd
    num = num + w_inter * _dot_nt(q, ct)
    den = den + w_inter * _dot(q * n_row, kc['ones256'][...])
    hh = num / jnp.maximum(jnp.abs(den), floor)
    return hh, ct_new, n_new


def _mixer_kernel(x_ref, mod_ref, gmix_ref, win_ref, wout_ref, convw_ref, sp_ref, gn_ref, lbl_ref, lbs_ref,
                  wup_ref, bup_ref,
                  masks_ref, ec_ref, ed_ref, bd256_ref, bd128_ref, bdp_ref, st256_ref, st128_ref,
                  ie256_ref, ie128_ref, iep_ref, ones256_ref, mean256_ref, lmat_ref,
                  conv0_ref, sdn0_ref, shg0_ref, sgla0_ref, c0_ref, n0_ref, m0_ref,
                  y_ref, convo_ref, sdno_ref, shgo_ref, sglao_ref, cmlo_ref, nml_ref, mml_ref,
                  z_ref, xp_ref, qkv_ref, mix_ref, hgk_ref, sdn_ref, shg_ref, sgla_ref, cml_ref, *, nb, tb, c, ilv):
    ti = pl.program_id(1)
    rows = nb * tb
    nchunk = tb // c
    mats = ((sdn0_ref, sdno_ref, sdn_ref), (shg0_ref, shgo_ref, shg_ref),
            (sgla0_ref, sglao_ref, sgla_ref), (c0_ref, cmlo_ref, cml_ref))

    @pl.when(ti == 0)
    def _():
        convo_ref[...] = conv0_ref[...]
        nml_ref[...] = n0_ref[...]
        mml_ref[...] = m0_ref[...]
        for raw_ref, _, st_ref in mats:
            dk = raw_ref.shape[2]
            stm = st256_ref[...] if dk == HEAD_V else st128_ref[...]
            for b in range(nb):
                raw2d = raw_ref[b].reshape(N_HEADS * dk, HEAD_V)
                if st_ref is sdn_ref:
                    st_ref[b] = jnp.concatenate([raw2d] * N_HEADS, axis=1) * stm
                else:
                    st_ref[b] = jnp.concatenate([raw2d.T] * N_HEADS, axis=0) * stm

    x = x_ref[...]
    mod = mod_ref[...]
    h = _rmsnorm_rows(x, gmix_ref[...]) * (1.0 + mod[:, 1:2, :]) + mod[:, 0:1, :]
    hb = h.reshape(rows, D_MODEL).astype(BF16)
    z_ref[:, 0:N_EARLY] = jnp.dot(hb, win_ref[:, 0:N_EARLY], preferred_element_type=F32)
    z_ref[:, N_EARLY:N_IN] = jnp.dot(hb, win_ref[:, N_EARLY:N_IN], preferred_element_type=F32)

    convw = convw_ref[...]
    for b in range(nb):
        xp_ref[b, SUB - (DN_CONV - 1):SUB, :] = convo_ref[b]
        xp_ref[b, SUB:SUB + tb, :] = z_ref[b * tb:(b + 1) * tb, Z_DNQKV:Z_DNQKV + DN_CONV_CH]
        acc = xp_ref[b, SUB - 3:SUB - 3 + tb, :] * convw[0:1, :]
        for j in range(1, DN_CONV):
            acc = acc + xp_ref[b, SUB - 3 + j:SUB - 3 + j + tb, :] * convw[j:j + 1, :]
        qkv_ref[b * tb:(b + 1) * tb, :] = _silu(acc)
        convo_ref[b] = xp_ref[b, SUB + tb - (DN_CONV - 1):SUB + tb, :]

    sp = sp_ref[...]
    gn = gn_ref[...]
    lbl = lbl_ref[...]
    lbs = lbs_ref[...]
    lbe = jnp.exp(lbl - jnp.max(lbl, axis=0, keepdims=True))
    lb = jnp.sum(lbs * (lbe / jnp.sum(lbe, axis=0, keepdims=True)), axis=0, keepdims=True)
    log_lb = jnp.log(lb)
    log_1mlb = jnp.log1p(-lb)
    neg_a = -jnp.exp(sp[1:2, :])

    zf = z_ref[:, Z_HGF:Z_HGF + GROUP_W]
    t2 = log_1mlb + _log_sigmoid(zf)
    mx = jnp.maximum(log_lb, t2)
    z_ref[:, Z_HGF:Z_HGF + GROUP_W] = (mx + jnp.log(jnp.exp(log_lb - mx) + jnp.exp(t2 - mx))) * LOG2E
    hgk_ref[...] = (1.0 - lb) * _sigmoid(-zf)
    z_ref[:, Z_HGQ:Z_HGQ + GROUP_W] = _silu(z_ref[:, Z_HGQ:Z_HGQ + GROUP_W])
    for m, col in enumerate((Z_DNG, Z_HGG, Z_GLAG)):
        z_ref[:, col:col + GROUP_W] = _silu(z_ref[:, col:col + GROUP_W]) * gn[m:m + 1, :]
    z_ref[:, Z_MLO:Z_MLO + GROUP_W] = _sigmoid(z_ref[:, Z_MLO:Z_MLO + GROUP_W]) * gn[3:4, :]

    kc = dict(masks=masks_ref, e_c=ec_ref, e_d=ed_ref, bd256=bd256_ref, bd128=bd128_ref, bdp=bdp_ref,
              st256=st256_ref, st128=st128_ref, ie256=ie256_ref, ie128=ie128_ref, iep=iep_ref,
              ones256=ones256_ref, mean256=mean256_ref, lmat=lmat_ref)
    run = min(ilv, nchunk)
    assert ilv % run == 0 and nchunk % run == 0

    def chunk_gens(it, u, boxes):
        i = it * ilv + u
        pos = u % run
        if nb == 1:
            b = 0
        elif ilv % nchunk == 0:
            b = it * (ilv // nchunk) + u // nchunk
        else:
            b = i // nchunk
        rs = pl.ds(pl.multiple_of(i * c, c), c)
        zc = lambda off, w: z_ref[rs, off:off + w]

        def getter(key, read):
            def get():
                if pos > 0:
                    return (yield from _await(boxes, (key, u - 1)))
                return read()
                yield
            return get

        def dn():
            small = zc(Z_SMALL, LANES)
            sb = small + sp[0:1, :]
            qkv = qkv_ref[rs, :]
            cq, ck, cv = qkv[:, 0:GROUP_W], qkv[:, GROUP_W:2 * GROUP_W], qkv[:, 2 * GROUP_W:3 * GROUP_W]
            ss = _dot(jnp.concatenate([cq * cq, ck * ck], axis=0), kc['ones256'][...])
            ssq, ssk = ss[0:c], ss[c:2 * c]
            beta_s = _sigmoid(small)
            gam_s = _move_l(kc['lmat'][0:c, :], neg_a * _softplus(sb))
            yield
            dq = cq * lax.rsqrt(ssq + EPS) * (DN_DK ** -0.5)
            dk = ck * lax.rsqrt(ssk + EPS)
            o, st_new = yield from _deltanet_chunk(dq, dk, cv, beta_s, gam_s,
                                                   getter('dn', lambda: sdn_ref[b]), kc, c)
            boxes[('dn', u)] = st_new
            sdn_ref[b] = st_new
            boxes[('out', u, 0)] = o

        def hg():
            o, st_new = yield from _gla_chunk(zc(Z_HGQ, GROUP_W), hgk_ref[rs, :], zc(Z_HGI, GROUP_W),
                                              zc(Z_HGF, GROUP_W), getter('hg', lambda: shg_ref[b]), kc, c, True)
            boxes[('hg', u)] = st_new
            shg_ref[b] = st_new
            boxes[('out', u, 1)] = o

        def gla():
            small = zc(Z_SMALL, LANES)
            g_gla = _log_sigmoid(_dot(small, wup_ref[...]) + bup_ref[...]) * (LOG2E / GLA_TAU)
            yield
            o, st_new = yield from _gla_chunk(zc(Z_GLAQ, GLA_W) * (GLA_DK ** -0.5), zc(Z_GLAK, GLA_W),
                                              zc(Z_GLAV, GROUP_W), g_gla,
                                              getter('gla', lambda: sgla_ref[b]), kc, c, False)
            boxes[('gla', u)] = st_new
            sgla_ref[b] = st_new
            boxes[('out', u, 2)] = o

        def ml():
            sb = zc(Z_SMALL, LANES) + sp[0:1, :]
            ig_s = pltpu.roll(sb, S_MLF - S_MLI, axis=1)
            lf_s = _log_sigmoid(sb)

            def put_m(m_new):
                boxes[('ml_m', u)] = m_new
                mml_ref[b] = m_new

            hh, c_new, n_new = yield from _mlstm_chunk(
                zc(Z_MLQ, GROUP_W) * (ML_DK ** -0.5), zc(Z_MLK, GROUP_W), zc(Z_MLV, GROUP_W), ig_s, lf_s,
                getter('ml_m', lambda: mml_ref[b]), put_m,
                getter('ml_cn', lambda: (cml_ref[b], nml_ref[b])), kc, c)
            boxes[('ml_cn', u)] = (c_new, n_new)
            cml_ref[b] = c_new
            nml_ref[b] = n_new
            boxes[('out', u, 3)] = hh

        late = STAGGER * pos
        return [(0, dn()), (late, hg()), (late, gla()), (late, ml())]

    def chunk_body(it, carry):
        boxes = {}
        gens = []
        for u in range(ilv):
            gens += chunk_gens(it, u, boxes)
        _run_interleaved(gens)
        outs = jnp.concatenate([boxes[('out', u, m)] for u in range(ilv) for m in range(4)], axis=0)
        ms = _dot(outs * outs, kc['mean256'][...])
        normed = outs * lax.rsqrt(ms + EPS)
        gate_cols = (Z_DNG, Z_HGG, Z_GLAG, Z_MLO)
        for u in range(ilv):
            rs = pl.ds(pl.multiple_of((it * ilv + u) * c, c), c)
            for m in range(4):
                gate = z_ref[rs, gate_cols[m]:gate_cols[m] + GROUP_W]
                r0 = (u * 4 + m) * c
                mix_ref[rs, m * GROUP_W:(m + 1) * GROUP_W] = (normed[r0:r0 + c] * gate).astype(BF16)
        return carry

    lax.fori_loop(0, nb * nchunk // ilv, chunk_body, 0)

    @pl.when(ti == pl.num_programs(1) - 1)
    def _():
        for _, out_ref, st_ref in mats:
            dk = out_ref.shape[2]
            for b in range(nb):
                st = st_ref[b]
                if st_ref is sdn_ref:
                    raw2d = sum(st[:, hd * HEAD_V:(hd + 1) * HEAD_V] for hd in range(N_HEADS))
                else:
                    raw2d = sum(st[hd * HEAD_V:(hd + 1) * HEAD_V, :] for hd in range(N_HEADS)).T
                out_ref[b] = raw2d.reshape(N_HEADS, dk, HEAD_V)

    out = jnp.dot(mix_ref[...], wout_ref[...], preferred_element_type=F32).reshape(nb, tb, D_MODEL)
    y_ref[...] = x + mod[:, 2:3, :] * out


def _const_tables(c):
    pc = N_HEADS * c
    t = np.arange(c)[:, None]
    lane = np.arange(pc)[None, :]
    hs, s = lane // c, lane % c
    masks = [s <= t, s < t, s == t, (s // TRI_BLOCK) == (t // TRI_BLOCK)]
    sz = c // 2
    while sz >= 1:
        masks.append(((s // (2 * sz)) == (t // (2 * sz))) & ((t // sz) % 2 == 1) & ((s // sz) % 2 == 0))
        sz //= 2
    masks = np.stack([np.broadcast_to(m, (c, pc)) for m in masks]).astype(np.float32)
    r = np.arange(c)[None, :]
    tri = (r <= t).astype(np.float32)
    lmat = np.concatenate([tri, tri - tri[(np.arange(c) // 4) * 4 + 1]], axis=0)

    def expand(col0, w):
        j = np.arange(LANES)[:, None]
        l = np.arange(N_HEADS * w)[None, :]
        return (j == col0 + l // w).astype(np.float32)

    e_c = np.stack([expand(S_DNB, c), expand(S_DNA, c), expand(S_MLF, c)])
    e_d = np.stack([expand(S_DNB, HEAD_V), expand(S_DNA, HEAD_V), expand(S_MLF, HEAD_V)])

    def blk(nr, rg, nl, lg):
        return ((np.arange(nr)[:, None] // rg) == (np.arange(nl)[None, :] // lg)).astype(np.float32)

    tabs = dict(
        masks=jnp.asarray(masks), e_c=jnp.asarray(e_c, BF16), e_d=jnp.asarray(e_d, BF16),
        bd256=jnp.asarray(blk(pc, c, GROUP_W, HEAD_V), BF16), bd128=jnp.asarray(blk(pc, c, GLA_W, GLA_DK), BF16),
        bdp=jnp.asarray(blk(pc, c, pc, c), BF16),
        st256=jnp.asarray(blk(GROUP_W, HEAD_V, GROUP_W, HEAD_V)), st128=jnp.asarray(blk(GROUP_W, HEAD_V, GLA_W, GLA_DK)),
        ie256=jnp.asarray(blk(GROUP_W, HEAD_V, pc, c), BF16), ie128=jnp.asarray(blk(GLA_W, GLA_DK, pc, c), BF16),
        iep=jnp.asarray(blk(pc, c, GROUP_W, HEAD_V), BF16),
        ones256=jnp.asarray(blk(GROUP_W, HEAD_V, GROUP_W, HEAD_V), BF16),
        mean256=jnp.asarray(blk(GROUP_W, HEAD_V, GROUP_W, HEAD_V) / HEAD_V, BF16), lmat=jnp.asarray(lmat, BF16))
    order = ['masks', 'e_c', 'e_d', 'bd256', 'bd128', 'bdp', 'st256', 'st128', 'ie256', 'ie128', 'iep', 'ones256',
             'mean256', 'lmat']
    return [tabs[k] for k in order]


def _full_spec(a):
    nd = a.ndim
    return pl.BlockSpec(a.shape, lambda bi, ti, _n=nd: (0,) * _n, pipeline_mode=pl.Buffered(1))


def _mixer_call(x, mod, lw, states, nb, tb):
    bsz, t, _ = x.shape
    c = min(CHUNK, t)
    assert t % tb == 0 and tb % c == 0 and bsz % nb == 0 and c % TRI_BLOCK == 0
    rows = nb * tb
    consts = _const_tables(c)
    params = [lw['g_mix'], lw['w_in'], lw['w_out'], lw['conv_w'], lw['sp'], lw['gn'], lw['lb_logits'], lw['lb_sel'],
              lw['wup'], lw['bup']]
    xspec = pl.BlockSpec((nb, tb, D_MODEL), lambda bi, ti: (bi, ti, 0))

    def bspec(a):
        nd = a.ndim
        return pl.BlockSpec((nb,) + a.shape[1:], lambda bi, ti, _n=nd: (bi,) + (0,) * (_n - 1))

    in_specs = ([xspec, bspec(mod)] + [_full_spec(a) for a in params] + [_full_spec(a) for a in consts]
                + [bspec(s) for s in states])
    out_shape = [jax.ShapeDtypeStruct(x.shape, F32)] + [jax.ShapeDtypeStruct(s.shape, F32) for s in states]
    out_specs = [xspec] + [bspec(s) for s in states]
    scratch = [pltpu.VMEM((rows, N_IN), F32),
               pltpu.VMEM((nb, SUB + tb, DN_CONV_CH), F32),
               pltpu.VMEM((rows, DN_CONV_CH), F32),
               pltpu.VMEM((rows, MIX_W), BF16),
               pltpu.VMEM((rows, N_HEADS * HG_DK), F32),
               pltpu.VMEM((nb, GROUP_W, N_HEADS * DN_DK), F32),
               pltpu.VMEM((nb, GROUP_W, N_HEADS * HG_DK), F32),
               pltpu.VMEM((nb, GROUP_W, GLA_W), F32),
               pltpu.VMEM((nb, GROUP_W, N_HEADS * ML_DK), F32)]
    ilv = MAX_INTERLEAVE
    while (nb * (tb // c)) % ilv:
        ilv //= 2
    kern = functools.partial(_mixer_kernel, nb=nb, tb=tb, c=c, ilv=ilv)
    return pl.pallas_call(
        kern, grid=(bsz // nb, t // tb), in_specs=in_specs, out_specs=out_specs, out_shape=out_shape,
        scratch_shapes=scratch, name='mixer',
        compiler_params=pltpu.CompilerParams(dimension_semantics=('arbitrary', 'arbitrary'),
                                             vmem_limit_bytes=VMEM_LIMIT),
    )(x, mod, *params, *consts, *states)


FF_TILE = 256


def _ffn_kernel(x_ref, mod_ref, gffn_ref, wup_ref, wdown_ref, gfin_ref, y_ref, *, nb, tb, final):
    rows = nb * tb
    x = x_ref[...]
    mod = mod_ref[...]
    h = _rmsnorm_rows(x, gffn_ref[...]) * (1.0 + mod[:, 4:5, :]) + mod[:, 3:4, :]
    hb = h.reshape(rows, D_MODEL).astype(BF16)
    acc = jnp.zeros((rows, D_MODEL), F32)
    for j in range(D_FF // FF_TILE):
        gate = jnp.dot(hb, wup_ref[:, j * FF_TILE:(j + 1) * FF_TILE], preferred_element_type=F32)
        up = jnp.dot(hb, wup_ref[:, D_FF + j * FF_TILE:D_FF + (j + 1) * FF_TILE], preferred_element_type=F32)
        act = (_silu(gate) * up).astype(BF16)
        acc = acc + jnp.dot(act, wdown_ref[j * FF_TILE:(j + 1) * FF_TILE, :], preferred_element_type=F32)
    y = x + mod[:, 5:6, :] * acc.reshape(nb, tb, D_MODEL)
    if final:
        y = _rmsnorm_rows(y, gfin_ref[...])
    y_ref[...] = y


def _ffn_call(x, mod, lw, g_final, nb, tb, final):
    bsz, t, _ = x.shape
    xspec = pl.BlockSpec((nb, tb, D_MODEL), lambda bi, ti: (bi, ti, 0))
    mspec = pl.BlockSpec((nb, 6, D_MODEL), lambda bi, ti: (bi, 0, 0))
    params = [lw['g_ffn'], lw['w_up'], lw['w_down'], g_final]
    kern = functools.partial(_ffn_kernel, nb=nb, tb=tb, final=final)
    return pl.pallas_call(
        kern, grid=(bsz // nb, t // tb), in_specs=[xspec, mspec] + [_full_spec(a) for a in params],
        out_specs=xspec, out_shape=jax.ShapeDtypeStruct(x.shape, F32), name='ffn',
        compiler_params=pltpu.CompilerParams(dimension_semantics=('arbitrary', 'arbitrary'),
                                             vmem_limit_bytes=VMEM_LIMIT),
    )(x, mod, *params)


ADA_TILE = 1536


def _ada_kernel(c_ref, w_ref, b_ref, o_ref):
    o_ref[0] = jnp.dot(_silu(c_ref[...]).astype(BF16), w_ref[0], preferred_element_type=F32) + b_ref[0]


def _ada_call(c_all, w_ada, b_ada):
    n = c_all.shape[0]
    nt = 6 * D_MODEL // ADA_TILE
    return pl.pallas_call(
        _ada_kernel, grid=(DEPTH, nt),
        in_specs=[pl.BlockSpec((n, D_MODEL), lambda l, j: (0, 0)),
                  pl.BlockSpec((1, D_MODEL, ADA_TILE), lambda l, j: (l, 0, j)),
                  pl.BlockSpec((1, 1, ADA_TILE), lambda l, j: (l, 0, j))],
        out_specs=pl.BlockSpec((1, n, ADA_TILE), lambda l, j: (l, 0, j)),
        out_shape=jax.ShapeDtypeStruct((DEPTH, n, 6 * D_MODEL), F32), name='ada',
        compiler_params=pltpu.CompilerParams(dimension_semantics=('arbitrary', 'arbitrary')),
    )(c_all, w_ada, b_ada.reshape(DEPTH, 1, 6 * D_MODEL))


_REF_SPLITS = (('dn_qkv', DN_CONV_CH), ('dn_b', N_HEADS), ('dn_a', N_HEADS), ('dn_g', GROUP_W),
               ('hg_q', GROUP_W), ('hg_f', GROUP_W), ('hg_i', GROUP_W), ('hg_g', GROUP_W),
               ('gla_q', GLA_W), ('gla_k', GLA_W), ('gla_v', GROUP_W), ('gla_r', GLA_RANK), ('gla_g', GROUP_W),
               ('ml_q', GROUP_W), ('ml_k', GROUP_W), ('ml_v', GROUP_W), ('ml_i', N_HEADS), ('ml_f', N_HEADS),
               ('ml_o', GROUP_W))


def _permute_w_in(w):
    d = w.shape[0]
    cols, off = {}, 0
    for name, n in _REF_SPLITS:
        cols[name] = w[:, off:off + n]
        off += n
    assert off == w.shape[1]
    pieces, lane = [], 0
    for name, start in (('dn_b', S_DNB), ('gla_r', S_GLAR), ('dn_a', S_DNA), ('ml_i', S_MLI), ('ml_f', S_MLF)):
        pieces += [jnp.zeros((d, start - lane), w.dtype), cols[name]]
        lane = start + cols[name].shape[1]
    small = jnp.concatenate(pieces + [jnp.zeros((d, LANES - lane), w.dtype)], axis=1)
    cols['small'] = small
    order = ('dn_qkv', 'small', 'hg_f', 'hg_q', 'dn_g', 'hg_g', 'gla_g', 'ml_o',
             'hg_i', 'gla_q', 'gla_k', 'gla_v', 'ml_q', 'ml_k', 'ml_v')
    out = jnp.concatenate([cols[k] for k in order], axis=1)
    assert out.shape[1] == N_IN
    return out


def _lane_row(pairs, width=LANES):
    row = jnp.zeros((width,), F32)
    for off, val in pairs:
        row = row.at[off:off + val.shape[0]].set(val.astype(F32))
    return row


def _layer_weights(p, l):
    sp = jnp.zeros((SUB, LANES), F32)
    sp = sp.at[0].set(_lane_row([(S_DNA, p['dn_dt_bias'][l]), (S_MLI, p['ml_i_bias'][l]), (S_MLF, p['ml_f_bias'][l])]))
    sp = sp.at[1].set(_lane_row([(S_DNA, p['dn_a_log'][l])]))
    gn = jnp.stack([jnp.tile(p[k][l].astype(F32), N_HEADS) for k in ('dn_norm_g', 'hg_norm_g', 'gla_norm_g', 'ml_norm_g')])
    wup = jnp.zeros((LANES, GLA_W), F32).at[S_GLAR:S_GLAR + GLA_RANK].set(p['gla_w_up'][l]).astype(BF16)
    lb_sel = (jnp.arange(DEPTH) >= 1) & (jnp.arange(DEPTH) <= l)
    return dict(
        g_mix=p['g_mix'][l].reshape(1, D_MODEL), g_ffn=p['g_ffn'][l].reshape(1, D_MODEL),
        w_in=_permute_w_in(p['w_in'][l]).astype(BF16), w_out=p['w_out'][l].astype(BF16),
        conv_w=p['dn_conv_w'][l], sp=sp, gn=gn, lb_logits=p['hg_lb_logits'].astype(F32),
        lb_sel=lb_sel.astype(F32).reshape(DEPTH, 1), wup=wup, bup=p['gla_b_up'][l].reshape(1, GLA_W).astype(F32),
        w_up=p['w_up'][l].astype(BF16), w_down=p['w_down'][l].astype(BF16))


def _trunk(x, mods, states, lws, g_final, nb, tb):
    new_states = []
    for l in range(DEPTH):
        outs = _mixer_call(x, mods[l], lws[l], states[l], nb, tb)
        x = outs[0]
        new_states.append(outs[1:])
        x = _ffn_call(x, mods[l], lws[l], g_final, *_tiling(x.shape[0], x.shape[1], FFN_ROWS, FFN_ROWS), l == DEPTH - 1)
    return x, new_states


def _pack_states(conv, s_dn, s_hg, s_gla, c_ml, n_ml, m_ml, l):
    b = conv.shape[1]
    m_row = jnp.zeros((b, 1, LANES), F32).at[:, 0, S_MLF:S_MLF + N_HEADS].set(m_ml[l].astype(F32))
    return (conv[l].astype(F32), s_dn[l].astype(F32), s_hg[l].astype(F32), s_gla[l].astype(F32),
            c_ml[l].astype(F32), n_ml[l].astype(F32).reshape(b, 1, N_HEADS * ML_DK), m_row)


def _unpack_states(sts):
    conv, s_dn, s_hg, s_gla, c_ml = (jnp.stack([s[i] for s in sts]) for i in range(5))
    n_ml = jnp.stack([s[5].reshape(s[5].shape[0], N_HEADS, ML_DK) for s in sts])
    m_ml = jnp.stack([s[6][:, 0, S_MLF:S_MLF + N_HEADS] for s in sts])
    return conv, s_dn, s_hg, s_gla, c_ml, n_ml, m_ml


def _zero_states(b):
    z = lambda *s: jnp.zeros(s, F32)
    return (z(b, DN_CONV - 1, DN_CONV_CH), z(b, N_HEADS, DN_DK, HEAD_V), z(b, N_HEADS, HG_DK, HEAD_V),
            z(b, N_HEADS, GLA_DK, HEAD_V), z(b, N_HEADS, ML_DK, HEAD_V), z(b, 1, N_HEADS * ML_DK), z(b, 1, LANES))


def _tiling(bsz, t, rows, tokens):
    tb = min(t, tokens)
    nb = max(1, min(bsz, rows // tb))
    while bsz % nb:
        nb -= 1
    return nb, tb


def kernel(x_prompt, x_sample, c_prompt, c_sample, cache_dn_conv, state_dn, state_hgrn, state_gla, state_mlstm_c, state_mlstm_n, state_mlstm_m, w_ada, b_ada, g_mix, g_ffn, w_in, dn_conv_w, dn_a_log, dn_dt_bias, dn_norm_g, hg_lb_logits, hg_norm_g, gla_w_up, gla_b_up, gla_norm_g, ml_i_bias, ml_f_bias, ml_norm_g, w_out, w_up, w_down, g_final):
    p = dict(g_mix=g_mix, g_ffn=g_ffn, w_in=w_in, dn_conv_w=dn_conv_w, dn_a_log=dn_a_log, dn_dt_bias=dn_dt_bias,
             dn_norm_g=dn_norm_g, hg_lb_logits=hg_lb_logits, hg_norm_g=hg_norm_g, gla_w_up=gla_w_up,
             gla_b_up=gla_b_up, gla_norm_g=gla_norm_g, ml_i_bias=ml_i_bias, ml_f_bias=ml_f_bias,
             ml_norm_g=ml_norm_g, w_out=w_out, w_up=w_up, w_down=w_down)
    lws = [_layer_weights(p, l) for l in range(DEPTH)]
    gfin = g_final.reshape(1, D_MODEL).astype(F32)
    bp, bs = x_prompt.shape[0], x_sample.shape[0]
    mod = _ada_call(jnp.concatenate([c_prompt, c_sample], axis=0).astype(F32), w_ada.astype(BF16),
                    b_ada.astype(F32)).reshape(DEPTH, bp + bs, 6, D_MODEL)

    outs = []
    raw = (cache_dn_conv, state_dn, state_hgrn, state_gla, state_mlstm_c, state_mlstm_n, state_mlstm_m)
    for x, lo, hi, states in ((x_prompt, 0, bp, [_zero_states(bp)] * DEPTH),
                              (x_sample, bp, bp + bs, [_pack_states(*raw, l) for l in range(DEPTH)])):
        nb, tb = _tiling(x.shape[0], x.shape[1], MIXER_ROWS, MIXER_TOKENS)
        y, new = _trunk(x.astype(F32), [mod[l, lo:hi] for l in range(DEPTH)], states, lws, gfin, nb, tb)
        outs.append((y, _unpack_states(new)))
    (y_p, st_p), (y_s, st_s) = outs
    return (y_p, y_s) + tuple(st_p) + tuple(st_s)
```

```python
import functools

import numpy as np
import jax
import jax.numpy as jnp
from jax import lax
from jax.experimental import pallas as pl
from jax.experimental.pallas import tpu as pltpu

F32 = jnp.float32
BF16 = jnp.bfloat16

D_MODEL = 1024
DEPTH = 2
CHUNK = 64
N_HEADS = 4
HEAD_V = 64
GROUP_W = N_HEADS * HEAD_V
MIX_W = 4 * GROUP_W
DN_DK = 64
DN_CONV = 4
DN_CONV_CH = 3 * GROUP_W
HG_DK = 64
GLA_DK = 32
GLA_W = N_HEADS * GLA_DK
GLA_RANK = 16
GLA_TAU = 16.0
ML_DK = 64
D_FF = 2816
EPS = 1e-6

Z_DNQKV = 0
Z_SMALL = 768
Z_HGF = 896
Z_HGQ = 1152
Z_DNG = 1408
Z_HGG = 1664
Z_GLAG = 1920
Z_MLO = 2176
N_EARLY = 2432
Z_HGI = 2432
Z_GLAQ = 2688
Z_GLAK = 2816
Z_GLAV = 2944
Z_MLQ = 3200
Z_MLK = 3456
Z_MLV = 3712
N_IN = 3968
S_DNB = 0
S_GLAR = 16
S_DNA = 32
S_MLI = 64
S_MLF = 96

SUB = 8
LANES = 128
TRI_BLOCK = 16
VMEM_LIMIT = 56 * 1024 * 1024

M_INCL, M_STRICT, M_DIAG, M_SAMEBLK = 0, 1, 2, 3
M_LEVEL0 = 4
LOG2E = 1.4426950408889634
MIXER_ROWS = 256
MIXER_TOKENS = 256
FFN_ROWS = 1024
STAGGER = 4
MAX_INTERLEAVE = 4


def _sigmoid(x):
    return 1.0 / (1.0 + jnp.exp(-x))


def _silu(x):
    return x * _sigmoid(x)


def _log_sigmoid(x):
    return jnp.minimum(x, 0.0) - jnp.log1p(jnp.exp(-jnp.abs(x)))


def _softplus(x):
    return jnp.maximum(x, 0.0) + jnp.log1p(jnp.exp(-jnp.abs(x)))


def _dot(a, b):
    return jnp.dot(a.astype(BF16), b.astype(BF16), preferred_element_type=F32)


def _dot_nt(a, b):
    return lax.dot_general(a.astype(BF16), b.astype(BF16), (((1,), (1,)), ((), ())),
                           preferred_element_type=F32)


def _split3(x):
    x1 = x.astype(BF16)
    r = x - x1.astype(F32)
    x2 = r.astype(BF16)
    r = r - x2.astype(F32)
    return x1, x2, r.astype(BF16)


def _move_rows(xs, sel):
    parts = [_split3(x) for x in xs]
    y = jnp.dot(jnp.concatenate([p[i] for i in range(3) for p in parts], axis=0), sel, preferred_element_type=F32)
    n = sum(x.shape[0] for x in xs)
    outs = []
    off = 0
    for x in xs:
        r = x.shape[0]
        outs.append(y[off:off + r] + (y[n + off:n + off + r] + y[2 * n + off:2 * n + off + r]))
        off += r
    return outs


def _move_r(x, sel):
    return _move_rows([x], sel)[0]


def _move_l(sel, x):
    x1, x2, x3 = _split3(x)
    d = lambda a: jnp.dot(sel, a, preferred_element_type=F32)
    return d(x1) + (d(x2) + d(x3))


def _bd(x, mask):
    xb = x.astype(BF16)
    return jnp.concatenate([xb] * N_HEADS, axis=0) * mask


def _scan0(x, op, fill):
    n = x.shape[0]
    row = lax.broadcasted_iota(jnp.int32, x.shape, 0)
    sh = 1
    while sh < n:
        r = pltpu.roll(x, sh, axis=0)
        x = op(x, jnp.where(row >= sh, r, fill))
        sh *= 2
    return x


def _rowform(xe, diag):
    return jnp.sum(xe * diag, axis=0, keepdims=True)


def _rmsnorm_rows(x, g):
    return x * lax.rsqrt(jnp.mean(x * x, axis=-1, keepdims=True) + EPS) * g


def _run_interleaved(gens):
    live = list(gens)
    rnd = 0
    while live:
        alive = []
        for start, g in live:
            if rnd >= start:
                try:
                    next(g)
                except StopIteration:
                    continue
            alive.append((start, g))
        live = alive
        rnd += 1


def _await(boxes, key):
    while key not in boxes:
        yield
    return boxes[key]


def _tri_solve(mm, rhs, masks, bdp, bdr, c):
    assert c // TRI_BLOCK <= 4
    eye = masks[M_DIAG]
    mul = lambda a, b: _dot(a, _bd(b, bdp))
    app = lambda a, r: _dot(a, _bd(r, bdr))
    md = mm * masks[M_SAMEBLK]
    mo = mm - md
    p2 = mul(md, md)
    yield
    d = eye - md
    d = d + mul(d, p2)
    p4 = mul(p2, p2)
    yield
    d = d + mul(d, p4)
    p8 = mul(p4, p4)
    yield
    d = d + mul(d, p8)
    yield
    n = mul(d, mo)
    ys = [app(d, r) for r in rhs]
    yield
    zs = [y - app(n, y) for y in ys]
    if c // TRI_BLOCK <= 2:
        yield
        return zs
    n2 = mul(n, n)
    yield
    ws = [z + app(n2, z) for z in zs]
    yield
    return ws


def _deltanet_chunk(q, k, v, beta_s, gam_s, get_state, kc, c):
    masks = kc['masks']
    gam_e = _move_r(gam_s, kc['e_c'][1])
    beta_e = _move_r(beta_s, kc['e_c'][0])
    if c == HEAD_V:
        gam_d, beta_d = gam_e, beta_e
    else:
        gam_d = _move_r(gam_s, kc['e_d'][1])
        beta_d = _move_r(beta_s, kc['e_d'][0])
    kq = _dot_nt(jnp.concatenate([k, q], axis=0), _bd(k, kc['bd256'][...]))
    kk, qk = kq[0:c], kq[c:2 * c]
    yield
    gam_r = _rowform(gam_e, masks[M_DIAG])
    dec = jnp.exp(jnp.minimum(gam_e - gam_r, 0.0))
    mm = beta_e * kk * dec * masks[M_STRICT]
    eg = jnp.exp(gam_d)
    w, u0 = yield from _tri_solve(mm, [beta_d * eg * k, beta_d * v], masks, kc['bdp'][...], kc['bd256'][...], c)
    gl = gam_d[c - 1:c, :]
    kdec_t = (k * jnp.exp(gl - gam_d)).T
    st = yield from get_state()
    wq = _dot(jnp.concatenate([w, q * eg], axis=0), st)
    u = u0 - wq[0:c]
    qs = wq[c:2 * c]
    yield
    st_new = jnp.exp(gl) * st + kc['st256'][...] * _dot(kdec_t, u)
    o = qs + _dot(qk * dec * masks[M_INCL], _bd(u, kc['bd256'][...]))
    return o, st_new


def _block_ref(b, sz):
    c, w = b.shape
    g3 = b.reshape(c // (2 * sz), 2 * sz, w)
    return jnp.broadcast_to(g3[:, sz - 1:sz, :], g3.shape).reshape(c, w)


def _gla_chunk(q, k, v, g2, get_state, kc, c, wide):
    masks = kc['masks']
    bdk = kc['bd256'][...] if wide else kc['bd128'][...]
    ie = kc['ie256'][...] if wide else kc['ie128'][...]
    stm = kc['st256'][...] if wide else kc['st128'][...]
    v_t = v.T
    bd2 = _move_l(kc['lmat'][...], g2)
    b = bd2[0:c, :]
    yield
    row = lax.broadcasted_iota(jnp.int32, b.shape, 0)
    x1 = jnp.where(jnp.bitwise_and(row, 1) == 1, q * pltpu.roll(k, 1, axis=0) * jnp.exp2(g2), 0.0)
    dd = _dot(jnp.concatenate([q * k, x1], axis=0), ie)
    nlvl = c.bit_length() - 1
    attn = dd[0:c] * masks[M_DIAG] + dd[c:2 * c] * masks[M_LEVEL0 + nlvl - 1]
    lvl = 0
    sz = c // 2
    while sz >= 2:
        d = b - _block_ref(b, sz) if sz >= 4 else bd2[c:2 * c, :]
        e = jnp.exp2(jnp.minimum(d, -d))
        attn = attn + _dot_nt(q * e, _bd(k * e, bdk)) * masks[M_LEVEL0 + lvl]
        yield
        sz //= 2
        lvl += 1
    last = b[c - 1:c, :]
    upd = stm * _dot(v_t, k * jnp.exp2(last - b))
    o = _dot(attn, _bd(v, kc['bd256'][...]))
    yield
    st = yield from get_state()
    st_new = jnp.exp2(last) * st + upd
    o = o + _dot_nt(q * jnp.exp2(b), st)
    return o, st_new


def _mlstm_chunk(q, k, v, ig_s, lf_s, get_m, put_m, get_cn, kc, c):
    masks = kc['masks']
    fcum = _move_l(kc['lmat'][0:c, :], lf_s)
    a = ig_s - fcum
    imax = fcum + _scan0(a, jnp.maximum, -jnp.inf)
    fl = fcum[c - 1:c, :]
    lw = fl - fcum + ig_s
    lw_max = jnp.max(lw, axis=0, keepdims=True)
    kb = _bd(k, kc['bd256'][...])
    qk = _dot_nt(q, kb)
    a_e = _move_r(a, kc['e_c'][2])
    a_r = _rowform(a_e, masks[M_DIAG])
    yield
    m_row = yield from get_m()
    m_new = jnp.maximum(fl + m_row, lw_max)
    put_m(m_new)
    mt = jnp.maximum(fcum + m_row, imax)
    rows = lambda r: jnp.broadcast_to(r, (2 * SUB, LANES))
    consts = [rows(fl + m_row - m_new), rows(m_row), rows(fl - m_new)]
    x1 = fcum - mt
    if c == HEAD_V:
        x1_e, lfl, lwc, m_d, sh_d = _move_rows([x1, -mt] + consts, kc['e_d'][2])
        x1_d, a_d = x1_e, a_e
    else:
        x1_e = _move_r(x1, kc['e_c'][2])
        x1_d, lfl, a_d, lwc, m_d, sh_d = _move_rows([x1, -mt, a] + consts, kc['e_d'][2])
    w_inter = jnp.exp(x1_d + m_d[0:1, :])
    floor = jnp.exp(lfl)
    ws = jnp.exp(a_d + sh_d[0:1, :])
    wc = jnp.exp(lwc)[0:1, :]
    wsv_t = (ws * v).T
    yield
    w_intra = jnp.exp(jnp.minimum(x1_e + a_r, 0.0)) * masks[M_INCL] * qk
    num = _dot(w_intra, _bd(v, kc['bd256'][...]))
    den = _dot(w_intra, kc['iep'][...])
    upd = kc['st256'][...] * _dot(wsv_t, k)
    n_upd = jnp.sum(ws * k, axis=0, keepdims=True)
    yield
    ct, n_row = yield from get_cn()
    ct_new = wc * ct + upd
    n_new = wc * n_row + n_upd
    num = num + w_inter * _dot_nt(q, ct)
    den = den + w_inter * _dot(q * n_row, kc['ones256'][...])
    hh = num / jnp.maximum(jnp.abs(den), floor)
    return hh, ct_new, n_new


def _mixer_kernel(x_ref, mod_ref, gmix_ref, win_ref, wout_ref, convw_ref, sp_ref, gn_ref, lbl_ref, lbs_ref,
                  wup_ref, bup_ref,
                  masks_ref, ec_ref, ed_ref, bd256_ref, bd128_ref, bdp_ref, st256_ref, st128_ref,
                  ie256_ref, ie128_ref, iep_ref, ones256_ref, mean256_ref, lmat_ref,
                  conv0_ref, sdn0_ref, shg0_ref, sgla0_ref, c0_ref, n0_ref, m0_ref,
                  y_ref, convo_ref, sdno_ref, shgo_ref, sglao_ref, cmlo_ref, nml_ref, mml_ref,
                  z_ref, xp_ref, qkv_ref, mix_ref, hgk_ref, sdn_ref, shg_ref, sgla_ref, cml_ref, *, nb, tb, c, ilv):
    ti = pl.program_id(1)
    rows = nb * tb
    nchunk = tb // c
    mats = ((sdn0_ref, sdno_ref, sdn_ref), (shg0_ref, shgo_ref, shg_ref),
            (sgla0_ref, sglao_ref, sgla_ref), (c0_ref, cmlo_ref, cml_ref))

    @pl.when(ti == 0)
    def _():
        convo_ref[...] = conv0_ref[...]
        nml_ref[...] = n0_ref[...]
        mml_ref[...] = m0_ref[...]
        for raw_ref, _, st_ref in mats:
            dk = raw_ref.shape[2]
            stm = st256_ref[...] if dk == HEAD_V else st128_ref[...]
            for b in range(nb):
                raw2d = raw_ref[b].reshape(N_HEADS * dk, HEAD_V)
                if st_ref is sdn_ref:
                    st_ref[b] = jnp.concatenate([raw2d] * N_HEADS, axis=1) * stm
                else:
                    st_ref[b] = jnp.concatenate([raw2d.T] * N_HEADS, axis=0) * stm

    x = x_ref[...]
    mod = mod_ref[...]
    h = _rmsnorm_rows(x, gmix_ref[...]) * (1.0 + mod[:, 1:2, :]) + mod[:, 0:1, :]
    hb = h.reshape(rows, D_MODEL).astype(BF16)
    z_ref[:, 0:N_EARLY] = jnp.dot(hb, win_ref[:, 0:N_EARLY], preferred_element_type=F32)
    z_ref[:, N_EARLY:N_IN] = jnp.dot(hb, win_ref[:, N_EARLY:N_IN], preferred_element_type=F32)

    convw = convw_ref[...]
    for b in range(nb):
        xp_ref[b, SUB - (DN_CONV - 1):SUB, :] = convo_ref[b]
        xp_ref[b, SUB:SUB + tb, :] = z_ref[b * tb:(b + 1) * tb, Z_DNQKV:Z_DNQKV + DN_CONV_CH]
        acc = xp_ref[b, SUB - 3:SUB - 3 + tb, :] * convw[0:1, :]
        for j in range(1, DN_CONV):
            acc = acc + xp_ref[b, SUB - 3 + j:SUB - 3 + j + tb, :] * convw[j:j + 1, :]
        qkv_ref[b * tb:(b + 1) * tb, :] = _silu(acc)
        convo_ref[b] = xp_ref[b, SUB + tb - (DN_CONV - 1):SUB + tb, :]

    sp = sp_ref[...]
    gn = gn_ref[...]
    lbl = lbl_ref[...]
    lbs = lbs_ref[...]
    lbe = jnp.exp(lbl - jnp.max(lbl, axis=0, keepdims=True))
    lb = jnp.sum(lbs * (lbe / jnp.sum(lbe, axis=0, keepdims=True)), axis=0, keepdims=True)
    log_lb = jnp.log(lb)
    log_1mlb = jnp.log1p(-lb)
    neg_a = -jnp.exp(sp[1:2, :])

    zf = z_ref[:, Z_HGF:Z_HGF + GROUP_W]
    t2 = log_1mlb + _log_sigmoid(zf)
    mx = jnp.maximum(log_lb, t2)
    z_ref[:, Z_HGF:Z_HGF + GROUP_W] = (mx + jnp.log(jnp.exp(log_lb - mx) + jnp.exp(t2 - mx))) * LOG2E
    hgk_ref[...] = (1.0 - lb) * _sigmoid(-zf)
    z_ref[:, Z_HGQ:Z_HGQ + GROUP_W] = _silu(z_ref[:, Z_HGQ:Z_HGQ + GROUP_W])
    for m, col in enumerate((Z_DNG, Z_HGG, Z_GLAG)):
        z_ref[:, col:col + GROUP_W] = _silu(z_ref[:, col:col + GROUP_W]) * gn[m:m + 1, :]
    z_ref[:, Z_MLO:Z_MLO + GROUP_W] = _sigmoid(z_ref[:, Z_MLO:Z_MLO + GROUP_W]) * gn[3:4, :]

    kc = dict(masks=masks_ref, e_c=ec_ref, e_d=ed_ref, bd256=bd256_ref, bd128=bd128_ref, bdp=bdp_ref,
              st256=st256_ref, st128=st128_ref, ie256=ie256_ref, ie128=ie128_ref, iep=iep_ref,
              ones256=ones256_ref, mean256=mean256_ref, lmat=lmat_ref)
    run = min(ilv, nchunk)
    assert ilv % run == 0 and nchunk % run == 0

    def chunk_gens(it, u, boxes):
        i = it * ilv + u
        pos = u % run
        if nb == 1:
            b = 0
        elif ilv % nchunk == 0:
            b = it * (ilv // nchunk) + u // nchunk
        else:
            b = i // nchunk
        rs = pl.ds(pl.multiple_of(i * c, c), c)
        zc = lambda off, w: z_ref[rs, off:off + w]

        def getter(key, read):
            def get():
                if pos > 0:
                    return (yield from _await(boxes, (key, u - 1)))
                return read()
                yield
            return get

        def dn():
            small = zc(Z_SMALL, LANES)
            sb = small + sp[0:1, :]
            qkv = qkv_ref[rs, :]
            cq, ck, cv = qkv[:, 0:GROUP_W], qkv[:, GROUP_W:2 * GROUP_W], qkv[:, 2 * GROUP_W:3 * GROUP_W]
            ss = _dot(jnp.concatenate([cq * cq, ck * ck], axis=0), kc['ones256'][...])
            ssq, ssk = ss[0:c], ss[c:2 * c]
            beta_s = _sigmoid(small)
            gam_s = _move_l(kc['lmat'][0:c, :], neg_a * _softplus(sb))
            yield
            dq = cq * lax.rsqrt(ssq + EPS) * (DN_DK ** -0.5)
            dk = ck * lax.rsqrt(ssk + EPS)
            o, st_new = yield from _deltanet_chunk(dq, dk, cv, beta_s, gam_s,
                                                   getter('dn', lambda: sdn_ref[b]), kc, c)
            boxes[('dn', u)] = st_new
            sdn_ref[b] = st_new
            boxes[('out', u, 0)] = o

        def hg():
            o, st_new = yield from _gla_chunk(zc(Z_HGQ, GROUP_W), hgk_ref[rs, :], zc(Z_HGI, GROUP_W),
                                              zc(Z_HGF, GROUP_W), getter('hg', lambda: shg_ref[b]), kc, c, True)
            boxes[('hg', u)] = st_new
            shg_ref[b] = st_new
            boxes[('out', u, 1)] = o

        def gla():
            small = zc(Z_SMALL, LANES)
            g_gla = _log_sigmoid(_dot(small, wup_ref[...]) + bup_ref[...]) * (LOG2E / GLA_TAU)
            yield
            o, st_new = yield from _gla_chunk(zc(Z_GLAQ, GLA_W) * (GLA_DK ** -0.5), zc(Z_GLAK, GLA_W),
                                              zc(Z_GLAV, GROUP_W), g_gla,
                                              getter('gla', lambda: sgla_ref[b]), kc, c, False)
            boxes[('gla', u)] = st_new
            sgla_ref[b] = st_new
            boxes[('out', u, 2)] = o

        def ml():
            sb = zc(Z_SMALL, LANES) + sp[0:1, :]
            ig_s = pltpu.roll(sb, S_MLF - S_MLI, axis=1)
            lf_s = _log_sigmoid(sb)

            def put_m(m_new):
                boxes[('ml_m', u)] = m_new
                mml_ref[b] = m_new

            hh, c_new, n_new = yield from _mlstm_chunk(
                zc(Z_MLQ, GROUP_W) * (ML_DK ** -0.5), zc(Z_MLK, GROUP_W), zc(Z_MLV, GROUP_W), ig_s, lf_s,
                getter('ml_m', lambda: mml_ref[b]), put_m,
                getter('ml_cn', lambda: (cml_ref[b], nml_ref[b])), kc, c)
            boxes[('ml_cn', u)] = (c_new, n_new)
            cml_ref[b] = c_new
            nml_ref[b] = n_new
            boxes[('out', u, 3)] = hh

        late = STAGGER * pos
        return [(0, dn()), (late, hg()), (late, gla()), (late, ml())]

    def chunk_body(it, carry):
        boxes = {}
        gens = []
        for u in range(ilv):
            gens += chunk_gens(it, u, boxes)
        _run_interleaved(gens)
        outs = jnp.concatenate([boxes[('out', u, m)] for u in range(ilv) for m in range(4)], axis=0)
        ms = _dot(outs * outs, kc['mean256'][...])
        normed = outs * lax.rsqrt(ms + EPS)
        gate_cols = (Z_DNG, Z_HGG, Z_GLAG, Z_MLO)
        for u in range(ilv):
            rs = pl.ds(pl.multiple_of((it * ilv + u) * c, c), c)
            for m in range(4):
                gate = z_ref[rs, gate_cols[m]:gate_cols[m] + GROUP_W]
                r0 = (u * 4 + m) * c
                mix_ref[rs, m * GROUP_W:(m + 1) * GROUP_W] = (normed[r0:r0 + c] * gate).astype(BF16)
        return carry

    lax.fori_loop(0, nb * nchunk // ilv, chunk_body, 0)

    @pl.when(ti == pl.num_programs(1) - 1)
    def _():
        for _, out_ref, st_ref in mats:
            dk = out_ref.shape[2]
            for b in range(nb):
                st = st_ref[b]
                if st_ref is sdn_ref:
                    raw2d = sum(st[:, hd * HEAD_V:(hd + 1) * HEAD_V] for hd in range(N_HEADS))
                else:
                    raw2d = sum(st[hd * HEAD_V:(hd + 1) * HEAD_V, :] for hd in range(N_HEADS)).T
                out_ref[b] = raw2d.reshape(N_HEADS, dk, HEAD_V)

    out = jnp.dot(mix_ref[...], wout_ref[...], preferred_element_type=F32).reshape(nb, tb, D_MODEL)
    y_ref[...] = x_ref[...] + mod_ref[:, 2:3, :] * out


def _const_tables(c):
    pc = N_HEADS * c
    t = np.arange(c)[:, None]
    lane = np.arange(pc)[None, :]
    hs, s = lane // c, lane % c
    masks = [s <= t, s < t, s == t, (s // TRI_BLOCK) == (t // TRI_BLOCK)]
    sz = c // 2
    while sz >= 1:
        masks.append(((s // (2 * sz)) == (t // (2 * sz))) & ((t // sz) % 2 == 1) & ((s // sz) % 2 == 0))
        sz //= 2
    masks = np.stack([np.broadcast_to(m, (c, pc)) for m in masks]).astype(np.float32)
    r = np.arange(c)[None, :]
    tri = (r <= t).astype(np.float32)
    lmat = np.concatenate([tri, tri - tri[(np.arange(c) // 4) * 4 + 1]], axis=0)

    def expand(col0, w):
        j = np.arange(LANES)[:, None]
        l = np.arange(N_HEADS * w)[None, :]
        return (j == col0 + l // w).astype(np.float32)

    e_c = np.stack([expand(S_DNB, c), expand(S_DNA, c), expand(S_MLF, c)])
    e_d = np.stack([expand(S_DNB, HEAD_V), expand(S_DNA, HEAD_V), expand(S_MLF, HEAD_V)])

    def blk(nr, rg, nl, lg):
        return ((np.arange(nr)[:, None] // rg) == (np.arange(nl)[None, :] // lg)).astype(np.float32)

    tabs = dict(
        masks=jnp.asarray(masks), e_c=jnp.asarray(e_c, BF16), e_d=jnp.asarray(e_d, BF16),
        bd256=jnp.asarray(blk(pc, c, GROUP_W, HEAD_V), BF16), bd128=jnp.asarray(blk(pc, c, GLA_W, GLA_DK), BF16),
        bdp=jnp.asarray(blk(pc, c, pc, c), BF16),
        st256=jnp.asarray(blk(GROUP_W, HEAD_V, GROUP_W, HEAD_V)), st128=jnp.asarray(blk(GROUP_W, HEAD_V, GLA_W, GLA_DK)),
        ie256=jnp.asarray(blk(GROUP_W, HEAD_V, pc, c), BF16), ie128=jnp.asarray(blk(GLA_W, GLA_DK, pc, c), BF16),
        iep=jnp.asarray(blk(pc, c, GROUP_W, HEAD_V), BF16),
        ones256=jnp.asarray(blk(GROUP_W, HEAD_V, GROUP_W, HEAD_V), BF16),
        mean256=jnp.asarray(blk(GROUP_W, HEAD_V, GROUP_W, HEAD_V) / HEAD_V, BF16), lmat=jnp.asarray(lmat, BF16))
    order = ['masks', 'e_c', 'e_d', 'bd256', 'bd128', 'bdp', 'st256', 'st128', 'ie256', 'ie128', 'iep', 'ones256',
             'mean256', 'lmat']
    return [tabs[k] for k in order]


def _full_spec(a):
    nd = a.ndim
    return pl.BlockSpec(a.shape, lambda bi, ti, _n=nd: (0,) * _n, pipeline_mode=pl.Buffered(1))


def _mixer_call(x, mod, lw, states, nb, tb):
    bsz, t, _ = x.shape
    c = min(CHUNK, t)
    assert t % tb == 0 and tb % c == 0 and bsz % nb == 0 and c % TRI_BLOCK == 0
    rows = nb * tb
    consts = _const_tables(c)
    params = [lw['g_mix'], lw['w_in'], lw['w_out'], lw['conv_w'], lw['sp'], lw['gn'], lw['lb_logits'], lw['lb_sel'],
              lw['wup'], lw['bup']]
    xspec = pl.BlockSpec((nb, tb, D_MODEL), lambda bi, ti: (bi, ti, 0))

    def bspec(a):
        nd = a.ndim
        return pl.BlockSpec((nb,) + a.shape[1:], lambda bi, ti, _n=nd: (bi,) + (0,) * (_n - 1))

    in_specs = ([xspec, bspec(mod)] + [_full_spec(a) for a in params] + [_full_spec(a) for a in consts]
                + [bspec(s) for s in states])
    out_shape = [jax.ShapeDtypeStruct(x.shape, F32)] + [jax.ShapeDtypeStruct(s.shape, F32) for s in states]
    out_specs = [xspec] + [bspec(s) for s in states]
    scratch = [pltpu.VMEM((rows, N_IN), F32),
               pltpu.VMEM((nb, SUB + tb, DN_CONV_CH), F32),
               pltpu.VMEM((rows, DN_CONV_CH), F32),
               pltpu.VMEM((rows, MIX_W), BF16),
               pltpu.VMEM((rows, N_HEADS * HG_DK), F32),
               pltpu.VMEM((nb, GROUP_W, N_HEADS * DN_DK), F32),
               pltpu.VMEM((nb, GROUP_W, N_HEADS * HG_DK), F32),
               pltpu.VMEM((nb, GROUP_W, GLA_W), F32),
               pltpu.VMEM((nb, GROUP_W, N_HEADS * ML_DK), F32)]
    ilv = MAX_INTERLEAVE
    while (nb * (tb // c)) % ilv:
        ilv //= 2
    kern = functools.partial(_mixer_kernel, nb=nb, tb=tb, c=c, ilv=ilv)
    return pl.pallas_call(
        kern, grid=(bsz // nb, t // tb), in_specs=in_specs, out_specs=out_specs, out_shape=out_shape,
        scratch_shapes=scratch, name='mixer',
        compiler_params=pltpu.CompilerParams(dimension_semantics=('arbitrary', 'arbitrary'),
                                             vmem_limit_bytes=VMEM_LIMIT),
    )(x, mod, *params, *consts, *states)


FF_TILE = 256


def _ffn_kernel(x_ref, mod_ref, gffn_ref, wup_ref, wdown_ref, gfin_ref, y_ref, *, nb, tb, final):
    rows = nb * tb
    x = x_ref[...]
    mod = mod_ref[...]
    h = _rmsnorm_rows(x, gffn_ref[...]) * (1.0 + mod[:, 4:5, :]) + mod[:, 3:4, :]
    hb = h.reshape(rows, D_MODEL).astype(BF16)
    acc = jnp.zeros((rows, D_MODEL), F32)
    for j in range(D_FF // FF_TILE):
        gate = jnp.dot(hb, wup_ref[:, j * FF_TILE:(j + 1) * FF_TILE], preferred_element_type=F32)
        up = jnp.dot(hb, wup_ref[:, D_FF + j * FF_TILE:D_FF + (j + 1) * FF_TILE], preferred_element_type=F32)
        act = (_silu(gate) * up).astype(BF16)
        acc = acc + jnp.dot(act, wdown_ref[j * FF_TILE:(j + 1) * FF_TILE, :], preferred_element_type=F32)
    y = x_ref[...] + mod_ref[:, 5:6, :] * acc.reshape(nb, tb, D_MODEL)
    if final:
        y = _rmsnorm_rows(y, gfin_ref[...])
    y_ref[...] = y


def _ffn_call(x, mod, lw, g_final, nb, tb, final):
    bsz, t, _ = x.shape
    xspec = pl.BlockSpec((nb, tb, D_MODEL), lambda bi, ti: (bi, ti, 0))
    mspec = pl.BlockSpec((nb, 6, D_MODEL), lambda bi, ti: (bi, 0, 0))
    params = [lw['g_ffn'], lw['w_up'], lw['w_down'], g_final]
    kern = functools.partial(_ffn_kernel, nb=nb, tb=tb, final=final)
    return pl.pallas_call(
        kern, grid=(bsz // nb, t // tb), in_specs=[xspec, mspec] + [_full_spec(a) for a in params],
        out_specs=xspec, out_shape=jax.ShapeDtypeStruct(x.shape, F32), name='ffn',
        compiler_params=pltpu.CompilerParams(dimension_semantics=('arbitrary', 'arbitrary'),
                                             vmem_limit_bytes=VMEM_LIMIT),
    )(x, mod, *params)


ADA_TILE = 1536


def _ada_kernel(c_ref, w_ref, b_ref, o_ref):
    o_ref[0] = jnp.dot(_silu(c_ref[...]).astype(BF16), w_ref[0], preferred_element_type=F32) + b_ref[0]


def _ada_call(c_all, w_ada, b_ada):
    n = c_all.shape[0]
    nt = 6 * D_MODEL // ADA_TILE
    return pl.pallas_call(
        _ada_kernel, grid=(DEPTH, nt),
        in_specs=[pl.BlockSpec((n, D_MODEL), lambda l, j: (0, 0)),
                  pl.BlockSpec((1, D_MODEL, ADA_TILE), lambda l, j: (l, 0, j)),
                  pl.BlockSpec((1, 1, ADA_TILE), lambda l, j: (l, 0, j))],
        out_specs=pl.BlockSpec((1, n, ADA_TILE), lambda l, j: (l, 0, j)),
        out_shape=jax.ShapeDtypeStruct((DEPTH, n, 6 * D_MODEL), F32), name='ada',
        compiler_params=pltpu.CompilerParams(dimension_semantics=('arbitrary', 'arbitrary')),
    )(c_all, w_ada, b_ada.reshape(DEPTH, 1, 6 * D_MODEL))


_REF_SPLITS = (('dn_qkv', DN_CONV_CH), ('dn_b', N_HEADS), ('dn_a', N_HEADS), ('dn_g', GROUP_W),
               ('hg_q', GROUP_W), ('hg_f', GROUP_W), ('hg_i', GROUP_W), ('hg_g', GROUP_W),
               ('gla_q', GLA_W), ('gla_k', GLA_W), ('gla_v', GROUP_W), ('gla_r', GLA_RANK), ('gla_g', GROUP_W),
               ('ml_q', GROUP_W), ('ml_k', GROUP_W), ('ml_v', GROUP_W), ('ml_i', N_HEADS), ('ml_f', N_HEADS),
               ('ml_o', GROUP_W))


def _permute_w_in(w):
    d = w.shape[0]
    cols, off = {}, 0
    for name, n in _REF_SPLITS:
        cols[name] = w[:, off:off + n]
        off += n
    assert off == w.shape[1]
    pieces, lane = [], 0
    for name, start in (('dn_b', S_DNB), ('gla_r', S_GLAR), ('dn_a', S_DNA), ('ml_i', S_MLI), ('ml_f', S_MLF)):
        pieces += [jnp.zeros((d, start - lane), w.dtype), cols[name]]
        lane = start + cols[name].shape[1]
    small = jnp.concatenate(pieces + [jnp.zeros((d, LANES - lane), w.dtype)], axis=1)
    cols['small'] = small
    order = ('dn_qkv', 'small', 'hg_f', 'hg_q', 'dn_g', 'hg_g', 'gla_g', 'ml_o',
             'hg_i', 'gla_q', 'gla_k', 'gla_v', 'ml_q', 'ml_k', 'ml_v')
    out = jnp.concatenate([cols[k] for k in order], axis=1)
    assert out.shape[1] == N_IN
    return out


def _lane_row(pairs, width=LANES):
    row = jnp.zeros((width,), F32)
    for off, val in pairs:
        row = row.at[off:off + val.shape[0]].set(val.astype(F32))
    return row


def _layer_weights(p, l):
    sp = jnp.zeros((SUB, LANES), F32)
    sp = sp.at[0].set(_lane_row([(S_DNA, p['dn_dt_bias'][l]), (S_MLI, p['ml_i_bias'][l]), (S_MLF, p['ml_f_bias'][l])]))
    sp = sp.at[1].set(_lane_row([(S_DNA, p['dn_a_log'][l])]))
    gn = jnp.stack([jnp.tile(p[k][l].astype(F32), N_HEADS) for k in ('dn_norm_g', 'hg_norm_g', 'gla_norm_g', 'ml_norm_g')])
    wup = jnp.zeros((LANES, GLA_W), F32).at[S_GLAR:S_GLAR + GLA_RANK].set(p['gla_w_up'][l]).astype(BF16)
    lb_sel = (jnp.arange(DEPTH) >= 1) & (jnp.arange(DEPTH) <= l)
    return dict(
        g_mix=p['g_mix'][l].reshape(1, D_MODEL), g_ffn=p['g_ffn'][l].reshape(1, D_MODEL),
        w_in=_permute_w_in(p['w_in'][l]).astype(BF16), w_out=p['w_out'][l].astype(BF16),
        conv_w=p['dn_conv_w'][l], sp=sp, gn=gn, lb_logits=p['hg_lb_logits'].astype(F32),
        lb_sel=lb_sel.astype(F32).reshape(DEPTH, 1), wup=wup, bup=p['gla_b_up'][l].reshape(1, GLA_W).astype(F32),
        w_up=p['w_up'][l].astype(BF16), w_down=p['w_down'][l].astype(BF16))


def _trunk(x, mods, states, lws, g_final, nb, tb):
    new_states = []
    for l in range(DEPTH):
        outs = _mixer_call(x, mods[l], lws[l], states[l], nb, tb)
        x = outs[0]
        new_states.append(outs[1:])
        x = _ffn_call(x, mods[l], lws[l], g_final, *_tiling(x.shape[0], x.shape[1], FFN_ROWS, FFN_ROWS), l == DEPTH - 1)
    return x, new_states


def _pack_states(conv, s_dn, s_hg, s_gla, c_ml, n_ml, m_ml, l):
    b = conv.shape[1]
    m_row = jnp.zeros((b, 1, LANES), F32).at[:, 0, S_MLF:S_MLF + N_HEADS].set(m_ml[l].astype(F32))
    return (conv[l].astype(F32), s_dn[l].astype(F32), s_hg[l].astype(F32), s_gla[l].astype(F32),
            c_ml[l].astype(F32), n_ml[l].astype(F32).reshape(b, 1, N_HEADS * ML_DK), m_row)


def _unpack_states(sts):
    conv, s_dn, s_hg, s_gla, c_ml = (jnp.stack([s[i] for s in sts]) for i in range(5))
    n_ml = jnp.stack([s[5].reshape(s[5].shape[0], N_HEADS, ML_DK) for s in sts])
    m_ml = jnp.stack([s[6][:, 0, S_MLF:S_MLF + N_HEADS] for s in sts])
    return conv, s_dn, s_hg, s_gla, c_ml, n_ml, m_ml


def _zero_states(b):
    z = lambda *s: jnp.zeros(s, F32)
    return (z(b, DN_CONV - 1, DN_CONV_CH), z(b, N_HEADS, DN_DK, HEAD_V), z(b, N_HEADS, HG_DK, HEAD_V),
            z(b, N_HEADS, GLA_DK, HEAD_V), z(b, N_HEADS, ML_DK, HEAD_V), z(b, 1, N_HEADS * ML_DK), z(b, 1, LANES))


def _tiling(bsz, t, rows, tokens):
    tb = min(t, tokens)
    nb = max(1, min(bsz, rows // tb))
    while bsz % nb:
        nb -= 1
    return nb, tb


def kernel(x_prompt, x_sample, c_prompt, c_sample, cache_dn_conv, state_dn, state_hgrn, state_gla, state_mlstm_c, state_mlstm_n, state_mlstm_m, w_ada, b_ada, g_mix, g_ffn, w_in, dn_conv_w, dn_a_log, dn_dt_bias, dn_norm_g, hg_lb_logits, hg_norm_g, gla_w_up, gla_b_up, gla_norm_g, ml_i_bias, ml_f_bias, ml_norm_g, w_out, w_up, w_down, g_final):
    p = dict(g_mix=g_mix, g_ffn=g_ffn, w_in=w_in, dn_conv_w=dn_conv_w, dn_a_log=dn_a_log, dn_dt_bias=dn_dt_bias,
             dn_norm_g=dn_norm_g, hg_lb_logits=hg_lb_logits, hg_norm_g=hg_norm_g, gla_w_up=gla_w_up,
             gla_b_up=gla_b_up, gla_norm_g=gla_norm_g, ml_i_bias=ml_i_bias, ml_f_bias=ml_f_bias,
             ml_norm_g=ml_norm_g, w_out=w_out, w_up=w_up, w_down=w_down)
    lws = [_layer_weights(p, l) for l in range(DEPTH)]
    gfin = g_final.reshape(1, D_MODEL).astype(F32)
    bp, bs = x_prompt.shape[0], x_sample.shape[0]
    mod = _ada_call(jnp.concatenate([c_prompt, c_sample], axis=0).astype(F32), w_ada.astype(BF16),
                    b_ada.astype(F32)).reshape(DEPTH, bp + bs, 6, D_MODEL)

    outs = []
    raw = (cache_dn_conv, state_dn, state_hgrn, state_gla, state_mlstm_c, state_mlstm_n, state_mlstm_m)
    for x, lo, hi, states in ((x_prompt, 0, bp, [_zero_states(bp)] * DEPTH),
                              (x_sample, bp, bp + bs, [_pack_states(*raw, l) for l in range(DEPTH)])):
        nb, tb = _tiling(x.shape[0], x.shape[1], MIXER_ROWS, MIXER_TOKENS)
        y, new = _trunk(x.astype(F32), [mod[l, lo:hi] for l in range(DEPTH)], states, lws, gfin, nb, tb)
        outs.append((y, _unpack_states(new)))
    (y_p, st_p), (y_s, st_s) = outs
    return (y_p, y_s) + tuple(st_p) + tuple(st_s)
```

```python
import functools

import numpy as np
import jax
import jax.numpy as jnp
from jax import lax
from jax.experimental import pallas as pl
from jax.experimental.pallas import tpu as pltpu

F32 = jnp.float32
BF16 = jnp.bfloat16

D_MODEL = 1024
DEPTH = 2
CHUNK = 64
N_HEADS = 4
HEAD_V = 64
GROUP_W = N_HEADS * HEAD_V
MIX_W = 4 * GROUP_W
DN_DK = 64
DN_CONV = 4
DN_CONV_CH = 3 * GROUP_W
HG_DK = 64
GLA_DK = 32
GLA_W = N_HEADS * GLA_DK
GLA_RANK = 16
GLA_TAU = 16.0
ML_DK = 64
D_FF = 2816
EPS = 1e-6

Z_DNQKV = 0
Z_SMALL = 768
Z_HGF = 896
Z_HGQ = 1152
Z_DNG = 1408
Z_HGG = 1664
Z_GLAG = 1920
Z_MLO = 2176
N_EARLY = 2432
Z_HGI = 2432
Z_GLAQ = 2688
Z_GLAK = 2816
Z_GLAV = 2944
Z_MLQ = 3200
Z_MLK = 3456
Z_MLV = 3712
N_IN = 3968
S_DNB = 0
S_GLAR = 16
S_DNA = 32
S_MLI = 64
S_MLF = 96

SUB = 8
LANES = 128
TRI_BLOCK = 16
VMEM_LIMIT = 56 * 1024 * 1024

M_INCL, M_STRICT, M_DIAG, M_SAMEBLK = 0, 1, 2, 3
M_LEVEL0 = 4
LOG2E = 1.4426950408889634
MIXER_ROWS = 256
MIXER_TOKENS = 256
FFN_ROWS = 1024
STAGGER = 4
MAX_INTERLEAVE = 4


def _sigmoid(x):
    return 1.0 / (1.0 + jnp.exp(-x))


def _silu(x):
    return x * _sigmoid(x)


def _log_sigmoid(x):
    return jnp.minimum(x, 0.0) - jnp.log1p(jnp.exp(-jnp.abs(x)))


def _softplus(x):
    return jnp.maximum(x, 0.0) + jnp.log1p(jnp.exp(-jnp.abs(x)))


def _dot(a, b):
    return jnp.dot(a.astype(BF16), b.astype(BF16), preferred_element_type=F32)


def _dot_nt(a, b):
    return lax.dot_general(a.astype(BF16), b.astype(BF16), (((1,), (1,)), ((), ())),
                           preferred_element_type=F32)


def _split3(x):
    x1 = x.astype(BF16)
    r = x - x1.astype(F32)
    x2 = r.astype(BF16)
    r = r - x2.astype(F32)
    return x1, x2, r.astype(BF16)


def _move_rows(xs, sel):
    parts = [_split3(x) for x in xs]
    y = jnp.dot(jnp.concatenate([p[i] for i in range(3) for p in parts], axis=0), sel, preferred_element_type=F32)
    n = sum(x.shape[0] for x in xs)
    outs = []
    off = 0
    for x in xs:
        r = x.shape[0]
        outs.append(y[off:off + r] + (y[n + off:n + off + r] + y[2 * n + off:2 * n + off + r]))
        off += r
    return outs


def _move_r(x, sel):
    return _move_rows([x], sel)[0]


def _move_l(sel, x):
    x1, x2, x3 = _split3(x)
    d = lambda a: jnp.dot(sel, a, preferred_element_type=F32)
    return d(x1) + (d(x2) + d(x3))


def _bd(x, mask):
    xb = x.astype(BF16)
    return jnp.concatenate([xb] * N_HEADS, axis=0) * mask


def _scan0(x, op, fill):
    n = x.shape[0]
    row = lax.broadcasted_iota(jnp.int32, x.shape, 0)
    sh = 1
    while sh < n:
        r = pltpu.roll(x, sh, axis=0)
        x = op(x, jnp.where(row >= sh, r, fill))
        sh *= 2
    return x


def _rowform(xe, diag):
    return jnp.sum(xe * diag, axis=0, keepdims=True)


def _rmsnorm_rows(x, g):
    return x * lax.rsqrt(jnp.mean(x * x, axis=-1, keepdims=True) + EPS) * g


def _run_interleaved(gens):
    live = list(gens)
    rnd = 0
    while live:
        alive = []
        for start, g in live:
            if rnd >= start:
                try:
                    next(g)
                except StopIteration:
                    continue
            alive.append((start, g))
        live = alive
        rnd += 1


def _await(boxes, key):
    while key not in boxes:
        yield
    return boxes[key]


def _tri_solve(mm, rhs, masks, bdp, bdr, c):
    assert c // TRI_BLOCK <= 4
    eye = masks[M_DIAG]
    mul = lambda a, b: _dot(a, _bd(b, bdp))
    app = lambda a, r: _dot(a, _bd(r, bdr))
    md = mm * masks[M_SAMEBLK]
    mo = mm - md
    p2 = mul(md, md)
    yield
    d = eye - md
    d = d + mul(d, p2)
    p4 = mul(p2, p2)
    yield
    d = d + mul(d, p4)
    p8 = mul(p4, p4)
    yield
    d = d + mul(d, p8)
    yield
    n = mul(d, mo)
    ys = [app(d, r) for r in rhs]
    yield
    zs = [y - app(n, y) for y in ys]
    if c // TRI_BLOCK <= 2:
        yield
        return zs
    n2 = mul(n, n)
    yield
    ws = [z + app(n2, z) for z in zs]
    yield
    return ws


def _deltanet_chunk(q, k, v, beta_s, gam_s, get_state, kc, c):
    masks = kc['masks']
    gam_e = _move_r(gam_s, kc['e_c'][1])
    beta_e = _move_r(beta_s, kc['e_c'][0])
    if c == HEAD_V:
        gam_d, beta_d = gam_e, beta_e
    else:
        gam_d = _move_r(gam_s, kc['e_d'][1])
        beta_d = _move_r(beta_s, kc['e_d'][0])
    kq = _dot_nt(jnp.concatenate([k, q], axis=0), _bd(k, kc['bd256'][...]))
    kk, qk = kq[0:c], kq[c:2 * c]
    yield
    gam_r = _rowform(gam_e, masks[M_DIAG])
    dec = jnp.exp(jnp.minimum(gam_e - gam_r, 0.0))
    mm = beta_e * kk * dec * masks[M_STRICT]
    eg = jnp.exp(gam_d)
    w, u0 = yield from _tri_solve(mm, [beta_d * eg * k, beta_d * v], masks, kc['bdp'][...], kc['bd256'][...], c)
    gl = gam_d[c - 1:c, :]
    kdec_t = (k * jnp.exp(gl - gam_d)).T
    st = yield from get_state()
    wq = _dot(jnp.concatenate([w, q * eg], axis=0), st)
    u = u0 - wq[0:c]
    qs = wq[c:2 * c]
    yield
    st_new = jnp.exp(gl) * st + kc['st256'][...] * _dot(kdec_t, u)
    o = qs + _dot(qk * dec * masks[M_INCL], _bd(u, kc['bd256'][...]))
    return o, st_new


def _block_ref(b, sz):
    c, w = b.shape
    g3 = b.reshape(c // (2 * sz), 2 * sz, w)
    return jnp.broadcast_to(g3[:, sz - 1:sz, :], g3.shape).reshape(c, w)


def _gla_chunk(q, k, v, g2, get_state, kc, c, wide):
    masks = kc['masks']
    bdk = kc['bd256'][...] if wide else kc['bd128'][...]
    ie = kc['ie256'][...] if wide else kc['ie128'][...]
    stm = kc['st256'][...] if wide else kc['st128'][...]
    v_t = v.T
    bd2 = _move_l(kc['lmat'][...], g2)
    b = bd2[0:c, :]
    yield
    row = lax.broadcasted_iota(jnp.int32, b.shape, 0)
    x1 = jnp.where(jnp.bitwise_and(row, 1) == 1, q * pltpu.roll(k, 1, axis=0) * jnp.exp2(g2), 0.0)
    dd = _dot(jnp.concatenate([q * k, x1], axis=0), ie)
    nlvl = c.bit_length() - 1
    attn = dd[0:c] * masks[M_DIAG] + dd[c:2 * c] * masks[M_LEVEL0 + nlvl - 1]
    lvl = 0
    sz = c // 2
    while sz >= 2:
        d = b - _block_ref(b, sz) if sz >= 4 else bd2[c:2 * c, :]
        e = jnp.exp2(jnp.minimum(d, -d))
        attn = attn + _dot_nt(q * e, _bd(k * e, bdk)) * masks[M_LEVEL0 + lvl]
        yield
        sz //= 2
        lvl += 1
    last = b[c - 1:c, :]
    upd = stm * _dot(v_t, k * jnp.exp2(last - b))
    o = _dot(attn, _bd(v, kc['bd256'][...]))
    yield
    st = yield from get_state()
    st_new = jnp.exp2(last) * st + upd
    o = o + _dot_nt(q * jnp.exp2(b), st)
    return o, st_new


def _mlstm_chunk(q, k, v, ig_s, lf_s, get_m, put_m, get_cn, kc, c):
    masks = kc['masks']
    fcum = _move_l(kc['lmat'][0:c, :], lf_s)
    a = ig_s - fcum
    imax = fcum + _scan0(a, jnp.maximum, -jnp.inf)
    fl = fcum[c - 1:c, :]
    lw = fl - fcum + ig_s
    lw_max = jnp.max(lw, axis=0, keepdims=True)
    kb = _bd(k, kc['bd256'][...])
    qk = _dot_nt(q, kb)
    a_e = _move_r(a, kc['e_c'][2])
    a_r = _rowform(a_e, masks[M_DIAG])
    yield
    m_row = yield from get_m()
    m_new = jnp.maximum(fl + m_row, lw_max)
    put_m(m_new)
    mt = jnp.maximum(fcum + m_row, imax)
    rows = lambda r: jnp.broadcast_to(r, (2 * SUB, LANES))
    consts = [rows(fl + m_row - m_new), rows(m_row), rows(fl - m_new)]
    x1 = fcum - mt
    if c == HEAD_V:
        x1_e, lfl, lwc, m_d, sh_d = _move_rows([x1, -mt] + consts, kc['e_d'][2])
        x1_d, a_d = x1_e, a_e
    else:
        x1_e = _move_r(x1, kc['e_c'][2])
        x1_d, lfl, a_d, lwc, m_d, sh_d = _move_rows([x1, -mt, a] + consts, kc['e_d'][2])
    w_inter = jnp.exp(x1_d + m_d[0:1, :])
    floor = jnp.exp(lfl)
    ws = jnp.exp(a_d + sh_d[0:1, :])
    wc = jnp.exp(lwc)[0:1, :]
    wsv_t = (ws * v).T
    yield
    w_intra = jnp.exp(jnp.minimum(x1_e + a_r, 0.0)) * masks[M_INCL] * qk
    num = _dot(w_intra, _bd(v, kc['bd256'][...]))
    den = _dot(w_intra, kc['iep'][...])
    upd = kc['st256'][...] * _dot(wsv_t, k)
    n_upd = jnp.sum(ws * k, axis=0, keepdims=True)
    yield
    ct, n_row = yield from get_cn()
    ct_new = wc * ct + upd
    n_new = wc * n_row + n_upd
    num = num + w_inter * _dot_nt(q, ct)
    den = den + w_inter * _dot(q * n_row, kc['ones256'][...])
    hh = num / jnp.maximum(jnp.abs(den), floor)
    return hh, ct_new, n_new


def _mixer_kernel(x_ref, mod_ref, gmix_ref, win_ref, wout_ref, convw_ref, sp_ref, gn_ref, lbl_ref, lbs_ref,
                  wup_ref, bup_ref,
                  masks_ref, ec_ref, ed_ref, bd256_ref, bd128_ref, bdp_ref, st256_ref, st128_ref,
                  ie256_ref, ie128_ref, iep_ref, ones256_ref, mean256_ref, lmat_ref,
                  conv0_ref, sdn0_ref, shg0_ref, sgla0_ref, c0_ref, n0_ref, m0_ref,
                  y_ref, convo_ref, sdno_ref, shgo_ref, sglao_ref, cmlo_ref, nml_ref, mml_ref,
                  z_ref, xp_ref, qkv_ref, mix_ref, hgk_ref, sdn_ref, shg_ref, sgla_ref, cml_ref, *, nb, tb, c, ilv):
    ti = pl.program_id(1)
    rows = nb * tb
    nchunk = tb // c
    mats = ((sdn0_ref, sdno_ref, sdn_ref), (shg0_ref, shgo_ref, shg_ref),
            (sgla0_ref, sglao_ref, sgla_ref), (c0_ref, cmlo_ref, cml_ref))

    @pl.when(ti == 0)
    def _():
        convo_ref[...] = conv0_ref[...]
        nml_ref[...] = n0_ref[...]
        mml_ref[...] = m0_ref[...]
        for raw_ref, _, st_ref in mats:
            dk = raw_ref.shape[2]
            stm = st256_ref[...] if dk == HEAD_V else st128_ref[...]
            for b in range(nb):
                raw2d = raw_ref[b].reshape(N_HEADS * dk, HEAD_V)
                if st_ref is sdn_ref:
                    st_ref[b] = jnp.concatenate([raw2d] * N_HEADS, axis=1) * stm
                else:
                    st_ref[b] = jnp.concatenate([raw2d.T] * N_HEADS, axis=0) * stm

    x = x_ref[...]
    mod = mod_ref[...]
    h = _rmsnorm_rows(x, gmix_ref[...]) * (1.0 + mod[:, 1:2, :]) + mod[:, 0:1, :]
    hb = h.reshape(rows, D_MODEL).astype(BF16)
    z_ref[:, 0:N_EARLY] = jnp.dot(hb, win_ref[:, 0:N_EARLY], preferred_element_type=F32)
    z_ref[:, N_EARLY:N_IN] = jnp.dot(hb, win_ref[:, N_EARLY:N_IN], preferred_element_type=F32)

    convw = convw_ref[...]
    for b in range(nb):
        xp_ref[b, SUB - (DN_CONV - 1):SUB, :] = convo_ref[b]
        xp_ref[b, SUB:SUB + tb, :] = z_ref[b * tb:(b + 1) * tb, Z_DNQKV:Z_DNQKV + DN_CONV_CH]
        acc = xp_ref[b, SUB - 3:SUB - 3 + tb, :] * convw[0:1, :]
        for j in range(1, DN_CONV):
            acc = acc + xp_ref[b, SUB - 3 + j:SUB - 3 + j + tb, :] * convw[j:j + 1, :]
        qkv_ref[b * tb:(b + 1) * tb, :] = _silu(acc)
        convo_ref[b] = xp_ref[b, SUB + tb - (DN_CONV - 1):SUB + tb, :]

    sp = sp_ref[...]
    gn = gn_ref[...]
    lbl = lbl_ref[...]
    lbs = lbs_ref[...]
    lbe = jnp.exp(lbl - jnp.max(lbl, axis=0, keepdims=True))
    lb = jnp.sum(lbs * (lbe / jnp.sum(lbe, axis=0, keepdims=True)), axis=0, keepdims=True)
    log_lb = jnp.log(lb)
    log_1mlb = jnp.log1p(-lb)
    neg_a = -jnp.exp(sp[1:2, :])

    zf = z_ref[:, Z_HGF:Z_HGF + GROUP_W]
    t2 = log_1mlb + _log_sigmoid(zf)
    mx = jnp.maximum(log_lb, t2)
    z_ref[:, Z_HGF:Z_HGF + GROUP_W] = (mx + jnp.log(jnp.exp(log_lb - mx) + jnp.exp(t2 - mx))) * LOG2E
    hgk_ref[...] = (1.0 - lb) * _sigmoid(-zf)
    z_ref[:, Z_HGQ:Z_HGQ + GROUP_W] = _silu(z_ref[:, Z_HGQ:Z_HGQ + GROUP_W])
    for m, col in enumerate((Z_DNG, Z_HGG, Z_GLAG)):
        z_ref[:, col:col + GROUP_W] = _silu(z_ref[:, col:col + GROUP_W]) * gn[m:m + 1, :]
    z_ref[:, Z_MLO:Z_MLO + GROUP_W] = _sigmoid(z_ref[:, Z_MLO:Z_MLO + GROUP_W]) * gn[3:4, :]

    kc = dict(masks=masks_ref, e_c=ec_ref, e_d=ed_ref, bd256=bd256_ref, bd128=bd128_ref, bdp=bdp_ref,
              st256=st256_ref, st128=st128_ref, ie256=ie256_ref, ie128=ie128_ref, iep=iep_ref,
              ones256=ones256_ref, mean256=mean256_ref, lmat=lmat_ref)
    run = min(ilv, nchunk)
    assert ilv % run == 0 and nchunk % run == 0

    def chunk_gens(it, u, boxes):
        i = it * ilv + u
        pos = u % run
        if nb == 1:
            b = 0
        elif ilv % nchunk == 0:
            b = it * (ilv // nchunk) + u // nchunk
        else:
            b = i // nchunk
        rs = pl.ds(pl.multiple_of(i * c, c), c)
        zc = lambda off, w: z_ref[rs, off:off + w]

        def getter(key, read):
            def get():
                if pos > 0:
                    return (yield from _await(boxes, (key, u - 1)))
                return read()
                yield
            return get

        def dn():
            small = zc(Z_SMALL, LANES)
            sb = small + sp[0:1, :]
            qkv = qkv_ref[rs, :]
            cq, ck, cv = qkv[:, 0:GROUP_W], qkv[:, GROUP_W:2 * GROUP_W], qkv[:, 2 * GROUP_W:3 * GROUP_W]
            ss = _dot(jnp.concatenate([cq * cq, ck * ck], axis=0), kc['ones256'][...])
            ssq, ssk = ss[0:c], ss[c:2 * c]
            beta_s = _sigmoid(small)
            gam_s = _move_l(kc['lmat'][0:c, :], neg_a * _softplus(sb))
            yield
            dq = cq * lax.rsqrt(ssq + EPS) * (DN_DK ** -0.5)
            dk = ck * lax.rsqrt(ssk + EPS)
            o, st_new = yield from _deltanet_chunk(dq, dk, cv, beta_s, gam_s,
                                                   getter('dn', lambda: sdn_ref[b]), kc, c)
            boxes[('dn', u)] = st_new
            sdn_ref[b] = st_new
            boxes[('out', u, 0)] = o

        def hg():
            o, st_new = yield from _gla_chunk(zc(Z_HGQ, GROUP_W), hgk_ref[rs, :], zc(Z_HGI, GROUP_W),
                                              zc(Z_HGF, GROUP_W), getter('hg', lambda: shg_ref[b]), kc, c, True)
            boxes[('hg', u)] = st_new
            shg_ref[b] = st_new
            boxes[('out', u, 1)] = o

        def gla():
            small = zc(Z_SMALL, LANES)
            g_gla = _log_sigmoid(_dot(small, wup_ref[...]) + bup_ref[...]) * (LOG2E / GLA_TAU)
            yield
            o, st_new = yield from _gla_chunk(zc(Z_GLAQ, GLA_W) * (GLA_DK ** -0.5), zc(Z_GLAK, GLA_W),
                                              zc(Z_GLAV, GROUP_W), g_gla,
                                              getter('gla', lambda: sgla_ref[b]), kc, c, False)
            boxes[('gla', u)] = st_new
            sgla_ref[b] = st_new
            boxes[('out', u, 2)] = o

        def ml():
            sb = zc(Z_SMALL, LANES) + sp[0:1, :]
            ig_s = pltpu.roll(sb, S_MLF - S_MLI, axis=1)
            lf_s = _log_sigmoid(sb)

            def put_m(m_new):
                boxes[('ml_m', u)] = m_new
                mml_ref[b] = m_new

            hh, c_new, n_new = yield from _mlstm_chunk(
                zc(Z_MLQ, GROUP_W) * (ML_DK ** -0.5), zc(Z_MLK, GROUP_W), zc(Z_MLV, GROUP_W), ig_s, lf_s,
                getter('ml_m', lambda: mml_ref[b]), put_m,
                getter('ml_cn', lambda: (cml_ref[b], nml_ref[b])), kc, c)
            boxes[('ml_cn', u)] = (c_new, n_new)
            cml_ref[b] = c_new
            nml_ref[b] = n_new
            boxes[('out', u, 3)] = hh

        late = STAGGER * pos
        return [(0, dn()), (late, hg()), (late, gla()), (late, ml())]

    def chunk_body(it, carry):
        boxes = {}
        gens = []
        for u in range(ilv):
            gens += chunk_gens(it, u, boxes)
        _run_interleaved(gens)
        gate_cols = (Z_DNG, Z_HGG, Z_GLAG, Z_MLO)
        half = max(ilv // 2, 1)
        for u0 in range(0, ilv, half):
            us = range(u0, u0 + half)
            outs = jnp.concatenate([boxes[('out', u, m)] for u in us for m in range(4)], axis=0)
            ms = _dot(outs * outs, kc['mean256'][...])
            normed = outs * lax.rsqrt(ms + EPS)
            for u in us:
                rs = pl.ds(pl.multiple_of((it * ilv + u) * c, c), c)
                for m in range(4):
                    gate = z_ref[rs, gate_cols[m]:gate_cols[m] + GROUP_W]
                    r0 = ((u - u0) * 4 + m) * c
                    mix_ref[rs, m * GROUP_W:(m + 1) * GROUP_W] = (normed[r0:r0 + c] * gate).astype(BF16)
        return carry

    lax.fori_loop(0, nb * nchunk // ilv, chunk_body, 0)

    @pl.when(ti == pl.num_programs(1) - 1)
    def _():
        for _, out_ref, st_ref in mats:
            dk = out_ref.shape[2]
            for b in range(nb):
                st = st_ref[b]
                if st_ref is sdn_ref:
                    raw2d = sum(st[:, hd * HEAD_V:(hd + 1) * HEAD_V] for hd in range(N_HEADS))
                else:
                    raw2d = sum(st[hd * HEAD_V:(hd + 1) * HEAD_V, :] for hd in range(N_HEADS)).T
                out_ref[b] = raw2d.reshape(N_HEADS, dk, HEAD_V)

    out = jnp.dot(mix_ref[...], wout_ref[...], preferred_element_type=F32).reshape(nb, tb, D_MODEL)
    y_ref[...] = x + mod[:, 2:3, :] * out


def _const_tables(c):
    pc = N_HEADS * c
    t = np.arange(c)[:, None]
    lane = np.arange(pc)[None, :]
    hs, s = lane // c, lane % c
    masks = [s <= t, s < t, s == t, (s // TRI_BLOCK) == (t // TRI_BLOCK)]
    sz = c // 2
    while sz >= 1:
        masks.append(((s // (2 * sz)) == (t // (2 * sz))) & ((t // sz) % 2 == 1) & ((s // sz) % 2 == 0))
        sz //= 2
    masks = np.stack([np.broadcast_to(m, (c, pc)) for m in masks]).astype(np.float32)
    r = np.arange(c)[None, :]
    tri = (r <= t).astype(np.float32)
    lmat = np.concatenate([tri, tri - tri[(np.arange(c) // 4) * 4 + 1]], axis=0)

    def expand(col0, w):
        j = np.arange(LANES)[:, None]
        l = np.arange(N_HEADS * w)[None, :]
        return (j == col0 + l // w).astype(np.float32)

    e_c = np.stack([expand(S_DNB, c), expand(S_DNA, c), expand(S_MLF, c)])
    e_d = np.stack([expand(S_DNB, HEAD_V), expand(S_DNA, HEAD_V), expand(S_MLF, HEAD_V)])

    def blk(nr, rg, nl, lg):
        return ((np.arange(nr)[:, None] // rg) == (np.arange(nl)[None, :] // lg)).astype(np.float32)

    tabs = dict(
        masks=jnp.asarray(masks), e_c=jnp.asarray(e_c, BF16), e_d=jnp.asarray(e_d, BF16),
        bd256=jnp.asarray(blk(pc, c, GROUP_W, HEAD_V), BF16), bd128=jnp.asarray(blk(pc, c, GLA_W, GLA_DK), BF16),
        bdp=jnp.asarray(blk(pc, c, pc, c), BF16),
        st256=jnp.asarray(blk(GROUP_W, HEAD_V, GROUP_W, HEAD_V)), st128=jnp.asarray(blk(GROUP_W, HEAD_V, GLA_W, GLA_DK)),
        ie256=jnp.asarray(blk(GROUP_W, HEAD_V, pc, c), BF16), ie128=jnp.asarray(blk(GLA_W, GLA_DK, pc, c), BF16),
        iep=jnp.asarray(blk(pc, c, GROUP_W, HEAD_V), BF16),
        ones256=jnp.asarray(blk(GROUP_W, HEAD_V, GROUP_W, HEAD_V), BF16),
        mean256=jnp.asarray(blk(GROUP_W, HEAD_V, GROUP_W, HEAD_V) / HEAD_V, BF16), lmat=jnp.asarray(lmat, BF16))
    order = ['masks', 'e_c', 'e_d', 'bd256', 'bd128', 'bdp', 'st256', 'st128', 'ie256', 'ie128', 'iep', 'ones256',
             'mean256', 'lmat']
    return [tabs[k] for k in order]


def _full_spec(a):
    nd = a.ndim
    return pl.BlockSpec(a.shape, lambda bi, ti, _n=nd: (0,) * _n, pipeline_mode=pl.Buffered(1))


def _mixer_call(x, mod, lw, states, nb, tb):
    bsz, t, _ = x.shape
    c = min(CHUNK, t)
    assert t % tb == 0 and tb % c == 0 and bsz % nb == 0 and c % TRI_BLOCK == 0
    rows = nb * tb
    consts = _const_tables(c)
    params = [lw['g_mix'], lw['w_in'], lw['w_out'], lw['conv_w'], lw['sp'], lw['gn'], lw['lb_logits'], lw['lb_sel'],
              lw['wup'], lw['bup']]
    xspec = pl.BlockSpec((nb, tb, D_MODEL), lambda bi, ti: (bi, ti, 0))

    def bspec(a):
        nd = a.ndim
        return pl.BlockSpec((nb,) + a.shape[1:], lambda bi, ti, _n=nd: (bi,) + (0,) * (_n - 1))

    in_specs = ([xspec, bspec(mod)] + [_full_spec(a) for a in params] + [_full_spec(a) for a in consts]
                + [bspec(s) for s in states])
    out_shape = [jax.ShapeDtypeStruct(x.shape, F32)] + [jax.ShapeDtypeStruct(s.shape, F32) for s in states]
    out_specs = [xspec] + [bspec(s) for s in states]
    scratch = [pltpu.VMEM((rows, N_IN), F32),
               pltpu.VMEM((nb, SUB + tb, DN_CONV_CH), F32),
               pltpu.VMEM((rows, DN_CONV_CH), F32),
               pltpu.VMEM((rows, MIX_W), BF16),
               pltpu.VMEM((rows, N_HEADS * HG_DK), F32),
               pltpu.VMEM((nb, GROUP_W, N_HEADS * DN_DK), F32),
               pltpu.VMEM((nb, GROUP_W, N_HEADS * HG_DK), F32),
               pltpu.VMEM((nb, GROUP_W, GLA_W), F32),
               pltpu.VMEM((nb, GROUP_W, N_HEADS * ML_DK), F32)]
    ilv = MAX_INTERLEAVE
    while (nb * (tb // c)) % ilv:
        ilv //= 2
    kern = functools.partial(_mixer_kernel, nb=nb, tb=tb, c=c, ilv=ilv)
    return pl.pallas_call(
        kern, grid=(bsz // nb, t // tb), in_specs=in_specs, out_specs=out_specs, out_shape=out_shape,
        scratch_shapes=scratch, name='mixer',
        compiler_params=pltpu.CompilerParams(dimension_semantics=('arbitrary', 'arbitrary'),
                                             vmem_limit_bytes=VMEM_LIMIT),
    )(x, mod, *params, *consts, *states)


FF_TILE = 256


def _ffn_kernel(x_ref, mod_ref, gffn_ref, wup_ref, wdown_ref, gfin_ref, y_ref, *, nb, tb, final):
    rows = nb * tb
    x = x_ref[...]
    mod = mod_ref[...]
    h = _rmsnorm_rows(x, gffn_ref[...]) * (1.0 + mod[:, 4:5, :]) + mod[:, 3:4, :]
    hb = h.reshape(rows, D_MODEL).astype(BF16)
    acc = jnp.zeros((rows, D_MODEL), F32)
    for j in range(D_FF // FF_TILE):
        gate = jnp.dot(hb, wup_ref[:, j * FF_TILE:(j + 1) * FF_TILE], preferred_element_type=F32)
        up = jnp.dot(hb, wup_ref[:, D_FF + j * FF_TILE:D_FF + (j + 1) * FF_TILE], preferred_element_type=F32)
        act = (_silu(gate) * up).astype(BF16)
        acc = acc + jnp.dot(act, wdown_ref[j * FF_TILE:(j + 1) * FF_TILE, :], preferred_element_type=F32)
    y = x + mod[:, 5:6, :] * acc.reshape(nb, tb, D_MODEL)
    if final:
        y = _rmsnorm_rows(y, gfin_ref[...])
    y_ref[...] = y


def _ffn_call(x, mod, lw, g_final, nb, tb, final):
    bsz, t, _ = x.shape
    xspec = pl.BlockSpec((nb, tb, D_MODEL), lambda bi, ti: (bi, ti, 0))
    mspec = pl.BlockSpec((nb, 6, D_MODEL), lambda bi, ti: (bi, 0, 0))
    params = [lw['g_ffn'], lw['w_up'], lw['w_down'], g_final]
    kern = functools.partial(_ffn_kernel, nb=nb, tb=tb, final=final)
    return pl.pallas_call(
        kern, grid=(bsz // nb, t // tb), in_specs=[xspec, mspec] + [_full_spec(a) for a in params],
        out_specs=xspec, out_shape=jax.ShapeDtypeStruct(x.shape, F32), name='ffn',
        compiler_params=pltpu.CompilerParams(dimension_semantics=('arbitrary', 'arbitrary'),
                                             vmem_limit_bytes=VMEM_LIMIT),
    )(x, mod, *params)


ADA_TILE = 1536


def _ada_kernel(c_ref, w_ref, b_ref, o_ref):
    o_ref[0] = jnp.dot(_silu(c_ref[...]).astype(BF16), w_ref[0], preferred_element_type=F32) + b_ref[0]


def _ada_call(c_all, w_ada, b_ada):
    n = c_all.shape[0]
    nt = 6 * D_MODEL // ADA_TILE
    return pl.pallas_call(
        _ada_kernel, grid=(DEPTH, nt),
        in_specs=[pl.BlockSpec((n, D_MODEL), lambda l, j: (0, 0)),
                  pl.BlockSpec((1, D_MODEL, ADA_TILE), lambda l, j: (l, 0, j)),
                  pl.BlockSpec((1, 1, ADA_TILE), lambda l, j: (l, 0, j))],
        out_specs=pl.BlockSpec((1, n, ADA_TILE), lambda l, j: (l, 0, j)),
        out_shape=jax.ShapeDtypeStruct((DEPTH, n, 6 * D_MODEL), F32), name='ada',
        compiler_params=pltpu.CompilerParams(dimension_semantics=('arbitrary', 'arbitrary')),
    )(c_all, w_ada, b_ada.reshape(DEPTH, 1, 6 * D_MODEL))


_REF_SPLITS = (('dn_qkv', DN_CONV_CH), ('dn_b', N_HEADS), ('dn_a', N_HEADS), ('dn_g', GROUP_W),
               ('hg_q', GROUP_W), ('hg_f', GROUP_W), ('hg_i', GROUP_W), ('hg_g', GROUP_W),
               ('gla_q', GLA_W), ('gla_k', GLA_W), ('gla_v', GROUP_W), ('gla_r', GLA_RANK), ('gla_g', GROUP_W),
               ('ml_q', GROUP_W), ('ml_k', GROUP_W), ('ml_v', GROUP_W), ('ml_i', N_HEADS), ('ml_f', N_HEADS),
               ('ml_o', GROUP_W))


def _permute_w_in(w):
    d = w.shape[0]
    cols, off = {}, 0
    for name, n in _REF_SPLITS:
        cols[name] = w[:, off:off + n]
        off += n
    assert off == w.shape[1]
    pieces, lane = [], 0
    for name, start in (('dn_b', S_DNB), ('gla_r', S_GLAR), ('dn_a', S_DNA), ('ml_i', S_MLI), ('ml_f', S_MLF)):
        pieces += [jnp.zeros((d, start - lane), w.dtype), cols[name]]
        lane = start + cols[name].shape[1]
    small = jnp.concatenate(pieces + [jnp.zeros((d, LANES - lane), w.dtype)], axis=1)
    cols['small'] = small
    order = ('dn_qkv', 'small', 'hg_f', 'hg_q', 'dn_g', 'hg_g', 'gla_g', 'ml_o',
             'hg_i', 'gla_q', 'gla_k', 'gla_v', 'ml_q', 'ml_k', 'ml_v')
    out = jnp.concatenate([cols[k] for k in order], axis=1)
    assert out.shape[1] == N_IN
    return out


def _lane_row(pairs, width=LANES):
    row = jnp.zeros((width,), F32)
    for off, val in pairs:
        row = row.at[off:off + val.shape[0]].set(val.astype(F32))
    return row


def _layer_weights(p, l):
    sp = jnp.zeros((SUB, LANES), F32)
    sp = sp.at[0].set(_lane_row([(S_DNA, p['dn_dt_bias'][l]), (S_MLI, p['ml_i_bias'][l]), (S_MLF, p['ml_f_bias'][l])]))
    sp = sp.at[1].set(_lane_row([(S_DNA, p['dn_a_log'][l])]))
    gn = jnp.stack([jnp.tile(p[k][l].astype(F32), N_HEADS) for k in ('dn_norm_g', 'hg_norm_g', 'gla_norm_g', 'ml_norm_g')])
    wup = jnp.zeros((LANES, GLA_W), F32).at[S_GLAR:S_GLAR + GLA_RANK].set(p['gla_w_up'][l]).astype(BF16)
    lb_sel = (jnp.arange(DEPTH) >= 1) & (jnp.arange(DEPTH) <= l)
    return dict(
        g_mix=p['g_mix'][l].reshape(1, D_MODEL), g_ffn=p['g_ffn'][l].reshape(1, D_MODEL),
        w_in=_permute_w_in(p['w_in'][l]).astype(BF16), w_out=p['w_out'][l].astype(BF16),
        conv_w=p['dn_conv_w'][l], sp=sp, gn=gn, lb_logits=p['hg_lb_logits'].astype(F32),
        lb_sel=lb_sel.astype(F32).reshape(DEPTH, 1), wup=wup, bup=p['gla_b_up'][l].reshape(1, GLA_W).astype(F32),
        w_up=p['w_up'][l].astype(BF16), w_down=p['w_down'][l].astype(BF16))


def _trunk(x, mods, states, lws, g_final, nb, tb):
    new_states = []
    for l in range(DEPTH):
        outs = _mixer_call(x, mods[l], lws[l], states[l], nb, tb)
        x = outs[0]
        new_states.append(outs[1:])
        x = _ffn_call(x, mods[l], lws[l], g_final, *_tiling(x.shape[0], x.shape[1], FFN_ROWS, FFN_ROWS), l == DEPTH - 1)
    return x, new_states


def _pack_states(conv, s_dn, s_hg, s_gla, c_ml, n_ml, m_ml, l):
    b = conv.shape[1]
    m_row = jnp.zeros((b, 1, LANES), F32).at[:, 0, S_MLF:S_MLF + N_HEADS].set(m_ml[l].astype(F32))
    return (conv[l].astype(F32), s_dn[l].astype(F32), s_hg[l].astype(F32), s_gla[l].astype(F32),
            c_ml[l].astype(F32), n_ml[l].astype(F32).reshape(b, 1, N_HEADS * ML_DK), m_row)


def _unpack_states(sts):
    conv, s_dn, s_hg, s_gla, c_ml = (jnp.stack([s[i] for s in sts]) for i in range(5))
    n_ml = jnp.stack([s[5].reshape(s[5].shape[0], N_HEADS, ML_DK) for s in sts])
    m_ml = jnp.stack([s[6][:, 0, S_MLF:S_MLF + N_HEADS] for s in sts])
    return conv, s_dn, s_hg, s_gla, c_ml, n_ml, m_ml


def _zero_states(b):
    z = lambda *s: jnp.zeros(s, F32)
    return (z(b, DN_CONV - 1, DN_CONV_CH), z(b, N_HEADS, DN_DK, HEAD_V), z(b, N_HEADS, HG_DK, HEAD_V),
            z(b, N_HEADS, GLA_DK, HEAD_V), z(b, N_HEADS, ML_DK, HEAD_V), z(b, 1, N_HEADS * ML_DK), z(b, 1, LANES))


def _tiling(bsz, t, rows, tokens):
    tb = min(t, tokens)
    nb = max(1, min(bsz, rows // tb))
    while bsz % nb:
        nb -= 1
    return nb, tb


def kernel(x_prompt, x_sample, c_prompt, c_sample, cache_dn_conv, state_dn, state_hgrn, state_gla, state_mlstm_c, state_mlstm_n, state_mlstm_m, w_ada, b_ada, g_mix, g_ffn, w_in, dn_conv_w, dn_a_log, dn_dt_bias, dn_norm_g, hg_lb_logits, hg_norm_g, gla_w_up, gla_b_up, gla_norm_g, ml_i_bias, ml_f_bias, ml_norm_g, w_out, w_up, w_down, g_final):
    p = dict(g_mix=g_mix, g_ffn=g_ffn, w_in=w_in, dn_conv_w=dn_conv_w, dn_a_log=dn_a_log, dn_dt_bias=dn_dt_bias,
             dn_norm_g=dn_norm_g, hg_lb_logits=hg_lb_logits, hg_norm_g=hg_norm_g, gla_w_up=gla_w_up,
             gla_b_up=gla_b_up, gla_norm_g=gla_norm_g, ml_i_bias=ml_i_bias, ml_f_bias=ml_f_bias,
             ml_norm_g=ml_norm_g, w_out=w_out, w_up=w_up, w_down=w_down)
    lws = [_layer_weights(p, l) for l in range(DEPTH)]
    gfin = g_final.reshape(1, D_MODEL).astype(F32)
    bp, bs = x_prompt.shape[0], x_sample.shape[0]
    mod = _ada_call(jnp.concatenate([c_prompt, c_sample], axis=0).astype(F32), w_ada.astype(BF16),
                    b_ada.astype(F32)).reshape(DEPTH, bp + bs, 6, D_MODEL)

    outs = []
    raw = (cache_dn_conv, state_dn, state_hgrn, state_gla, state_mlstm_c, state_mlstm_n, state_mlstm_m)
    for x, lo, hi, states in ((x_prompt, 0, bp, [_zero_states(bp)] * DEPTH),
                              (x_sample, bp, bp + bs, [_pack_states(*raw, l) for l in range(DEPTH)])):
        nb, tb = _tiling(x.shape[0], x.shape[1], MIXER_ROWS, MIXER_TOKENS)
        y, new = _trunk(x.astype(F32), [mod[l, lo:hi] for l in range(DEPTH)], states, lws, gfin, nb, tb)
        outs.append((y, _unpack_states(new)))
    (y_p, st_p), (y_s, st_s) = outs
    return (y_p, y_s) + tuple(st_p) + tuple(st_s)
```
